```python
import functools
import jax, jax.numpy as jnp
from jax import lax
import numpy as np

D_MODEL = 1024
BATCH = 1
SEQ = 16384
DEPTH = 1
DEC_BATCH = 32
DEC_SEQ = 1
PAST_LEN = 16384
PAGE_SIZE = 128

HEAD_DIM = 64
HEADS_PER_GROUP = 8
GROUPS = ((128, 1), (512, 4), (2048, 16))
N_GROUPS = len(GROUPS)
ATTN_HEADS = N_GROUPS * HEADS_PER_GROUP
QKV_WIDTH = ATTN_HEADS * HEAD_DIM
ATTN_OUT = HEADS_PER_GROUP * HEAD_DIM
BLOCK = 128
ATTN_SCALE = HEAD_DIM ** -0.5
NEG_INF = -1e30
D_RNN = 1280
RNN_BLOCKS = 10
RNN_BLOCK_W = D_RNN // RNN_BLOCKS
CONV_W = 4
LRU_C = 8.0
D_FF = -(-8 * D_MODEL // (3 * 256)) * 256
RMS_EPS = 1e-6

IN_WIDTHS = (QKV_WIDTH, QKV_WIDTH, QKV_WIDTH, D_RNN, D_RNN, D_MODEL, D_MODEL)
IN_WIDTH = sum(IN_WIDTHS)
SPLIT_POINTS = tuple(int(c) for c in np.cumsum(IN_WIDTHS)[:-1])

kernel_name = "hybrid_dilated_swa_rglru_decode_step"


def rms_norm(x, g):
    xf = x.astype(jnp.float32)
    y = xf * lax.rsqrt(jnp.mean(xf * xf, axis=-1, keepdims=True) + RMS_EPS)
    return (y * g.astype(jnp.float32)).astype(x.dtype)


def dilated_group_prompt(q, k, v, window, dilation):
    b, s, h, dh = q.shape
    span = window // dilation
    unit = dilation * BLOCK
    sp = -(-s // unit) * unit
    m_len = sp // dilation
    nb = m_len // BLOCK

    def to_blocks(t):
        t = jnp.pad(t, ((0, 0), (0, sp - s), (0, 0), (0, 0)))
        t = t.reshape(b, m_len, dilation, h, dh).transpose(0, 2, 1, 3, 4)
        return t.reshape(b * dilation, nb, BLOCK, h, dh)

    def with_prev(t):
        prev = jnp.pad(t[:, :-1], ((0, 0), (1, 0), (0, 0), (0, 0), (0, 0)))
        return jnp.concatenate([prev, t], axis=2)

    def from_blocks(t):
        t = t.reshape((b, dilation, m_len) + t.shape[3:])
        t = jnp.swapaxes(t, 1, 2).reshape((b, sp) + t.shape[3:])
        return t[:, :s]

    qb = to_blocks(q)
    kc = with_prev(to_blocks(k))
    vc = with_prev(to_blocks(v))
    qi = jnp.arange(BLOCK)[:, None]
    kj = jnp.arange(2 * BLOCK)[None, :]
    dist = BLOCK + qi - kj
    blk = jnp.arange(nb)[:, None, None]
    valid = (dist >= 0) & (dist <= span) & ((blk > 0) | (kj >= BLOCK))
    scores = jnp.einsum('nbqhd,nbkhd->nbhqk', qb, kc,
                        preferred_element_type=jnp.float32) * ATTN_SCALE
    scores = jnp.where(valid[None, :, None], scores, NEG_INF)
    mx = jnp.max(scores, axis=-1)
    p = jnp.exp(scores - mx[..., None])
    den = jnp.sum(p, axis=-1)
    numer = jnp.einsum('nbhqk,nbkhd->nbqhd', p, vc.astype(jnp.float32))
    return (from_blocks(numer), from_blocks(jnp.swapaxes(mx, 2, 3)),
            from_blocks(jnp.swapaxes(den, 2, 3)))


def dilated_group_sample(q, k_full, v_full, past_rows, window, dilation):
    t = q.shape[1]
    span = window // dilation
    idx = past_rows + jnp.arange(t)[:, None] - dilation * jnp.arange(span + 1)[None, :]
    valid = idx >= 0
    idx = jnp.maximum(idx, 0)
    kg = k_full[:, idx]
    vg = v_full[:, idx]
    scores = jnp.einsum('bthd,btjhd->bthj', q, kg,
                        preferred_element_type=jnp.float32) * ATTN_SCALE
    scores = jnp.where(valid[None, :, None, :], scores, NEG_INF)
    mx = jnp.max(scores, axis=-1)
    p = jnp.exp(scores - mx[..., None])
    den = jnp.sum(p, axis=-1)
    numer = jnp.einsum('bthj,btjhd->bthd', p, vg.astype(jnp.float32))
    return numer, mx, den


def combine_groups(parts):
    numers, maxes, dens = zip(*parts)
    m_all = jnp.max(jnp.stack(maxes), axis=0)
    w = [jnp.exp(m - m_all) for m in maxes]
    num = sum(n * wi[..., None] for n, wi in zip(numers, w))
    den = sum(d * wi for d, wi in zip(dens, w))
    return num / den[..., None]


def group_slice(t, g):
    return t[:, :, g * HEADS_PER_GROUP:(g + 1) * HEADS_PER_GROUP]


def prompt_attention(q, k, v):
    parts, rows = [], []
    s = q.shape[1]
    for g, (window, dilation) in enumerate(GROUPS):
        qg, kg, vg = group_slice(q, g), group_slice(k, g), group_slice(v, g)
        parts.append(dilated_group_prompt(qg, kg, vg, window, dilation))
        keep = min(window, s)
        rows.append(jnp.stack([kg[:, s - keep:], vg[:, s - keep:]], axis=2))
    return combine_groups(parts), rows


def sample_attention(q, k, v, caches):
    parts, rows = [], []
    for g, (window, dilation) in enumerate(GROUPS):
        qg, kg, vg = group_slice(q, g), group_slice(k, g), group_slice(v, g)
        cache = caches[g]
        past_rows = cache.shape[1]
        k_full = jnp.concatenate([cache[:, :, 0].astype(kg.dtype), kg], axis=1)
        v_full = jnp.concatenate([cache[:, :, 1].astype(vg.dtype), vg], axis=1)
        parts.append(dilated_group_sample(qg, k_full, v_full, past_rows, window, dilation))
        rows.append(jnp.stack([kg, vg], axis=2))
    return combine_groups(parts), rows


def linear_recurrence(a, b, h0):
    b = b.at[:, 0].add(a[:, 0] * h0)

    def combine(left, right):
        a_l, b_l = left
        a_r, b_r = right
        return a_l * a_r, a_r * b_l + b_r

    _, h = lax.associative_scan(combine, (a, b), axis=1)
    return h


def hybrid_layer(x, attention, conv_buf, h0, norm1_g, w_in, b_gate, q_norm_g, k_norm_g,
                 conv_w, conv_b, w_rg, b_rg, w_ig, b_ig, lru_lambda, w_o_attn, w_o_rnn,
                 w_out, norm2_g, w_ffn_in, w_ffn_out):
    bsz, t, _ = x.shape
    hn = rms_norm(x, norm1_g)
    q, k, v, xb, gb, gate_a, gate_b = jnp.split(hn @ w_in, SPLIT_POINTS, axis=-1)
    q = rms_norm(q.reshape(bsz, t, ATTN_HEADS, HEAD_DIM), q_norm_g)
    k = rms_norm(k.reshape(bsz, t, ATTN_HEADS, HEAD_DIM), k_norm_g)
    v = v.reshape(bsz, t, ATTN_HEADS, HEAD_DIM)
    attn, kv_rows = attention(q, k, v)
    y_a = attn.reshape(bsz, t, ATTN_OUT).astype(x.dtype) @ w_o_attn
    xpad = jnp.concatenate([conv_buf.astype(xb.dtype), xb], axis=1)
    xc = sum(xpad[:, j:j + t] * conv_w[j] for j in range(CONV_W)) + conv_b
    new_conv = xpad[:, t:]
    xblk = xc.reshape(bsz, t, RNN_BLOCKS, RNN_BLOCK_W)
    r = jax.nn.sigmoid((jnp.einsum('btnc,ncd->btnd', xblk, w_rg).reshape(bsz, t, D_RNN)
                        + b_rg).astype(jnp.float32))
    i = jax.nn.sigmoid((jnp.einsum('btnc,ncd->btnd', xblk, w_ig).reshape(bsz, t, D_RNN)
                        + b_ig).astype(jnp.float32))
    log_a = -LRU_C * r * jax.nn.softplus(-lru_lambda.astype(jnp.float32))
    a = jnp.exp(log_a)
    b_in = jnp.sqrt(-jnp.expm1(2.0 * log_a)) * i * xc.astype(jnp.float32)
    h = linear_recurrence(a, b_in, h0.astype(jnp.float32))
    h_last = h[:, -1]
    y_b = (h * jax.nn.gelu(gb.astype(jnp.float32))).astype(x.dtype) @ w_o_rnn
    merged = jax.nn.sigmoid(gate_a + b_gate[0]) * y_a + jax.nn.sigmoid(gate_b + b_gate[1]) * y_b
    x = x + merged @ w_out
    g, u = jnp.split(rms_norm(x, norm2_g) @ w_ffn_in, 2, axis=-1)
    x = x + (jax.nn.silu(g) * u) @ w_ffn_out
    return x, kv_rows, new_conv, h_last


def setup_inputs(seed: int = 0) -> dict:
    key = jax.random.key(seed)
    ks = jax.random.split(key, 32)
    f32 = jnp.float32
    nrm = lambda k, shape: jax.random.normal(k, shape, f32)
    x_prompt = nrm(ks[0], (BATCH, SEQ, D_MODEL))
    x_sample = nrm(ks[1], (DEC_BATCH, DEC_SEQ, D_MODEL))
    caches = [nrm(ks[2 + g], (DEPTH, DEC_BATCH, min(w, PAST_LEN), 2, HEADS_PER_GROUP, HEAD_DIM))
              for g, (w, _) in enumerate(GROUPS)]
    state_conv = nrm(ks[5], (DEPTH, DEC_BATCH, CONV_W - 1, D_RNN))
    state_h = 0.5 * nrm(ks[6], (DEPTH, DEC_BATCH, D_RNN))
    a0 = jax.random.uniform(ks[7], (DEPTH, D_RNN), f32, 0.9, 0.999) ** (1.0 / LRU_C)
    lru_lambda = jnp.log(a0) - jnp.log1p(-a0)
    return {
        "x_prompt": x_prompt,
        "x_sample": x_sample,
        "cache_kv_w128": caches[0],
        "cache_kv_w512": caches[1],
        "cache_kv_w2048": caches[2],
        "state_conv": state_conv,
        "state_h": state_h,
        "norm1_g": 1.0 + 0.02 * nrm(ks[8], (DEPTH, D_MODEL)),
        "w_in": nrm(ks[9], (DEPTH, D_MODEL, IN_WIDTH)) * D_MODEL ** -0.5,
        "b_gate": 0.02 * nrm(ks[10], (DEPTH, 2, D_MODEL)),
        "q_norm_g": 1.0 + 0.02 * nrm(ks[11], (DEPTH, HEAD_DIM)),
        "k_norm_g": 1.0 + 0.02 * nrm(ks[12], (DEPTH, HEAD_DIM)),
        "conv_w": nrm(ks[13], (DEPTH, CONV_W, D_RNN)) * CONV_W ** -0.5,
        "conv_b": 0.02 * nrm(ks[14], (DEPTH, D_RNN)),
        "w_rg": nrm(ks[15], (DEPTH, RNN_BLOCKS, RNN_BLOCK_W, RNN_BLOCK_W)) * RNN_BLOCK_W ** -0.5,
        "b_rg": 0.02 * nrm(ks[16], (DEPTH, D_RNN)),
        "w_ig": nrm(ks[17], (DEPTH, RNN_BLOCKS, RNN_BLOCK_W, RNN_BLOCK_W)) * RNN_BLOCK_W ** -0.5,
        "b_ig": 0.02 * nrm(ks[18], (DEPTH, D_RNN)),
        "lru_lambda": lru_lambda,
        "w_o_attn": nrm(ks[19], (DEPTH, ATTN_OUT, D_MODEL)) * ATTN_OUT ** -0.5,
        "w_o_rnn": nrm(ks[20], (DEPTH, D_RNN, D_MODEL)) * D_RNN ** -0.5,
        "w_out": nrm(ks[21], (DEPTH, D_MODEL, D_MODEL)) * D_MODEL ** -0.5,
        "norm2_g": 1.0 + 0.02 * nrm(ks[22], (DEPTH, D_MODEL)),
        "w_ffn_in": nrm(ks[23], (DEPTH, D_MODEL, 2 * D_FF)) * D_MODEL ** -0.5,
        "w_ffn_out": nrm(ks[24], (DEPTH, D_FF, D_MODEL)) * D_FF ** -0.5,
    }


def reference(x_prompt, x_sample, cache_kv_w128, cache_kv_w512, cache_kv_w2048, state_conv,
              state_h, norm1_g, w_in, b_gate, q_norm_g, k_norm_g, conv_w, conv_b, w_rg, b_rg,
              w_ig, b_ig, lru_lambda, w_o_attn, w_o_rnn, w_out, norm2_g, w_ffn_in, w_ffn_out):
    y_p, y_s = x_prompt, x_sample
    kvp, kvs = [[], [], []], [[], [], []]
    conv_p, conv_s, h_p, h_s = [], [], [], []
    for layer in range(DEPTH):
        weights = (norm1_g[layer], w_in[layer], b_gate[layer], q_norm_g[layer], k_norm_g[layer],
                   conv_w[layer], conv_b[layer], w_rg[layer], b_rg[layer], w_ig[layer],
                   b_ig[layer], lru_lambda[layer], w_o_attn[layer], w_o_rnn[layer],
                   w_out[layer], norm2_g[layer], w_ffn_in[layer], w_ffn_out[layer])
        zero_conv = jnp.zeros((y_p.shape[0], CONV_W - 1, D_RNN), y_p.dtype)
        zero_h = jnp.zeros((y_p.shape[0], D_RNN), jnp.float32)
        y_p, rows_p, c_p, hl_p = hybrid_layer(y_p, prompt_attention, zero_conv, zero_h, *weights)
        sample_attn = functools.partial(
            sample_attention,
            caches=(cache_kv_w128[layer], cache_kv_w512[layer], cache_kv_w2048[layer]))
        y_s, rows_s, c_s, hl_s = hybrid_layer(y_s, sample_attn, state_conv[layer],
                                              state_h[layer], *weights)
        for g in range(N_GROUPS):
            kvp[g].append(rows_p[g])
            kvs[g].append(rows_s[g])
        conv_p.append(c_p)
        conv_s.append(c_s)
        h_p.append(hl_p)
        h_s.append(hl_s)
    return (y_p, y_s,
            jnp.stack(kvp[0]), jnp.stack(kvp[1]), jnp.stack(kvp[2]),
            jnp.stack(conv_p), jnp.stack(h_p),
            jnp.stack(kvs[0]), jnp.stack(kvs[1]), jnp.stack(kvs[2]),
            jnp.stack(conv_s), jnp.stack(h_s))
```

```python
import functools

import jax
import jax.numpy as jnp
from jax import lax
from jax.experimental import pallas as pl
from jax.experimental.pallas import tpu as pltpu

F32 = jnp.float32
BF16 = jnp.bfloat16

D_MODEL = 1024
HEAD_DIM = 64
HEADS_PER_GROUP = 8
GROUPS = ((128, 1), (512, 4), (2048, 16))
N_GROUPS = len(GROUPS)
GROUP_W = HEADS_PER_GROUP * HEAD_DIM
QKV_WIDTH = N_GROUPS * GROUP_W
BLOCK = 128
ATTN_SCALE = HEAD_DIM ** -0.5
NEG_INF = -1e30
D_RNN = 1280
RNN_BLOCKS = 10
RNN_BLOCK_W = D_RNN // RNN_BLOCKS
CONV_W = 4
LRU_C = 8.0
D_FF = 2816
RMS_EPS = 1e-6
MAX_WINDOW = max(w for w, _ in GROUPS)

LANES_V7X = 128
SUBLANES_V7X = 8
MXU_DIM_V7X = 256
VMEM_LIMIT_BYTES = 56 * 1024 * 1024

OFF_Q, OFF_K, OFF_V = 0, QKV_WIDTH, 2 * QKV_WIDTH
OFF_XB = 3 * QKV_WIDTH
OFF_GB = OFF_XB + D_RNN
OFF_GA = OFF_GB + D_RNN
OFF_GB2 = OFF_GA + D_MODEL


def _mm(a, b):
    return jnp.dot(a, b, preferred_element_type=F32)


def _rms_norm_rows(x, g):
    return x * lax.rsqrt(jnp.mean(x * x, axis=-1, keepdims=True) + RMS_EPS) * g


def _const_spec(shape):
    nd = len(shape)
    return pl.BlockSpec(shape, lambda *_: (0,) * nd, pipeline_mode=pl.Buffered(1))


def _params(sem):
    return pltpu.CompilerParams(dimension_semantics=sem, vmem_limit_bytes=VMEM_LIMIT_BYTES)


def _inproj_kernel(x_ref, g1_ref, w_ref, qg_ref, kg_ref, mavg_ref,
                   q_ref, k_ref, v_ref, xb_ref, gbr_ref, ga_ref, gb2_ref, kvt_ref,
                   *, tail_first_step):
    step = pl.program_id(0)
    hn = _rms_norm_rows(x_ref[...], g1_ref[...]).astype(BF16)
    mavg = mavg_ref[...]

    def seg(c0, width):
        return _mm(hn, w_ref[:, c0:c0 + width])

    def head_norm(t, gain):
        tt = (t * t).astype(BF16)
        ms = jnp.concatenate(
            [_mm(tt[:, c:c + MXU_DIM_V7X], mavg) for c in range(0, GROUP_W, MXU_DIM_V7X)], axis=1)
        return t * lax.rsqrt(ms + RMS_EPS) * gain

    for g in range(N_GROUPS):
        c = g * GROUP_W
        qn = head_norm(seg(OFF_Q + c, GROUP_W), qg_ref[...]) * ATTN_SCALE
        kn = head_norm(seg(OFF_K + c, GROUP_W), kg_ref[...])
        vv = seg(OFF_V + c, GROUP_W)
        q_ref[g] = qn.astype(q_ref.dtype)
        k_ref[g] = kn.astype(k_ref.dtype)
        v_ref[g] = vv.astype(v_ref.dtype)

        @pl.when(step >= tail_first_step)
        def _():
            kvt_ref[g, :, 0:GROUP_W] = kn
            kvt_ref[g, :, GROUP_W:2 * GROUP_W] = vv

    xb_ref[...] = seg(OFF_XB, D_RNN)
    gbr_ref[...] = seg(OFF_GB, D_RNN)
    ga_ref[...] = seg(OFF_GA, D_MODEL)
    gb2_ref[...] = seg(OFF_GB2, D_MODEL)


def _inproj(x, g1, w_in, qg, kg, mavg, *, tm, tail, qkv_dtype):
    m = x.shape[0]
    nt = m // tm
    tail_first_step = nt - tail // tm
    row = lambda w: pl.BlockSpec((tm, w), lambda i: (i, 0))
    grp = pl.BlockSpec((N_GROUPS, tm, GROUP_W), lambda i: (0, i, 0))
    kvt_spec = pl.BlockSpec((N_GROUPS, tm, 2 * GROUP_W),
                            lambda i: (0, jnp.maximum(i - tail_first_step, 0), 0))
    out_shape = (
        jax.ShapeDtypeStruct((N_GROUPS, m, GROUP_W), qkv_dtype),
        jax.ShapeDtypeStruct((N_GROUPS, m, GROUP_W), qkv_dtype),
        jax.ShapeDtypeStruct((N_GROUPS, m, GROUP_W), qkv_dtype),
        jax.ShapeDtypeStruct((m, D_RNN), F32),
        jax.ShapeDtypeStruct((m, D_RNN), F32),
        jax.ShapeDtypeStruct((m, D_MODEL), F32),
        jax.ShapeDtypeStruct((m, D_MODEL), F32),
        jax.ShapeDtypeStruct((N_GROUPS, tail, 2 * GROUP_W), F32),
    )
    return pl.pallas_call(
        functools.partial(_inproj_kernel, tail_first_step=tail_first_step),
        grid=(nt,),
        in_specs=[row(D_MODEL), _const_spec(g1.shape), _const_spec(w_in.shape),
                  _const_spec(qg.shape), _const_spec(kg.shape), _const_spec(mavg.shape)],
        out_specs=(grp, grp, grp, row(D_RNN), row(D_RNN), row(D_MODEL), row(D_MODEL), kvt_spec),
        out_shape=out_shape,
        compiler_params=_params(("arbitrary",)),
        name="inproj",
    )(x, g1, w_in, qg, kg, mavg)


def _attn_prompt_kernel(q_ref, kc_ref, kp_ref, vc_ref, vp_ref, o_ref, lse_ref):
    blk = pl.program_id(1)
    qi = lax.broadcasted_iota(jnp.int32, (BLOCK, 2 * BLOCK), 0)
    kj = lax.broadcasted_iota(jnp.int32, (BLOCK, 2 * BLOCK), 1)
    dist = BLOCK + qi - kj
    valid = (dist >= 0) & (dist <= BLOCK) & ((blk > 0) | (kj >= BLOCK))
    lane = lax.broadcasted_iota(jnp.int32, (1, LANES_V7X), 1)
    stat_lane = lax.broadcasted_iota(jnp.int32, (BLOCK, LANES_V7X), 1)
    lse_tile = jnp.zeros((BLOCK, LANES_V7X), F32)
    zero = jnp.zeros((), BF16)

    for pair in range(HEADS_PER_GROUP // 2):
        cols = slice(pair * LANES_V7X, (pair + 1) * LANES_V7X)
        qp = q_ref[:, cols]
        kk = jnp.concatenate([kp_ref[:, cols], kc_ref[:, cols]], axis=0)
        vv = jnp.concatenate([vp_ref[:, cols], vc_ref[:, cols]], axis=0)
        o_pair = jnp.zeros((BLOCK, LANES_V7X), F32)
        for e in range(2):
            in_head = (lane >= e * HEAD_DIM) & (lane < (e + 1) * HEAD_DIM)
            s = lax.dot_general(jnp.where(in_head, qp, zero), kk, (((1,), (1,)), ((), ())),
                                preferred_element_type=F32)
            s = jnp.where(valid, s, NEG_INF)
            mx = jnp.max(s, axis=-1, keepdims=True)
            p = jnp.exp(s - mx)
            den = jnp.sum(p, axis=-1, keepdims=True)
            pv = _mm(p.astype(BF16), jnp.where(in_head, vv, zero))
            o_pair = o_pair + pv * (1.0 / den)
            lse_tile = jnp.where(stat_lane == 2 * pair + e, mx + jnp.log(den), lse_tile)
        o_ref[:, cols] = o_pair.astype(o_ref.dtype)
    lse_ref[...] = lse_tile


def _attn_prompt(q, k, v, g, dilation):
    s = q.shape[1]
    m_len = s // dilation
    nb = m_len // BLOCK
    view = lambda t: t.reshape(N_GROUPS, m_len, dilation * GROUP_W)
    cur = pl.BlockSpec((None, BLOCK, GROUP_W), lambda r, b: (g, b, r))
    prev = pl.BlockSpec((None, BLOCK, GROUP_W), lambda r, b: (g, jnp.maximum(b - 1, 0), r))
    o, lse = pl.pallas_call(
        _attn_prompt_kernel,
        grid=(dilation, nb),
        in_specs=[cur, cur, prev, cur, prev],
        out_specs=(pl.BlockSpec((BLOCK, GROUP_W), lambda r, b: (b, r)),
                   pl.BlockSpec((BLOCK, LANES_V7X), lambda r, b: (b, r))),
        out_shape=(jax.ShapeDtypeStruct((m_len, dilation * GROUP_W), BF16),
                   jax.ShapeDtypeStruct((m_len, dilation * LANES_V7X), F32)),
        compiler_params=_params(("arbitrary", "arbitrary")),
        name=f"attn_prompt_g{g}",
    )(view(q), view(k), view(k), view(v), view(v))
    return o.reshape(s, GROUP_W), lse.reshape(s, LANES_V7X)


def _attn_sample_kernel(q_ref, k_ref, v_ref, c0_ref, c1_ref, c2_ref, o_ref):
    b = pl.program_id(0)
    n_keys = BLOCK + SUBLANES_V7X
    head_row = lax.broadcasted_iota(jnp.int32, (HEADS_PER_GROUP, GROUP_W), 0)
    head_lane = lax.broadcasted_iota(jnp.int32, (HEADS_PER_GROUP, GROUP_W), 1) // HEAD_DIM
    own = head_row == head_lane
    key_ok = lax.broadcasted_iota(jnp.int32, (HEADS_PER_GROUP, n_keys), 1) <= BLOCK
    pad = jnp.zeros((SUBLANES_V7X - 1, GROUP_W), F32)
    parts = []
    for g, c_ref in enumerate((c0_ref, c1_ref, c2_ref)):
        qrow = q_ref[g, pl.ds(b, 1), :]
        knew = k_ref[g, pl.ds(b, 1), :]
        vnew = v_ref[g, pl.ds(b, 1), :]
        kk = jnp.concatenate([c_ref[:, 0:GROUP_W], knew, pad], axis=0).astype(BF16)
        vv = jnp.concatenate([c_ref[:, GROUP_W:2 * GROUP_W], vnew, pad], axis=0).astype(BF16)
        qmat = jnp.where(own, jnp.broadcast_to(qrow, (HEADS_PER_GROUP, GROUP_W)), 0.0).astype(BF16)
        s = lax.dot_general(qmat, kk, (((1,), (1,)), ((), ())), preferred_element_type=F32)
        s = jnp.where(key_ok, s, NEG_INF)
        mx = jnp.max(s, axis=-1, keepdims=True)
        p = jnp.exp(s - mx)
        den = jnp.sum(p, axis=-1, keepdims=True)
        numer = jnp.where(own, _mm(p.astype(BF16), vv), 0.0)
        parts.append((numer, mx, den))
    m_all = functools.reduce(jnp.maximum, [mx for _, mx, _ in parts])
    ws = [jnp.exp(mx - m_all) for _, mx, _ in parts]
    num = sum(n * w for (n, _, _), w in zip(parts, ws))
    den = sum(d * w for (_, _, d), w in zip(parts, ws))
    o_ref[pl.ds(b, 1), :] = jnp.sum(num / den, axis=0, keepdims=True)


def _attn_sample(q, k, v, caches):
    nb = q.shape[1]
    views, specs = [], []
    for (window, dilation), c in zip(GROUPS, caches):
        views.append(c.reshape(nb, window // dilation, dilation * 2 * GROUP_W))
        specs.append(pl.BlockSpec((None, BLOCK, 2 * GROUP_W), lambda b: (b, 0, 0)))
    full = pl.BlockSpec((N_GROUPS, nb, GROUP_W), lambda b: (0, 0, 0))
    return pl.pallas_call(
        _attn_sample_kernel,
        grid=(nb,),
        in_specs=[full, full, full] + specs,
        out_specs=pl.BlockSpec((nb, GROUP_W), lambda b: (0, 0)),
        out_shape=jax.ShapeDtypeStruct((nb, GROUP_W), F32),
        compiler_params=_params(("arbitrary",)),
        name="attn_sample",
    )(q, k, v, *views)


def _gelu_tanh(x):
    cdf = 0.5 * (1.0 + jnp.tanh(0.7978845608028654 * (x + 0.044715 * (x * x * x))))
    return x * cdf


def _softplus(x):
    return jnp.maximum(x, 0.0) + jnp.log1p(jnp.exp(-jnp.abs(x)))


def _lru_coeffs(xc, wrg_ref, brg_ref, wig_ref, big_ref, lam_ref):
    xcb = xc.astype(BF16)

    def gate(w_ref, b_ref):
        cols = [_mm(xcb[:, n * RNN_BLOCK_W:(n + 1) * RNN_BLOCK_W], w_ref[n]) for n in range(RNN_BLOCKS)]
        return jax.nn.sigmoid(jnp.concatenate(cols, axis=1) + b_ref[...])

    r = gate(wrg_ref, brg_ref)
    i = gate(wig_ref, big_ref)
    log_a = -LRU_C * r * _softplus(-lam_ref[...])
    a = jnp.exp(log_a)
    t = jnp.tanh(-log_a)
    b = jnp.sqrt(2.0 * t / (1.0 + t)) * i * xc
    return a, b


def _rglru_prompt_kernel(xb_ref, gb_ref, cw_ref, cb_ref, wrg_ref, brg_ref, wig_ref, big_ref, lam_ref,
                         hb_ref, conv_ref, hlast_ref, xpad_ref, a_ref, b_ref, hcar_ref, *, tr):
    @pl.when(pl.program_id(0) == 0)
    def _():
        xpad_ref[0:SUBLANES_V7X, :] = jnp.zeros((SUBLANES_V7X, D_RNN), F32)
        hcar_ref[...] = jnp.zeros((1, D_RNN), F32)

    base = SUBLANES_V7X
    xpad_ref[base:base + tr, :] = xb_ref[...]
    xc = cb_ref[...] + sum(
        xpad_ref[base - (CONV_W - 1) + j:base - (CONV_W - 1) + j + tr, :] * cw_ref[j:j + 1, :]
        for j in range(CONV_W))
    xpad_ref[0:SUBLANES_V7X, :] = xb_ref[tr - SUBLANES_V7X:tr, :]
    conv_ref[...] = xb_ref[tr - SUBLANES_V7X:tr, :]

    a, b = _lru_coeffs(xc, wrg_ref, brg_ref, wig_ref, big_ref, lam_ref)

    row8 = lax.broadcasted_iota(jnp.int32, (tr, D_RNN), 0) % SUBLANES_V7X
    for s in (1, 2, 4):
        keep = row8 >= s
        a_prev = jnp.where(keep, pltpu.roll(a, s, 0), 1.0)
        b_prev = jnp.where(keep, pltpu.roll(b, s, 0), 0.0)
        b = a * b_prev + b
        a = a * a_prev
    a_ref[...] = a
    b_ref[...] = b

    h = hcar_ref[...]
    for grp in range(tr // SUBLANES_V7X):
        rows = pl.ds(grp * SUBLANES_V7X, SUBLANES_V7X)
        hrows = a_ref[rows, :] * h + b_ref[rows, :]
        b_ref[rows, :] = hrows
        h = hrows[SUBLANES_V7X - 1:SUBLANES_V7X, :]
    hcar_ref[...] = h
    hlast_ref[...] = h
    hb_ref[...] = (b_ref[...] * _gelu_tanh(gb_ref[...])).astype(hb_ref.dtype)


def _rglru_prompt(xb, gb, cw, cb, wrg, brg, wig, big, lam, *, tr):
    s = xb.shape[0]
    row = pl.BlockSpec((tr, D_RNN), lambda i: (i, 0))
    consts = (cw, cb, wrg, brg, wig, big, lam)
    return pl.pallas_call(
        functools.partial(_rglru_prompt_kernel, tr=tr),
        grid=(s // tr,),
        in_specs=[row, row] + [_const_spec(c.shape) for c in consts],
        out_specs=(row,
                   pl.BlockSpec((SUBLANES_V7X, D_RNN), lambda i: (0, 0)),
                   pl.BlockSpec((1, D_RNN), lambda i: (0, 0))),
        out_shape=(jax.ShapeDtypeStruct((s, D_RNN), BF16),
                   jax.ShapeDtypeStruct((SUBLANES_V7X, D_RNN), F32),
                   jax.ShapeDtypeStruct((1, D_RNN), F32)),
        scratch_shapes=[pltpu.VMEM((tr + SUBLANES_V7X, D_RNN), F32),
                        pltpu.VMEM((tr, D_RNN), F32),
                        pltpu.VMEM((tr, D_RNN), F32),
                        pltpu.VMEM((1, D_RNN), F32)],
        compiler_params=_params(("arbitrary",)),
        name="rglru_prompt",
    )(xb, gb, *consts)


def _rglru_sample_kernel(xb_ref, gb_ref, sc_ref, h0_ref, cw_ref, cb_ref, wrg_ref, brg_ref, wig_ref,
                         big_ref, lam_ref, hb_ref, conv_ref, h_ref):
    xb = xb_ref[...]
    taps = [sc_ref[:, j * D_RNN:(j + 1) * D_RNN] for j in range(CONV_W - 1)] + [xb]
    xc = cb_ref[...] + sum(t * cw_ref[j:j + 1, :] for j, t in enumerate(taps))
    a, b = _lru_coeffs(xc, wrg_ref, brg_ref, wig_ref, big_ref, lam_ref)
    h = a * h0_ref[...] + b
    h_ref[...] = h
    hb_ref[...] = (h * _gelu_tanh(gb_ref[...])).astype(hb_ref.dtype)
    for j in range(CONV_W - 1):
        conv_ref[:, j * D_RNN:(j + 1) * D_RNN] = taps[j + 1]


def _rglru_sample(xb, gb, state_conv, h0, cw, cb, wrg, brg, wig, big, lam):
    nb = xb.shape[0]
    args = (xb, gb, state_conv, h0, cw, cb, wrg, brg, wig, big, lam)
    return pl.pallas_call(
        _rglru_sample_kernel,
        grid=(1,),
        in_specs=[_const_spec(a.shape) for a in args],
        out_specs=(_const_spec((nb, D_RNN)), _const_spec((nb, (CONV_W - 1) * D_RNN)),
                   _const_spec((nb, D_RNN))),
        out_shape=(jax.ShapeDtypeStruct((nb, D_RNN), BF16),
                   jax.ShapeDtypeStruct((nb, (CONV_W - 1) * D_RNN), F32),
                   jax.ShapeDtypeStruct((nb, D_RNN), F32)),
        compiler_params=_params(("arbitrary",)),
        name="rglru_sample",
    )(*args)


def _post_kernel(*refs, n_parts):
    attn_refs = refs[:2 * n_parts] if n_parts > 1 else refs[:1]
    (hb_ref, ga_ref, gb2_ref, x_ref, expand_ref, bg_ref, woa_ref, wor_ref, wout_ref, g2_ref,
     wfi_ref, wfo_ref, y_ref) = refs[len(attn_refs):]

    if n_parts > 1:
        o_refs, lse_refs = attn_refs[:n_parts], attn_refs[n_parts:]
        lses = [r[...] for r in lse_refs]
        mx = functools.reduce(jnp.maximum, lses)
        es = [jnp.exp(l - mx) for l in lses]
        inv = 1.0 / sum(es)
        attn = 0.0
        for o_ref, e in zip(o_refs, es):
            cw = e * inv
            hi = cw.astype(BF16)
            lo = (cw - hi.astype(F32)).astype(BF16)
            wide = _mm(hi, expand_ref[...]) + _mm(lo, expand_ref[...])
            attn = attn + wide * o_ref[...].astype(F32)
    else:
        attn = attn_refs[0][...]

    ya = _mm(attn.astype(BF16), woa_ref[...])
    yb = _mm(hb_ref[...], wor_ref[...])
    merged = (jax.nn.sigmoid(ga_ref[...] + bg_ref[0:1, :]) * ya
              + jax.nn.sigmoid(gb2_ref[...] + bg_ref[1:2, :]) * yb)
    x1 = x_ref[...] + _mm(merged.astype(BF16), wout_ref[...])
    hn2 = _rms_norm_rows(x1, g2_ref[...]).astype(BF16)
    gu = _mm(hn2, wfi_ref[...])
    act = jax.nn.silu(gu[:, :D_FF]) * gu[:, D_FF:]
    y_ref[...] = x1 + _mm(act.astype(BF16), wfo_ref[...])


def _post(attn_parts, hb, ga, gb2, x, expand, bg, woa, wor, wout, g2, wfi, wfo, *, tm):
    m = x.shape[0]
    row = lambda w: pl.BlockSpec((tm, w), lambda i: (i, 0))
    n_parts = len(attn_parts)
    if n_parts > 1:
        attn_args = [o for o, _ in attn_parts] + [l for _, l in attn_parts]
        attn_specs = [row(GROUP_W)] * n_parts + [row(LANES_V7X)] * n_parts
    else:
        attn_args, attn_specs = list(attn_parts), [row(GROUP_W)]
    consts = (expand, bg, woa, wor, wout, g2, wfi, wfo)
    return pl.pallas_call(
        functools.partial(_post_kernel, n_parts=n_parts),
        grid=(m // tm,),
        in_specs=attn_specs + [row(D_RNN), row(D_MODEL), row(D_MODEL), row(D_MODEL)]
        + [_const_spec(c.shape) for c in consts],
        out_specs=row(D_MODEL),
        out_shape=jax.ShapeDtypeStruct((m, D_MODEL), F32),
        compiler_params=_params(("arbitrary",)),
        name="post_prompt" if n_parts > 1 else "post_sample",
    )(*attn_args, hb, ga, gb2, x, *consts)


def kernel(x_prompt, x_sample, cache_kv_w128, cache_kv_w512, cache_kv_w2048, state_conv, state_h,
           norm1_g, w_in, b_gate, q_norm_g, k_norm_g, conv_w, conv_b, w_rg, b_rg, w_ig, b_ig,
           lru_lambda, w_o_attn, w_o_rnn, w_out, norm2_g, w_ffn_in, w_ffn_out):
    assert x_prompt.shape[0] == 1 and norm1_g.shape[0] == 1 and x_sample.shape[1] == 1
    seq = x_prompt.shape[1]
    nb = x_sample.shape[0]
    layer = 0

    idx = jnp.arange(MXU_DIM_V7X) // HEAD_DIM
    mavg = jnp.where(idx[:, None] == idx[None, :], 1.0 / HEAD_DIM, 0.0).astype(BF16)
    expand = (jnp.arange(LANES_V7X)[:, None] == (jnp.arange(GROUP_W) // HEAD_DIM)[None, :]).astype(BF16)

    row2 = lambda t: t[layer].reshape(1, -1)
    g1, g2 = row2(norm1_g), row2(norm2_g)
    qg = jnp.tile(q_norm_g[layer], HEADS_PER_GROUP).reshape(1, GROUP_W)
    kg = jnp.tile(k_norm_g[layer], HEADS_PER_GROUP).reshape(1, GROUP_W)
    w_in_b = w_in[layer].astype(BF16)
    wrg, wig = w_rg[layer].astype(BF16), w_ig[layer].astype(BF16)
    woa, wor, wout = (w[layer].astype(BF16) for w in (w_o_attn, w_o_rnn, w_out))
    wfi, wfo = w_ffn_in[layer].astype(BF16), w_ffn_out[layer].astype(BF16)
    lru = (conv_w[layer], row2(conv_b), wrg, row2(b_rg), wig, row2(b_ig), row2(lru_lambda))
    post_w = (expand, b_gate[layer], woa, wor, wout, g2, wfi, wfo)

    xp = x_prompt[0]
    q, k, v, xb, gb, ga, gb2, kvt = _inproj(xp, g1, w_in_b, qg, kg, mavg, tm=256, tail=MAX_WINDOW,
                                            qkv_dtype=BF16)
    parts = [_attn_prompt(q, k, v, g, dilation) for g, (_, dilation) in enumerate(GROUPS)]
    hb, conv8, h_last = _rglru_prompt(xb, gb, *lru, tr=256)
    y_p = _post(parts, hb, ga, gb2, xp, *post_w, tm=256)

    xs = x_sample[:, 0]
    qs, ks, vs, xbs, gbs, gas, gb2s, kvts = _inproj(xs, g1, w_in_b, qg, kg, mavg, tm=nb, tail=nb,
                                                    qkv_dtype=F32)
    caches = [c[layer].reshape(nb, c.shape[2], 2 * GROUP_W)
              for c in (cache_kv_w128, cache_kv_w512, cache_kv_w2048)]
    attn_s = _attn_sample(qs, ks, vs, caches)
    hbs, conv_s, h_s = _rglru_sample(xbs, gbs, state_conv[layer].reshape(nb, -1), state_h[layer], *lru)
    y_s = _post([attn_s], hbs, gas, gb2s, xs, *post_w, tm=nb)

    kv_shape = lambda rows: (1, 1, rows, 2, HEADS_PER_GROUP, HEAD_DIM)
    kv_prompt = [kvt[g, MAX_WINDOW - min(w, seq):].reshape(kv_shape(min(w, seq)))
                 for g, (w, _) in enumerate(GROUPS)]
    kv_sample = [kvts[g].reshape(1, nb, 1, 2, HEADS_PER_GROUP, HEAD_DIM) for g in range(N_GROUPS)]
    return (y_p[None], y_s[:, None],
            kv_prompt[0], kv_prompt[1], kv_prompt[2],
            conv8[SUBLANES_V7X - (CONV_W - 1):][None, None], h_last[None],
            kv_sample[0], kv_sample[1], kv_sample[2],
            conv_s.reshape(1, nb, CONV_W - 1, D_RNN), h_s[None])
```

```python
import functools

import jax
import jax.numpy as jnp
from jax import lax
from jax.experimental import pallas as pl
from jax.experimental.pallas import tpu as pltpu

F32 = jnp.float32
BF16 = jnp.bfloat16

D_MODEL = 1024
HEAD_DIM = 64
HEADS_PER_GROUP = 8
GROUPS = ((128, 1), (512, 4), (2048, 16))
DILATIONS = tuple(d for _, d in GROUPS)
N_GROUPS = len(GROUPS)
GROUP_W = HEADS_PER_GROUP * HEAD_DIM
QKV_WIDTH = N_GROUPS * GROUP_W
BLOCK = 128
ATTN_SCALE = HEAD_DIM ** -0.5
NEG_INF = -1e30
D_RNN = 1280
RNN_BLOCKS = 10
RNN_BLOCK_W = D_RNN // RNN_BLOCKS
CONV_W = 4
LRU_C = 8.0
D_FF = 2816
RMS_EPS = 1e-6
MAX_WINDOW = max(w for w, _ in GROUPS)

LANES_V7X = 128
SUBLANES_V7X = 8
MXU_DIM_V7X = 256
VMEM_LIMIT_BYTES = 56 * 1024 * 1024

OFF_Q, OFF_K, OFF_V = 0, QKV_WIDTH, 2 * QKV_WIDTH
OFF_XB = 3 * QKV_WIDTH
OFF_GB = OFF_XB + D_RNN
OFF_GA = OFF_GB + D_RNN
OFF_GB2 = OFF_GA + D_MODEL


def _mm(a, b):
    return jnp.dot(a, b, preferred_element_type=F32)


def _mm_nt(a, b):
    return lax.dot_general(a, b, (((1,), (1,)), ((), ())), preferred_element_type=F32)


def _rms_norm_rows(x, g):
    return x * lax.rsqrt(jnp.mean(x * x, axis=-1, keepdims=True) + RMS_EPS) * g


def _const_spec(shape):
    nd = len(shape)
    return pl.BlockSpec(shape, lambda *_: (0,) * nd, pipeline_mode=pl.Buffered(1))


def _params(sem):
    return pltpu.CompilerParams(dimension_semantics=sem, vmem_limit_bytes=VMEM_LIMIT_BYTES)


def _inproj_kernel(x_ref, g1_ref, w_ref, qg_ref, kg_ref, mavg_ref, *refs, dilations, tail_first_step):
    qkv_refs = refs[:3 * N_GROUPS]
    xb_ref, gbr_ref, ga_ref, gb2_ref, kvt_ref, hn_ref = refs[3 * N_GROUPS:]
    step = pl.program_id(0)
    tm = x_ref.shape[0]
    hn32 = _rms_norm_rows(x_ref[...], g1_ref[...])
    hn = hn32.astype(BF16)
    n_slabs = D_MODEL // LANES_V7X
    for c in range(n_slabs):
        hn_ref[c] = hn32[:, c * LANES_V7X:(c + 1) * LANES_V7X]
    mavg = mavg_ref[...]

    def head_norm(t, gain):
        tt = (t * t).astype(BF16)
        ms = jnp.concatenate(
            [_mm(tt[:, c:c + MXU_DIM_V7X], mavg) for c in range(0, GROUP_W, MXU_DIM_V7X)], axis=1)
        return t * lax.rsqrt(ms + RMS_EPS) * gain

    def qkv(h, g):
        c = g * GROUP_W
        qn = head_norm(_mm(h, w_ref[:, OFF_Q + c:OFF_Q + c + GROUP_W]), qg_ref[...]) * ATTN_SCALE
        kn = head_norm(_mm(h, w_ref[:, OFF_K + c:OFF_K + c + GROUP_W]), kg_ref[...])
        vv = _mm(h, w_ref[:, OFF_V + c:OFF_V + c + GROUP_W])
        return qn, kn, vv

    for g, d in enumerate(dilations):
        rows = tm // d
        if d == 1:
            hg = hn
        else:
            hg = jnp.concatenate(
                [jnp.concatenate([hn_ref[c, pl.ds(r, rows, stride=d), :] for c in range(n_slabs)], axis=1)
                 for r in range(d)], axis=0).astype(BF16)
        parts = qkv(hg, g)
        for t, o_ref in zip(parts, qkv_refs[3 * g:3 * g + 3]):
            for r in range(d):
                o_ref[:, r * GROUP_W:(r + 1) * GROUP_W] = t[r * rows:(r + 1) * rows].astype(o_ref.dtype)

        @pl.when(step >= tail_first_step)
        def _():
            _, kn, vv = parts if d == 1 else qkv(hn, g)
            kvt_ref[g, :, 0:GROUP_W] = kn
            kvt_ref[g, :, GROUP_W:2 * GROUP_W] = vv

    xb_ref[...] = _mm(hn, w_ref[:, OFF_XB:OFF_XB + D_RNN])
    gbr_ref[...] = _mm(hn, w_ref[:, OFF_GB:OFF_GB + D_RNN])
    ga_ref[...] = _mm(hn, w_ref[:, OFF_GA:OFF_GA + D_MODEL])
    gb2_ref[...] = _mm(hn, w_ref[:, OFF_GB2:OFF_GB2 + D_MODEL])


def _inproj(x, g1, w_in, qg, kg, mavg, *, tm, tail, qkv_dtype, dilations):
    m = x.shape[0]
    nt = m // tm
    tail_first_step = nt - tail // tm
    row = lambda w: pl.BlockSpec((tm, w), lambda i: (i, 0))
    qkv_specs, qkv_shapes = [], []
    for d in dilations:
        qkv_specs += [pl.BlockSpec((tm // d, d * GROUP_W), lambda i: (i, 0))] * 3
        qkv_shapes += [jax.ShapeDtypeStruct((m // d, d * GROUP_W), qkv_dtype)] * 3
    kvt_spec = pl.BlockSpec((N_GROUPS, tm, 2 * GROUP_W),
                            lambda i: (0, jnp.maximum(i - tail_first_step, 0), 0))
    out_shape = tuple(qkv_shapes) + (
        jax.ShapeDtypeStruct((m, D_RNN), F32),
        jax.ShapeDtypeStruct((m, D_RNN), F32),
        jax.ShapeDtypeStruct((m, D_MODEL), F32),
        jax.ShapeDtypeStruct((m, D_MODEL), F32),
        jax.ShapeDtypeStruct((N_GROUPS, tail, 2 * GROUP_W), F32),
    )
    return pl.pallas_call(
        functools.partial(_inproj_kernel, dilations=dilations, tail_first_step=tail_first_step),
        grid=(nt,),
        in_specs=[row(D_MODEL), _const_spec(g1.shape), _const_spec(w_in.shape),
                  _const_spec(qg.shape), _const_spec(kg.shape), _const_spec(mavg.shape)],
        out_specs=tuple(qkv_specs) + (row(D_RNN), row(D_RNN), row(D_MODEL), row(D_MODEL), kvt_spec),
        out_shape=out_shape,
        scratch_shapes=[pltpu.VMEM((D_MODEL // LANES_V7X, tm, LANES_V7X), F32)],
        compiler_params=_params(("arbitrary",)),
        name="inproj",
    )(x, g1, w_in, qg, kg, mavg)


def _attn_prompt_kernel(q_ref, kc_ref, kp_ref, vc_ref, vp_ref, o_ref, lse_ref):
    blk = pl.program_id(1)
    qi = lax.broadcasted_iota(jnp.int32, (BLOCK, 2 * BLOCK), 0)
    kj = lax.broadcasted_iota(jnp.int32, (BLOCK, 2 * BLOCK), 1)
    dist = BLOCK + qi - kj
    valid = (dist >= 0) & (dist <= BLOCK) & ((blk > 0) | (kj >= BLOCK))
    lane = lax.broadcasted_iota(jnp.int32, (1, LANES_V7X), 1)
    stat_lane = lax.broadcasted_iota(jnp.int32, (BLOCK, LANES_V7X), 1)
    lse_tile = jnp.zeros((BLOCK, LANES_V7X), F32)
    zero = jnp.zeros((), BF16)

    for pair in range(HEADS_PER_GROUP // 2):
        cols = slice(pair * LANES_V7X, (pair + 1) * LANES_V7X)
        qp = q_ref[:, cols]
        kk = jnp.concatenate([kp_ref[:, cols], kc_ref[:, cols]], axis=0)
        vv = jnp.concatenate([vp_ref[:, cols], vc_ref[:, cols]], axis=0)
        o_pair = jnp.zeros((BLOCK, LANES_V7X), F32)
        for e in range(2):
            in_head = (lane >= e * HEAD_DIM) & (lane < (e + 1) * HEAD_DIM)
            s = _mm_nt(jnp.where(in_head, qp, zero), kk)
            s = jnp.where(valid, s, NEG_INF)
            mx = jnp.max(s, axis=-1, keepdims=True)
            p = jnp.exp(s - mx)
            den = jnp.sum(p, axis=-1, keepdims=True)
            pv = _mm(p.astype(BF16), jnp.where(in_head, vv, zero))
            o_pair = o_pair + pv * (1.0 / den)
            lse_tile = jnp.where(stat_lane == 2 * pair + e, mx + jnp.log(den), lse_tile)
        o_ref[:, cols] = o_pair.astype(o_ref.dtype)
    lse_ref[...] = lse_tile


def _attn_prompt(q, k, v, g, dilation):
    m_len = q.shape[0]
    nb = m_len // BLOCK
    cur = pl.BlockSpec((BLOCK, GROUP_W), lambda r, b: (b, r))
    prev = pl.BlockSpec((BLOCK, GROUP_W), lambda r, b: (jnp.maximum(b - 1, 0), r))
    return pl.pallas_call(
        _attn_prompt_kernel,
        grid=(dilation, nb),
        in_specs=[cur, cur, prev, cur, prev],
        out_specs=(pl.BlockSpec((BLOCK, GROUP_W), lambda r, b: (b, r)),
                   pl.BlockSpec((BLOCK, LANES_V7X), lambda r, b: (b, r))),
        out_shape=(jax.ShapeDtypeStruct((m_len, dilation * GROUP_W), BF16),
                   jax.ShapeDtypeStruct((m_len, dilation * LANES_V7X), F32)),
        compiler_params=_params(("arbitrary", "arbitrary")),
        name=f"attn_prompt_g{g}",
    )(q, k, k, v, v)


def _attn_sample_kernel(q_ref, k_ref, v_ref, c0_ref, c1_ref, c2_ref, o_ref):
    b = pl.program_id(0)
    head_row = lax.broadcasted_iota(jnp.int32, (HEADS_PER_GROUP, GROUP_W), 0)
    head_lane = lax.broadcasted_iota(jnp.int32, (HEADS_PER_GROUP, GROUP_W), 1) // HEAD_DIM
    own = head_row == head_lane
    parts = []
    for g, c_ref in enumerate((c0_ref, c1_ref, c2_ref)):
        window, dilation = GROUPS[g]
        bf = lambda t: t.astype(BF16).astype(F32)
        qmat = jnp.where(own, jnp.broadcast_to(q_ref[g, pl.ds(b, 1), :], (HEADS_PER_GROUP, GROUP_W)), 0.0)
        qmat = qmat.astype(BF16)
        knew = bf(k_ref[g, pl.ds(b, 1), :])
        vnew = bf(v_ref[g, pl.ds(b, 1), :])
        pos = lax.broadcasted_iota(jnp.int32, (HEADS_PER_GROUP, window), 1)
        s = _mm(qmat, c_ref[0:GROUP_W, :].astype(BF16))
        s = jnp.where((pos & (dilation - 1)) == 0, s, NEG_INF)
        s_new = jnp.sum(qmat.astype(F32) * knew, axis=-1, keepdims=True)
        mx = jnp.maximum(jnp.max(s, axis=-1, keepdims=True), s_new)
        p = jnp.exp(s - mx)
        p_new = jnp.exp(s_new - mx)
        den = jnp.sum(p, axis=-1, keepdims=True) + p_new
        numer = _mm_nt(p.astype(BF16), c_ref[GROUP_W:2 * GROUP_W, :].astype(BF16)) + bf(p_new) * vnew
        parts.append((jnp.where(own, numer, 0.0), mx, den))
    m_all = functools.reduce(jnp.maximum, [mx for _, mx, _ in parts])
    ws = [jnp.exp(mx - m_all) for _, mx, _ in parts]
    num = sum(n * w for (n, _, _), w in zip(parts, ws))
    den = sum(d * w for (_, _, d), w in zip(parts, ws))
    o_ref[pl.ds(b, 1), :] = jnp.sum(num / den, axis=0, keepdims=True)


def _attn_sample(q, k, v, caches):
    nb = q.shape[1]
    specs = [pl.BlockSpec((None,) + c.shape[1:], lambda b: (b, 0, 0)) for c in caches]
    full = pl.BlockSpec((N_GROUPS, nb, GROUP_W), lambda b: (0, 0, 0))
    return pl.pallas_call(
        _attn_sample_kernel,
        grid=(nb,),
        in_specs=[full, full, full] + specs,
        out_specs=pl.BlockSpec((nb, GROUP_W), lambda b: (0, 0)),
        out_shape=jax.ShapeDtypeStruct((nb, GROUP_W), F32),
        compiler_params=_params(("arbitrary",)),
        name="attn_sample",
    )(q, k, v, *caches)


def _gelu_tanh(x):
    cdf = 0.5 * (1.0 + jnp.tanh(0.7978845608028654 * (x + 0.044715 * (x * x * x))))
    return x * cdf


def _softplus(x):
    return jnp.maximum(x, 0.0) + jnp.log1p(jnp.exp(-jnp.abs(x)))


def _lru_coeffs(xc, wrg_ref, brg_ref, wig_ref, big_ref, lam_ref):
    xcb = xc.astype(BF16)

    def gate(w_ref, b_ref):
        cols = [_mm(xcb[:, n * RNN_BLOCK_W:(n + 1) * RNN_BLOCK_W], w_ref[n]) for n in range(RNN_BLOCKS)]
        return jax.nn.sigmoid(jnp.concatenate(cols, axis=1) + b_ref[...])

    r = gate(wrg_ref, brg_ref)
    i = gate(wig_ref, big_ref)
    log_a = -LRU_C * r * _softplus(-lam_ref[...])
    a = jnp.exp(log_a)
    t = jnp.tanh(-log_a)
    b = jnp.sqrt(2.0 * t / (1.0 + t)) * i * xc
    return a, b


def _rglru_prompt_kernel(xb_ref, gb_ref, cw_ref, cb_ref, wrg_ref, brg_ref, wig_ref, big_ref, lam_ref,
                         hb_ref, conv_ref, hlast_ref, xpad_ref, a_ref, b_ref, hcar_ref, *, tr):
    @pl.when(pl.program_id(0) == 0)
    def _():
        xpad_ref[0:SUBLANES_V7X, :] = jnp.zeros((SUBLANES_V7X, D_RNN), F32)
        hcar_ref[...] = jnp.zeros((1, D_RNN), F32)

    base = SUBLANES_V7X
    xpad_ref[base:base + tr, :] = xb_ref[...]
    xc = cb_ref[...] + sum(
        xpad_ref[base - (CONV_W - 1) + j:base - (CONV_W - 1) + j + tr, :] * cw_ref[j:j + 1, :]
        for j in range(CONV_W))
    xpad_ref[0:SUBLANES_V7X, :] = xb_ref[tr - SUBLANES_V7X:tr, :]
    conv_ref[...] = xb_ref[tr - SUBLANES_V7X:tr, :]

    a, b = _lru_coeffs(xc, wrg_ref, brg_ref, wig_ref, big_ref, lam_ref)

    row8 = lax.broadcasted_iota(jnp.int32, (tr, D_RNN), 0) % SUBLANES_V7X
    for s in (1, 2, 4):
        keep = row8 >= s
        a_prev = jnp.where(keep, pltpu.roll(a, s, 0), 1.0)
        b_prev = jnp.where(keep, pltpu.roll(b, s, 0), 0.0)
        b = a * b_prev + b
        a = a * a_prev
    a_ref[...] = a
    b_ref[...] = b

    h = hcar_ref[...]
    for grp in range(tr // SUBLANES_V7X):
        rows = pl.ds(grp * SUBLANES_V7X, SUBLANES_V7X)
        hrows = a_ref[rows, :] * h + b_ref[rows, :]
        b_ref[rows, :] = hrows
        h = hrows[SUBLANES_V7X - 1:SUBLANES_V7X, :]
    hcar_ref[...] = h
    hlast_ref[...] = h
    hb_ref[...] = (b_ref[...] * _gelu_tanh(gb_ref[...])).astype(hb_ref.dtype)


def _rglru_prompt(xb, gb, cw, cb, wrg, brg, wig, big, lam, *, tr):
    s = xb.shape[0]
    row = pl.BlockSpec((tr, D_RNN), lambda i: (i, 0))
    consts = (cw, cb, wrg, brg, wig, big, lam)
    return pl.pallas_call(
        functools.partial(_rglru_prompt_kernel, tr=tr),
        grid=(s // tr,),
        in_specs=[row, row] + [_const_spec(c.shape) for c in consts],
        out_specs=(row,
                   pl.BlockSpec((SUBLANES_V7X, D_RNN), lambda i: (0, 0)),
                   pl.BlockSpec((1, D_RNN), lambda i: (0, 0))),
        out_shape=(jax.ShapeDtypeStruct((s, D_RNN), BF16),
                   jax.ShapeDtypeStruct((SUBLANES_V7X, D_RNN), F32),
                   jax.ShapeDtypeStruct((1, D_RNN), F32)),
        scratch_shapes=[pltpu.VMEM((tr + SUBLANES_V7X, D_RNN), F32),
                        pltpu.VMEM((tr, D_RNN), F32),
                        pltpu.VMEM((tr, D_RNN), F32),
                        pltpu.VMEM((1, D_RNN), F32)],
        compiler_params=_params(("arbitrary",)),
        name="rglru_prompt",
    )(xb, gb, *consts)


def _rglru_sample_kernel(xb_ref, gb_ref, sc_ref, h0_ref, cw_ref, cb_ref, wrg_ref, brg_ref, wig_ref,
                         big_ref, lam_ref, hb_ref, conv_ref, h_ref):
    xb = xb_ref[...]
    taps = [sc_ref[:, j * D_RNN:(j + 1) * D_RNN] for j in range(CONV_W - 1)] + [xb]
    xc = cb_ref[...] + sum(t * cw_ref[j:j + 1, :] for j, t in enumerate(taps))
    a, b = _lru_coeffs(xc, wrg_ref, brg_ref, wig_ref, big_ref, lam_ref)
    h = a * h0_ref[...] + b
    h_ref[...] = h
    hb_ref[...] = (h * _gelu_tanh(gb_ref[...])).astype(hb_ref.dtype)
    for j in range(CONV_W - 1):
        conv_ref[:, j * D_RNN:(j + 1) * D_RNN] = taps[j + 1]


def _rglru_sample(xb, gb, state_conv, h0, cw, cb, wrg, brg, wig, big, lam):
    nb = xb.shape[0]
    args = (xb, gb, state_conv, h0, cw, cb, wrg, brg, wig, big, lam)
    return pl.pallas_call(
        _rglru_sample_kernel,
        grid=(1,),
        in_specs=[_const_spec(a.shape) for a in args],
        out_specs=(_const_spec((nb, D_RNN)), _const_spec((nb, (CONV_W - 1) * D_RNN)),
                   _const_spec((nb, D_RNN))),
        out_shape=(jax.ShapeDtypeStruct((nb, D_RNN), BF16),
                   jax.ShapeDtypeStruct((nb, (CONV_W - 1) * D_RNN), F32),
                   jax.ShapeDtypeStruct((nb, D_RNN), F32)),
        compiler_params=_params(("arbitrary",)),
        name="rglru_sample",
    )(*args)


def _post_kernel(*refs, dilations):
    n_parts = len(dilations)
    attn_refs = refs[:2 * n_parts] if n_parts > 1 else refs[:1]
    rest = refs[len(attn_refs):]
    (hb_ref, ga_ref, gb2_ref, x_ref, expand_ref, bg_ref, woa_ref, wor_ref, wout_ref, g2_ref,
     wfi_ref, wfo_ref, y_ref) = rest[:13]
    tm = x_ref.shape[0]

    if n_parts > 1:
        o_nat_ref, lse_nat_ref = rest[13:]

        def natural(src_ref, dst_ref, width, d):
            if d == 1:
                return src_ref[...].astype(F32)
            n_slabs = width // LANES_V7X
            for r in range(d):
                for c in range(n_slabs):
                    lanes = slice(r * width + c * LANES_V7X, r * width + (c + 1) * LANES_V7X)
                    dst_ref[c, pl.ds(r, tm // d, stride=d), :] = src_ref[:, lanes].astype(F32)
            return jnp.concatenate([dst_ref[c] for c in range(n_slabs)], axis=1)

        o_refs, lse_refs = attn_refs[:n_parts], attn_refs[n_parts:]
        lses = [natural(r, lse_nat_ref, LANES_V7X, d) for r, d in zip(lse_refs, dilations)]
        mx = functools.reduce(jnp.maximum, lses)
        es = [jnp.exp(l - mx) for l in lses]
        inv = 1.0 / sum(es)
        attn = 0.0
        for o_ref, e, d in zip(o_refs, es, dilations):
            cw = e * inv
            hi = cw.astype(BF16)
            lo = (cw - hi.astype(F32)).astype(BF16)
            wide = _mm(hi, expand_ref[...]) + _mm(lo, expand_ref[...])
            attn = attn + wide * natural(o_ref, o_nat_ref, GROUP_W, d)
    else:
        attn = attn_refs[0][...]

    ya = _mm(attn.astype(BF16), woa_ref[...])
    yb = _mm(hb_ref[...], wor_ref[...])
    merged = (jax.nn.sigmoid(ga_ref[...] + bg_ref[0:1, :]) * ya
              + jax.nn.sigmoid(gb2_ref[...] + bg_ref[1:2, :]) * yb)
    x1 = x_ref[...] + _mm(merged.astype(BF16), wout_ref[...])
    hn2 = _rms_norm_rows(x1, g2_ref[...]).astype(BF16)
    gu = _mm(hn2, wfi_ref[...])
    act = jax.nn.silu(gu[:, :D_FF]) * gu[:, D_FF:]
    y_ref[...] = x1 + _mm(act.astype(BF16), wfo_ref[...])


def _post(attn_parts, dilations, hb, ga, gb2, x, expand, bg, woa, wor, wout, g2, wfi, wfo, *, tm):
    m = x.shape[0]
    row = lambda w: pl.BlockSpec((tm, w), lambda i: (i, 0))
    n_parts = len(attn_parts)
    scratch = []
    if n_parts > 1:
        blocked = lambda w, d: pl.BlockSpec((tm // d, d * w), lambda i: (i, 0))
        attn_args = [o for o, _ in attn_parts] + [l for _, l in attn_parts]
        attn_specs = ([blocked(GROUP_W, d) for d in dilations]
                      + [blocked(LANES_V7X, d) for d in dilations])
        scratch = [pltpu.VMEM((GROUP_W // LANES_V7X, tm, LANES_V7X), F32),
                   pltpu.VMEM((1, tm, LANES_V7X), F32)]
    else:
        attn_args, attn_specs = list(attn_parts), [row(GROUP_W)]
    consts = (expand, bg, woa, wor, wout, g2, wfi, wfo)
    return pl.pallas_call(
        functools.partial(_post_kernel, dilations=dilations),
        grid=(m // tm,),
        in_specs=attn_specs + [row(D_RNN), row(D_MODEL), row(D_MODEL), row(D_MODEL)]
        + [_const_spec(c.shape) for c in consts],
        out_specs=row(D_MODEL),
        out_shape=jax.ShapeDtypeStruct((m, D_MODEL), F32),
        scratch_shapes=scratch,
        compiler_params=_params(("arbitrary",)),
        name="post_prompt" if n_parts > 1 else "post_sample",
    )(*attn_args, hb, ga, gb2, x, *consts)


def kernel(x_prompt, x_sample, cache_kv_w128, cache_kv_w512, cache_kv_w2048, state_conv, state_h,
           norm1_g, w_in, b_gate, q_norm_g, k_norm_g, conv_w, conv_b, w_rg, b_rg, w_ig, b_ig,
           lru_lambda, w_o_attn, w_o_rnn, w_out, norm2_g, w_ffn_in, w_ffn_out):
    assert x_prompt.shape[0] == 1 and norm1_g.shape[0] == 1 and x_sample.shape[1] == 1
    seq = x_prompt.shape[1]
    nb = x_sample.shape[0]
    layer = 0

    idx = jnp.arange(MXU_DIM_V7X) // HEAD_DIM
    mavg = jnp.where(idx[:, None] == idx[None, :], 1.0 / HEAD_DIM, 0.0).astype(BF16)
    expand = (jnp.arange(LANES_V7X)[:, None] == (jnp.arange(GROUP_W) // HEAD_DIM)[None, :]).astype(BF16)

    row2 = lambda t: t[layer].reshape(1, -1)
    g1, g2 = row2(norm1_g), row2(norm2_g)
    qg = jnp.tile(q_norm_g[layer], HEADS_PER_GROUP).reshape(1, GROUP_W)
    kg = jnp.tile(k_norm_g[layer], HEADS_PER_GROUP).reshape(1, GROUP_W)
    w_in_b = w_in[layer].astype(BF16)
    wrg, wig = w_rg[layer].astype(BF16), w_ig[layer].astype(BF16)
    woa, wor, wout = (w[layer].astype(BF16) for w in (w_o_attn, w_o_rnn, w_out))
    wfi, wfo = w_ffn_in[layer].astype(BF16), w_ffn_out[layer].astype(BF16)
    lru = (conv_w[layer], row2(conv_b), wrg, row2(b_rg), wig, row2(b_ig), row2(lru_lambda))
    post_w = (expand, b_gate[layer], woa, wor, wout, g2, wfi, wfo)

    xp = x_prompt[0]
    *qkv, xb, gb, ga, gb2, kvt = _inproj(xp, g1, w_in_b, qg, kg, mavg, tm=256, tail=MAX_WINDOW,
                                         qkv_dtype=BF16, dilations=DILATIONS)
    parts = [_attn_prompt(*qkv[3 * g:3 * g + 3], g, d) for g, d in enumerate(DILATIONS)]
    hb, conv8, h_last = _rglru_prompt(xb, gb, *lru, tr=256)
    y_p = _post(parts, DILATIONS, hb, ga, gb2, xp, *post_w, tm=256)

    xs = x_sample[:, 0]
    *qkvs, xbs, gbs, gas, gb2s, kvts = _inproj(xs, g1, w_in_b, qg, kg, mavg, tm=nb, tail=nb,
                                               qkv_dtype=F32, dilations=(1,) * N_GROUPS)
    qs, ks, vs = (jnp.stack(qkvs[j::3]) for j in range(3))
    caches = [jnp.transpose(c[layer], (0, 2, 3, 4, 1)).reshape(nb, 2 * GROUP_W, c.shape[2])
              for c in (cache_kv_w128, cache_kv_w512, cache_kv_w2048)]
    attn_s = _attn_sample(qs, ks, vs, caches)
    hbs, conv_s, h_s = _rglru_sample(xbs, gbs, state_conv[layer].reshape(nb, -1), state_h[layer], *lru)
    y_s = _post([attn_s], (1,), hbs, gas, gb2s, xs, *post_w, tm=nb)

    kv_shape = lambda rows: (1, 1, rows, 2, HEADS_PER_GROUP, HEAD_DIM)
    kv_prompt = [kvt[g, MAX_WINDOW - min(w, seq):].reshape(kv_shape(min(w, seq)))
                 for g, (w, _) in enumerate(GROUPS)]
    kv_sample = [kvts[g].reshape(1, nb, 1, 2, HEADS_PER_GROUP, HEAD_DIM) for g in range(N_GROUPS)]
    return (y_p[None], y_s[:, None],
            kv_prompt[0], kv_prompt[1], kv_prompt[2],
            conv8[SUBLANES_V7X - (CONV_W - 1):][None, None], h_last[None],
            kv_sample[0], kv_sample[1], kv_sample[2],
            conv_s.reshape(1, nb, CONV_W - 1, D_RNN), h_s[None])
```

```python
import functools

import jax
import jax.numpy as jnp
from jax import lax
from jax.experimental import pallas as pl
from jax.experimental.pallas import tpu as pltpu

F32 = jnp.float32
BF16 = jnp.bfloat16

D_MODEL = 1024
HEAD_DIM = 64
HEADS_PER_GROUP = 8
GROUPS = ((128, 1), (512, 4), (2048, 16))
DILATIONS = tuple(d for _, d in GROUPS)
N_GROUPS = len(GROUPS)
GROUP_W = HEADS_PER_GROUP * HEAD_DIM
QKV_WIDTH = N_GROUPS * GROUP_W
BLOCK = 128
ATTN_SCALE = HEAD_DIM ** -0.5
NEG_INF = -1e30
D_RNN = 1280
RNN_BLOCKS = 10
RNN_BLOCK_W = D_RNN // RNN_BLOCKS
CONV_W = 4
LRU_C = 8.0
D_FF = 2816
RMS_EPS = 1e-6
MAX_WINDOW = max(w for w, _ in GROUPS)

LANES_V7X = 128
SUBLANES_V7X = 8
MXU_DIM_V7X = 256
VMEM_LIMIT_BYTES = 56 * 1024 * 1024

OFF_Q, OFF_K, OFF_V = 0, QKV_WIDTH, 2 * QKV_WIDTH
OFF_XB = 3 * QKV_WIDTH
OFF_GB = OFF_XB + D_RNN
OFF_GA = OFF_GB + D_RNN
OFF_GB2 = OFF_GA + D_MODEL


def _mm(a, b):
    return jnp.dot(a, b, preferred_element_type=F32)


def _mm_nt(a, b):
    return lax.dot_general(a, b, (((1,), (1,)), ((), ())), preferred_element_type=F32)


def _rms_norm_rows(x, g):
    return x * lax.rsqrt(jnp.mean(x * x, axis=-1, keepdims=True) + RMS_EPS) * g


def _const_spec(shape):
    nd = len(shape)
    return pl.BlockSpec(shape, lambda *_: (0,) * nd, pipeline_mode=pl.Buffered(1))


def _params(sem):
    return pltpu.CompilerParams(dimension_semantics=sem, vmem_limit_bytes=VMEM_LIMIT_BYTES)


def _inproj_kernel(x_ref, g1_ref, w_ref, qg_ref, kg_ref, mavg_ref, *refs, dilations, tail_first_step,
                   fuse_lru):
    if fuse_lru:
        lru_refs, refs = refs[:7], refs[7:]
    qkv_refs = refs[:3 * N_GROUPS]
    if fuse_lru:
        (hb_ref, conv_ref, hlast_ref, ga_ref, gb2_ref, kvt_ref,
         hn_ref, hbs_ref, ctail_ref, hcar_ref) = refs[3 * N_GROUPS:]
    else:
        xb_ref, gbr_ref, ga_ref, gb2_ref, kvt_ref, hn_ref = refs[3 * N_GROUPS:]
    step = pl.program_id(0)
    tm = x_ref.shape[0]
    hn32 = _rms_norm_rows(x_ref[...], g1_ref[...])
    hn = hn32.astype(BF16)
    n_slabs = D_MODEL // LANES_V7X
    for c in range(n_slabs):
        hn_ref[c] = hn32[:, c * LANES_V7X:(c + 1) * LANES_V7X]
    mavg = mavg_ref[...]

    def strided_rows(start, size, stride):
        return jnp.concatenate([hn_ref[c, pl.ds(start, size, stride=stride), :] for c in range(n_slabs)],
                               axis=1)

    if fuse_lru:
        @pl.when(step == 0)
        def _():
            ctail_ref[...] = jnp.zeros(ctail_ref.shape, F32)
            hcar_ref[...] = jnp.zeros(hcar_ref.shape, F32)

        nj = tm // SUBLANES_V7X
        ht = jnp.concatenate([strided_rows(j, SUBLANES_V7X, nj) for j in range(nj)], axis=0).astype(BF16)
        lru_stages = _lru_tile(_mm(ht, w_ref[:, OFF_XB:OFF_XB + D_RNN]),
                               _mm(ht, w_ref[:, OFF_GB:OFF_GB + D_RNN]), lru_refs, ctail_ref, hcar_ref)
    else:
        lru_stages = iter(())
        xb_ref[...] = _mm(hn, w_ref[:, OFF_XB:OFF_XB + D_RNN])
        gbr_ref[...] = _mm(hn, w_ref[:, OFF_GB:OFF_GB + D_RNN])

    def head_norm(t, gain):
        tt = (t * t).astype(BF16)
        ms = jnp.concatenate(
            [_mm(tt[:, c:c + MXU_DIM_V7X], mavg) for c in range(0, GROUP_W, MXU_DIM_V7X)], axis=1)
        return t * lax.rsqrt(ms + RMS_EPS) * gain

    def qkv(h, g):
        c = g * GROUP_W
        qn = head_norm(_mm(h, w_ref[:, OFF_Q + c:OFF_Q + c + GROUP_W]), qg_ref[...]) * ATTN_SCALE
        kn = head_norm(_mm(h, w_ref[:, OFF_K + c:OFF_K + c + GROUP_W]), kg_ref[...])
        vv = _mm(h, w_ref[:, OFF_V + c:OFF_V + c + GROUP_W])
        return qn, kn, vv

    for g, d in enumerate(dilations):
        hb_rows = next(lru_stages, None)
        rows = tm // d
        if d == 1:
            hg = hn
        else:
            hg = jnp.concatenate([strided_rows(r, rows, d) for r in range(d)], axis=0).astype(BF16)
        parts = qkv(hg, g)
        for t, o_ref in zip(parts, qkv_refs[3 * g:3 * g + 3]):
            for r in range(d):
                o_ref[:, r * GROUP_W:(r + 1) * GROUP_W] = t[r * rows:(r + 1) * rows].astype(o_ref.dtype)

    if fuse_lru:
        conv_ref[...] = ctail_ref[...]
        hlast_ref[...] = hcar_ref[...]
        for j, rows in enumerate(hb_rows):
            for c in range(D_RNN // LANES_V7X):
                hbs_ref[c, pl.ds(j, SUBLANES_V7X, stride=nj), :] = rows[:, c * LANES_V7X:(c + 1) * LANES_V7X]
        hb_ref[...] = jnp.concatenate([hbs_ref[c] for c in range(D_RNN // LANES_V7X)],
                                      axis=1).astype(hb_ref.dtype)

    ga_ref[...] = _mm(hn, w_ref[:, OFF_GA:OFF_GA + D_MODEL])
    gb2_ref[...] = _mm(hn, w_ref[:, OFF_GB2:OFF_GB2 + D_MODEL])

    @pl.when(step >= tail_first_step)
    def _():
        for g in range(N_GROUPS):
            _, kn, vv = qkv(hn, g)
            kvt_ref[g, :, 0:GROUP_W] = kn
            kvt_ref[g, :, GROUP_W:2 * GROUP_W] = vv


def _inproj(x, g1, w_in, qg, kg, mavg, lru=None, *, tm, tail, qkv_dtype, dilations):
    m = x.shape[0]
    nt = m // tm
    tail_first_step = nt - tail // tm
    row = lambda w: pl.BlockSpec((tm, w), lambda i: (i, 0))
    fixed = lambda r, w: pl.BlockSpec((r, w), lambda i: (0, 0))
    qkv_specs, qkv_shapes = [], []
    for d in dilations:
        qkv_specs += [pl.BlockSpec((tm // d, d * GROUP_W), lambda i: (i, 0))] * 3
        qkv_shapes += [jax.ShapeDtypeStruct((m // d, d * GROUP_W), qkv_dtype)] * 3
    kvt_spec = pl.BlockSpec((N_GROUPS, tm, 2 * GROUP_W),
                            lambda i: (0, jnp.maximum(i - tail_first_step, 0), 0))
    scratch = [pltpu.VMEM((D_MODEL // LANES_V7X, tm, LANES_V7X), F32)]
    tail_rows = (CONV_W - 1) * SUBLANES_V7X
    if lru is None:
        lru = ()
        rnn_specs = (row(D_RNN), row(D_RNN))
        rnn_shapes = (jax.ShapeDtypeStruct((m, D_RNN), F32),) * 2
    else:
        rnn_specs = (row(D_RNN), fixed(tail_rows, D_RNN), fixed(1, D_RNN))
        rnn_shapes = (jax.ShapeDtypeStruct((m, D_RNN), BF16),
                      jax.ShapeDtypeStruct((tail_rows, D_RNN), F32),
                      jax.ShapeDtypeStruct((1, D_RNN), F32))
        scratch += [pltpu.VMEM((D_RNN // LANES_V7X, tm, LANES_V7X), F32),
                    pltpu.VMEM((tail_rows, D_RNN), F32),
                    pltpu.VMEM((1, D_RNN), F32)]
    out_shape = tuple(qkv_shapes) + rnn_shapes + (
        jax.ShapeDtypeStruct((m, D_MODEL), F32),
        jax.ShapeDtypeStruct((m, D_MODEL), F32),
        jax.ShapeDtypeStruct((N_GROUPS, tail, 2 * GROUP_W), F32),
    )
    consts = (g1, w_in, qg, kg, mavg) + tuple(lru)
    return pl.pallas_call(
        functools.partial(_inproj_kernel, dilations=dilations, tail_first_step=tail_first_step,
                          fuse_lru=bool(lru)),
        grid=(nt,),
        in_specs=[row(D_MODEL)] + [_const_spec(c.shape) for c in consts],
        out_specs=tuple(qkv_specs) + rnn_specs + (row(D_MODEL), row(D_MODEL), kvt_spec),
        out_shape=out_shape,
        scratch_shapes=scratch,
        compiler_params=_params(("arbitrary",)),
        name="inproj_lru" if lru else "inproj",
    )(x, *consts)


ATTN_BLOCKS_PER_STEP = 8


def _attn_prompt_kernel(q_ref, kc_ref, kp_ref, vc_ref, vp_ref, o_ref, lse_ref):
    step = pl.program_id(1)
    qi = lax.broadcasted_iota(jnp.int32, (BLOCK, 2 * BLOCK), 0)
    kj = lax.broadcasted_iota(jnp.int32, (BLOCK, 2 * BLOCK), 1)
    dist = BLOCK + qi - kj
    band = (dist >= 0) & (dist <= BLOCK)
    first_band = band & ((step > 0) | (kj >= BLOCK))
    lane = lax.broadcasted_iota(jnp.int32, (1, LANES_V7X), 1)
    stat_lane = lax.broadcasted_iota(jnp.int32, (BLOCK, LANES_V7X), 1)
    zero = jnp.zeros((), BF16)

    for blk in range(q_ref.shape[0] // BLOCK):
        rows = slice(blk * BLOCK, (blk + 1) * BLOCK)
        prev_rows = slice((blk - 1) * BLOCK, blk * BLOCK)
        valid = band if blk else first_band
        lse_tile = jnp.zeros((BLOCK, LANES_V7X), F32)
        for pair in range(HEADS_PER_GROUP // 2):
            cols = slice(pair * LANES_V7X, (pair + 1) * LANES_V7X)
            qp = q_ref[rows, cols]
            k_prev = kc_ref[prev_rows, cols] if blk else kp_ref[:, cols]
            v_prev = vc_ref[prev_rows, cols] if blk else vp_ref[:, cols]
            kk = jnp.concatenate([k_prev, kc_ref[rows, cols]], axis=0)
            vv = jnp.concatenate([v_prev, vc_ref[rows, cols]], axis=0)
            o_pair = jnp.zeros((BLOCK, LANES_V7X), F32)
            for e in range(2):
                in_head = (lane >= e * HEAD_DIM) & (lane < (e + 1) * HEAD_DIM)
                s = _mm_nt(jnp.where(in_head, qp, zero), kk)
                s = jnp.where(valid, s, NEG_INF)
                mx = jnp.max(s, axis=-1, keepdims=True)
                p = jnp.exp(s - mx)
                den = jnp.sum(p, axis=-1, keepdims=True)
                pv = _mm(p.astype(BF16), jnp.where(in_head, vv, zero))
                o_pair = o_pair + pv * (1.0 / den)
                lse_tile = jnp.where(stat_lane == 2 * pair + e, mx + jnp.log(den), lse_tile)
            o_ref[rows, cols] = o_pair.astype(o_ref.dtype)
        lse_ref[rows, :] = lse_tile


def _attn_prompt(q, k, v, g, dilation):
    m_len = q.shape[0]
    nbs = ATTN_BLOCKS_PER_STEP
    rows = nbs * BLOCK
    cur = pl.BlockSpec((rows, GROUP_W), lambda r, b: (b, r))
    prev = pl.BlockSpec((BLOCK, GROUP_W), lambda r, b: (jnp.maximum(b * nbs - 1, 0), r))
    return pl.pallas_call(
        _attn_prompt_kernel,
        grid=(dilation, m_len // rows),
        in_specs=[cur, cur, prev, cur, prev],
        out_specs=(pl.BlockSpec((rows, GROUP_W), lambda r, b: (b, r)),
                   pl.BlockSpec((rows, LANES_V7X), lambda r, b: (b, r))),
        out_shape=(jax.ShapeDtypeStruct((m_len, dilation * GROUP_W), BF16),
                   jax.ShapeDtypeStruct((m_len, dilation * LANES_V7X), F32)),
        compiler_params=_params(("arbitrary", "arbitrary")),
        name=f"attn_prompt_g{g}",
    )(q, k, k, v, v)


def _attn_sample_kernel(q_ref, k_ref, v_ref, c0_ref, c1_ref, c2_ref, o_ref):
    b = pl.program_id(0)
    head_row = lax.broadcasted_iota(jnp.int32, (HEADS_PER_GROUP, GROUP_W), 0)
    head_lane = lax.broadcasted_iota(jnp.int32, (HEADS_PER_GROUP, GROUP_W), 1) // HEAD_DIM
    own = head_row == head_lane
    parts = []
    for g, c_ref in enumerate((c0_ref, c1_ref, c2_ref)):
        window, dilation = GROUPS[g]
        bf = lambda t: t.astype(BF16).astype(F32)
        qmat = jnp.where(own, jnp.broadcast_to(q_ref[g, pl.ds(b, 1), :], (HEADS_PER_GROUP, GROUP_W)), 0.0)
        qmat = qmat.astype(BF16)
        knew = bf(k_ref[g, pl.ds(b, 1), :])
        vnew = bf(v_ref[g, pl.ds(b, 1), :])
        pos = lax.broadcasted_iota(jnp.int32, (HEADS_PER_GROUP, window), 1)
        s = _mm(qmat, c_ref[0:GROUP_W, :].astype(BF16))
        s = jnp.where((pos & (dilation - 1)) == 0, s, NEG_INF)
        s_new = jnp.sum(qmat.astype(F32) * knew, axis=-1, keepdims=True)
        mx = jnp.maximum(jnp.max(s, axis=-1, keepdims=True), s_new)
        p = jnp.exp(s - mx)
        p_new = jnp.exp(s_new - mx)
        den = jnp.sum(p, axis=-1, keepdims=True) + p_new
        numer = _mm_nt(p.astype(BF16), c_ref[GROUP_W:2 * GROUP_W, :].astype(BF16)) + bf(p_new) * vnew
        parts.append((jnp.where(own, numer, 0.0), mx, den))
    m_all = functools.reduce(jnp.maximum, [mx for _, mx, _ in parts])
    ws = [jnp.exp(mx - m_all) for _, mx, _ in parts]
    num = sum(n * w for (n, _, _), w in zip(parts, ws))
    den = sum(d * w for (_, _, d), w in zip(parts, ws))
    o_ref[pl.ds(b, 1), :] = jnp.sum(num / den, axis=0, keepdims=True)


def _attn_sample(q, k, v, caches):
    nb = q.shape[1]
    specs = [pl.BlockSpec((None,) + c.shape[1:], lambda b: (b, 0, 0)) for c in caches]
    full = pl.BlockSpec((N_GROUPS, nb, GROUP_W), lambda b: (0, 0, 0))
    return pl.pallas_call(
        _attn_sample_kernel,
        grid=(nb,),
        in_specs=[full, full, full] + specs,
        out_specs=pl.BlockSpec((nb, GROUP_W), lambda b: (0, 0)),
        out_shape=jax.ShapeDtypeStruct((nb, GROUP_W), F32),
        compiler_params=_params(("arbitrary",)),
        name="attn_sample",
    )(q, k, v, *caches)


def _gelu_tanh(x):
    cdf = 0.5 * (1.0 + jnp.tanh(0.7978845608028654 * (x + 0.044715 * (x * x * x))))
    return x * cdf


def _softplus(x):
    return jnp.maximum(x, 0.0) + jnp.log1p(jnp.exp(-jnp.abs(x)))


def _lru_gate_logits(xc, wrg_ref, brg_ref, wig_ref, big_ref):
    xcb = xc.astype(BF16)

    def logits(w_ref, b_ref):
        cols = [_mm(xcb[:, n * MXU_DIM_V7X:(n + 1) * MXU_DIM_V7X], w_ref[n])
                for n in range(D_RNN // MXU_DIM_V7X)]
        return jnp.concatenate(cols, axis=1) + b_ref[...]

    return logits(wrg_ref, brg_ref), logits(wig_ref, big_ref)


def _lru_coeffs(xc, r_logits, i_logits, lam_ref):
    r = jax.nn.sigmoid(r_logits)
    i = jax.nn.sigmoid(i_logits)
    log_a = -LRU_C * r * _softplus(-lam_ref[...])
    a = jnp.exp(log_a)
    t = jnp.tanh(-log_a)
    y = 2.0 * t / (1.0 + t)
    root = jnp.where(y > 0.0, y * lax.rsqrt(y), 0.0)
    return a, root * i * xc


def _pair_blocks(w):
    z = jnp.zeros_like(w[0::2])
    top = jnp.concatenate([w[0::2], z], axis=2)
    bot = jnp.concatenate([z, w[1::2]], axis=2)
    return jnp.concatenate([top, bot], axis=1)


def _lru_tile(xb, gb, lru_refs, ctail_ref, hcar_ref):
    cw_ref, cb_ref, wrg_ref, brg_ref, wig_ref, big_ref, lam_ref = lru_refs
    sl = SUBLANES_V7X
    tm = xb.shape[0]
    nj = tm // sl
    sub = lax.broadcasted_iota(jnp.int32, (sl, D_RNN), 0)
    vrow = lambda t, j: t[j * sl:(j + 1) * sl]

    taps = CONV_W - 1
    wrap = [pltpu.roll(jnp.where(sub == sl - 1, vrow(ctail_ref[...], i), vrow(xb, nj - taps + i)), 1, 0)
            for i in range(taps)]
    ctail_ref[...] = xb[tm - taps * sl:]
    xc = cb_ref[...] + xb * cw_ref[taps:taps + 1, :]
    for k in range(1, CONV_W):
        shifted = jnp.concatenate(wrap[taps - k:] + [xb[:tm - k * sl]], axis=0)
        xc = xc + shifted * cw_ref[taps - k:taps - k + 1, :]

    r_logits, i_logits = _lru_gate_logits(xc, wrg_ref, brg_ref, wig_ref, big_ref)
    yield
    a, b = _lru_coeffs(xc, r_logits, i_logits, lam_ref)
    gate = _gelu_tanh(gb)
    yield

    hl, acc = vrow(b, 0), vrow(a, 0)
    hls, accs = [hl], [acc]
    for j in range(1, nj):
        hl = vrow(a, j) * hl + vrow(b, j)
        acc = vrow(a, j) * acc
        hls.append(hl)
        accs.append(acc)

    for s in (1, 2, 4):
        keep = sub >= s
        acc_prev = jnp.where(keep, pltpu.roll(acc, s, 0), 1.0)
        hl_prev = jnp.where(keep, pltpu.roll(hl, s, 0), 0.0)
        hl = acc * hl_prev + hl
        acc = acc * acc_prev
    carry = hcar_ref[...]
    h_end = hl + acc * carry
    h_in = jnp.where(sub == 0, carry, pltpu.roll(h_end, 1, 0))
    hcar_ref[...] = h_end[sl - 1:sl]

    yield [(hls[j] + accs[j] * h_in) * vrow(gate, j) for j in range(nj)]


def _rglru_sample_kernel(xb_ref, gb_ref, sc_ref, h0_ref, cw_ref, cb_ref, wrg_ref, brg_ref, wig_ref,
                         big_ref, lam_ref, hb_ref, conv_ref, h_ref):
    xb = xb_ref[...]
    taps = [sc_ref[:, j * D_RNN:(j + 1) * D_RNN] for j in range(CONV_W - 1)] + [xb]
    xc = cb_ref[...] + sum(t * cw_ref[j:j + 1, :] for j, t in enumerate(taps))
    a, b = _lru_coeffs(xc, *_lru_gate_logits(xc, wrg_ref, brg_ref, wig_ref, big_ref), lam_ref)
    h = a * h0_ref[...] + b
    h_ref[...] = h
    hb_ref[...] = (h * _gelu_tanh(gb_ref[...])).astype(hb_ref.dtype)
    for j in range(CONV_W - 1):
        conv_ref[:, j * D_RNN:(j + 1) * D_RNN] = taps[j + 1]


def _rglru_sample(xb, gb, state_conv, h0, cw, cb, wrg, brg, wig, big, lam):
    nb = xb.shape[0]
    args = (xb, gb, state_conv, h0, cw, cb, wrg, brg, wig, big, lam)
    return pl.pallas_call(
        _rglru_sample_kernel,
        grid=(1,),
        in_specs=[_const_spec(a.shape) for a in args],
        out_specs=(_const_spec((nb, D_RNN)), _const_spec((nb, (CONV_W - 1) * D_RNN)),
                   _const_spec((nb, D_RNN))),
        out_shape=(jax.ShapeDtypeStruct((nb, D_RNN), BF16),
                   jax.ShapeDtypeStruct((nb, (CONV_W - 1) * D_RNN), F32),
                   jax.ShapeDtypeStruct((nb, D_RNN), F32)),
        compiler_params=_params(("arbitrary",)),
        name="rglru_sample",
    )(*args)


def _post_kernel(*refs, dilations):
    n_parts = len(dilations)
    attn_refs = refs[:2 * n_parts] if n_parts > 1 else refs[:1]
    rest = refs[len(attn_refs):]
    (hb_ref, ga_ref, gb2_ref, x_ref, expand_ref, bg_ref, woa_ref, wor_ref, wout_ref, g2_ref,
     wfi_ref, wfo_ref, y_ref) = rest[:13]
    tm = x_ref.shape[0]

    if n_parts > 1:
        o_nat_ref, lse_nat_ref = rest[13:]

        def natural(src_ref, dst_ref, width, d):
            if d == 1:
                return src_ref[...].astype(F32)
            n_slabs = width // LANES_V7X
            for r in range(d):
                for c in range(n_slabs):
                    lanes = slice(r * width + c * LANES_V7X, r * width + (c + 1) * LANES_V7X)
                    dst_ref[c, pl.ds(r, tm // d, stride=d), :] = src_ref[:, lanes].astype(F32)
            return jnp.concatenate([dst_ref[c] for c in range(n_slabs)], axis=1)

        o_refs, lse_refs = attn_refs[:n_parts], attn_refs[n_parts:]
        lses = [natural(r, lse_nat_ref, LANES_V7X, d) for r, d in zip(lse_refs, dilations)]
        mx = functools.reduce(jnp.maximum, lses)
        es = [jnp.exp(l - mx) for l in lses]
        inv = 1.0 / sum(es)
        attn = 0.0
        for o_ref, e, d in zip(o_refs, es, dilations):
            cw = e * inv
            hi = cw.astype(BF16)
            lo = (cw - hi.astype(F32)).astype(BF16)
            wide = _mm(hi, expand_ref[...]) + _mm(lo, expand_ref[...])
            attn = attn + wide * natural(o_ref, o_nat_ref, GROUP_W, d)
    else:
        attn = attn_refs[0][...]

    ya = _mm(attn.astype(BF16), woa_ref[...])
    yb = _mm(hb_ref[...], wor_ref[...])
    merged = (jax.nn.sigmoid(ga_ref[...] + bg_ref[0:1, :]) * ya
              + jax.nn.sigmoid(gb2_ref[...] + bg_ref[1:2, :]) * yb)
    x1 = x_ref[...] + _mm(merged.astype(BF16), wout_ref[...])
    hn2 = _rms_norm_rows(x1, g2_ref[...]).astype(BF16)
    gu = _mm(hn2, wfi_ref[...])
    act = jax.nn.silu(gu[:, :D_FF]) * gu[:, D_FF:]
    y_ref[...] = x1 + _mm(act.astype(BF16), wfo_ref[...])


def _post(attn_parts, dilations, hb, ga, gb2, x, expand, bg, woa, wor, wout, g2, wfi, wfo, *, tm):
    m = x.shape[0]
    row = lambda w: pl.BlockSpec((tm, w), lambda i: (i, 0))
    n_parts = len(attn_parts)
    scratch = []
    if n_parts > 1:
        blocked = lambda w, d: pl.BlockSpec((tm // d, d * w), lambda i: (i, 0))
        attn_args = [o for o, _ in attn_parts] + [l for _, l in attn_parts]
        attn_specs = ([blocked(GROUP_W, d) for d in dilations]
                      + [blocked(LANES_V7X, d) for d in dilations])
        scratch = [pltpu.VMEM((GROUP_W // LANES_V7X, tm, LANES_V7X), F32),
                   pltpu.VMEM((1, tm, LANES_V7X), F32)]
    else:
        attn_args, attn_specs = list(attn_parts), [row(GROUP_W)]
    consts = (expand, bg, woa, wor, wout, g2, wfi, wfo)
    return pl.pallas_call(
        functools.partial(_post_kernel, dilations=dilations),
        grid=(m // tm,),
        in_specs=attn_specs + [row(D_RNN), row(D_MODEL), row(D_MODEL), row(D_MODEL)]
        + [_const_spec(c.shape) for c in consts],
        out_specs=row(D_MODEL),
        out_shape=jax.ShapeDtypeStruct((m, D_MODEL), F32),
        scratch_shapes=scratch,
        compiler_params=_params(("arbitrary",)),
        name="post_prompt" if n_parts > 1 else "post_sample",
    )(*attn_args, hb, ga, gb2, x, *consts)


def kernel(x_prompt, x_sample, cache_kv_w128, cache_kv_w512, cache_kv_w2048, state_conv, state_h,
           norm1_g, w_in, b_gate, q_norm_g, k_norm_g, conv_w, conv_b, w_rg, b_rg, w_ig, b_ig,
           lru_lambda, w_o_attn, w_o_rnn, w_out, norm2_g, w_ffn_in, w_ffn_out):
    assert x_prompt.shape[0] == 1 and norm1_g.shape[0] == 1 and x_sample.shape[1] == 1
    seq = x_prompt.shape[1]
    nb = x_sample.shape[0]
    layer = 0

    idx = jnp.arange(MXU_DIM_V7X) // HEAD_DIM
    mavg = jnp.where(idx[:, None] == idx[None, :], 1.0 / HEAD_DIM, 0.0).astype(BF16)
    expand = (jnp.arange(LANES_V7X)[:, None] == (jnp.arange(GROUP_W) // HEAD_DIM)[None, :]).astype(BF16)

    row2 = lambda t: t[layer].reshape(1, -1)
    g1, g2 = row2(norm1_g), row2(norm2_g)
    qg = jnp.tile(q_norm_g[layer], HEADS_PER_GROUP).reshape(1, GROUP_W)
    kg = jnp.tile(k_norm_g[layer], HEADS_PER_GROUP).reshape(1, GROUP_W)
    w_in_b = w_in[layer].astype(BF16)
    wrg, wig = _pair_blocks(w_rg[layer].astype(BF16)), _pair_blocks(w_ig[layer].astype(BF16))
    woa, wor, wout = (w[layer].astype(BF16) for w in (w_o_attn, w_o_rnn, w_out))
    wfi, wfo = w_ffn_in[layer].astype(BF16), w_ffn_out[layer].astype(BF16)
    lru = (conv_w[layer], row2(conv_b), wrg, row2(b_rg), wig, row2(b_ig), row2(lru_lambda))
    post_w = (expand, b_gate[layer], woa, wor, wout, g2, wfi, wfo)

    xp = x_prompt[0]
    *qkv, hb, conv_rows, h_last, ga, gb2, kvt = _inproj(
        xp, g1, w_in_b, qg, kg, mavg, lru, tm=256, tail=MAX_WINDOW, qkv_dtype=BF16, dilations=DILATIONS)
    parts = [_attn_prompt(*qkv[3 * g:3 * g + 3], g, d) for g, d in enumerate(DILATIONS)]
    y_p = _post(parts, DILATIONS, hb, ga, gb2, xp, *post_w, tm=256)
    conv_p = conv_rows[SUBLANES_V7X - 1::SUBLANES_V7X]

    xs = x_sample[:, 0]
    *qkvs, xbs, gbs, gas, gb2s, kvts = _inproj(xs, g1, w_in_b, qg, kg, mavg, tm=nb, tail=nb,
                                               qkv_dtype=F32, dilations=(1,) * N_GROUPS)
    qs, ks, vs = (jnp.stack(qkvs[j::3]) for j in range(3))
    caches = [jnp.transpose(c[layer], (0, 2, 3, 4, 1)).reshape(nb, 2 * GROUP_W, c.shape[2])
              for c in (cache_kv_w128, cache_kv_w512, cache_kv_w2048)]
    attn_s = _attn_sample(qs, ks, vs, caches)
    hbs, conv_s, h_s = _rglru_sample(xbs, gbs, state_conv[layer].reshape(nb, -1), state_h[layer], *lru)
    y_s = _post([attn_s], (1,), hbs, gas, gb2s, xs, *post_w, tm=nb)

    kv_shape = lambda rows: (1, 1, rows, 2, HEADS_PER_GROUP, HEAD_DIM)
    kv_prompt = [kvt[g, MAX_WINDOW - min(w, seq):].reshape(kv_shape(min(w, seq)))
                 for g, (w, _) in enumerate(GROUPS)]
    kv_sample = [kvts[g].reshape(1, nb, 1, 2, HEADS_PER_GROUP, HEAD_DIM) for g in range(N_GROUPS)]
    return (y_p[None], y_s[:, None],
            kv_prompt[0], kv_prompt[1], kv_prompt[2],
            conv_p[None, None], h_last[None],
            kv_sample[0], kv_sample[1], kv_sample[2],
            conv_s.reshape(1, nb, CONV_W - 1, D_RNN), h_s[None])
```

```python
import functools

import jax
import jax.numpy as jnp
from jax import lax
from jax.experimental import pallas as pl
from jax.experimental.pallas import tpu as pltpu

F32 = jnp.float32
BF16 = jnp.bfloat16

D_MODEL = 1024
HEAD_DIM = 64
HEADS_PER_GROUP = 8
GROUPS = ((128, 1), (512, 4), (2048, 16))
DILATIONS = tuple(d for _, d in GROUPS)
N_GROUPS = len(GROUPS)
GROUP_W = HEADS_PER_GROUP * HEAD_DIM
QKV_WIDTH = N_GROUPS * GROUP_W
BLOCK = 128
ATTN_SCALE = HEAD_DIM ** -0.5
LOG2_E = 1.4426950408889634
NEG_INF = -1e30
D_RNN = 1280
RNN_BLOCKS = 10
RNN_BLOCK_W = D_RNN // RNN_BLOCKS
CONV_W = 4
LRU_C = 8.0
D_FF = 2816
RMS_EPS = 1e-6
MAX_WINDOW = max(w for w, _ in GROUPS)

LANES_V7X = 128
SUBLANES_V7X = 8
MXU_DIM_V7X = 256
VMEM_LIMIT_BYTES = 56 * 1024 * 1024

OFF_Q, OFF_K, OFF_V = 0, QKV_WIDTH, 2 * QKV_WIDTH
OFF_XB = 3 * QKV_WIDTH
OFF_GB = OFF_XB + D_RNN
OFF_GA = OFF_GB + D_RNN
OFF_GB2 = OFF_GA + D_MODEL


def _mm(a, b):
    return jnp.dot(a, b, preferred_element_type=F32)


def _mm_nt(a, b):
    return lax.dot_general(a, b, (((1,), (1,)), ((), ())), preferred_element_type=F32)


def _rms_norm_rows(x, g):
    return x * lax.rsqrt(jnp.mean(x * x, axis=-1, keepdims=True) + RMS_EPS) * g


def _const_spec(shape):
    nd = len(shape)
    return pl.BlockSpec(shape, lambda *_: (0,) * nd, pipeline_mode=pl.Buffered(1))


def _params(sem):
    return pltpu.CompilerParams(dimension_semantics=sem, vmem_limit_bytes=VMEM_LIMIT_BYTES)


def _inproj_kernel(x_ref, g1_ref, w_ref, qg_ref, kg_ref, mavg_ref, *refs, dilations, tail_first_step,
                   fuse_lru):
    if fuse_lru:
        lru_refs, refs = refs[:7], refs[7:]
    qkv_refs = refs[:3 * N_GROUPS]
    if fuse_lru:
        (hb_ref, conv_ref, hlast_ref, ga_ref, gb2_ref, kvt_ref,
         hn_ref, hbs_ref, hnp_ref, ctail_ref, hcar_ref) = refs[3 * N_GROUPS:]
    else:
        xb_ref, gbr_ref, ga_ref, gb2_ref, kvt_ref, hn_ref = refs[3 * N_GROUPS:]
    step = pl.program_id(0)
    tm = x_ref.shape[0]
    hn32 = _rms_norm_rows(x_ref[...], g1_ref[...])
    hn = hn32.astype(BF16)
    n_slabs = D_MODEL // LANES_V7X
    for c in range(n_slabs):
        hn_ref[c] = hn32[:, c * LANES_V7X:(c + 1) * LANES_V7X]
    mavg = mavg_ref[...]

    def strided_rows(start, size, stride):
        return jnp.concatenate([hn_ref[c, pl.ds(start, size, stride=stride), :] for c in range(n_slabs)],
                               axis=1)

    if fuse_lru:
        @pl.when(step == 0)
        def _():
            ctail_ref[...] = jnp.zeros(ctail_ref.shape, F32)
            hcar_ref[...] = jnp.zeros(hcar_ref.shape, F32)

        nj = tm // SUBLANES_V7X
        pitch = nj + SUBLANES_V7X
        for s in range(SUBLANES_V7X):
            for c in range(n_slabs):
                hnp_ref[c, s * pitch:s * pitch + nj, :] = hn32[s * nj:(s + 1) * nj, c * LANES_V7X:(c + 1) * LANES_V7X]
        ht = jnp.concatenate(
            [jnp.concatenate([hnp_ref[c, pl.ds(j, SUBLANES_V7X, stride=pitch), :] for c in range(n_slabs)],
                             axis=1) for j in range(nj)], axis=0).astype(BF16)
        lru_stages = _lru_tile(_mm(ht, w_ref[:, OFF_XB:OFF_XB + D_RNN]),
                               _mm(ht, w_ref[:, OFF_GB:OFF_GB + D_RNN]), lru_refs, ctail_ref, hcar_ref)
    else:
        lru_stages = iter(())
        xb_ref[...] = _mm(hn, w_ref[:, OFF_XB:OFF_XB + D_RNN])
        gbr_ref[...] = _mm(hn, w_ref[:, OFF_GB:OFF_GB + D_RNN])

    def head_norm(t, gain):
        tt = (t * t).astype(BF16)
        ms = jnp.concatenate(
            [_mm(tt[:, c:c + MXU_DIM_V7X], mavg) for c in range(0, GROUP_W, MXU_DIM_V7X)], axis=1)
        return t * lax.rsqrt(ms + RMS_EPS) * gain

    def qkv(h, g):
        c = g * GROUP_W
        qn = head_norm(_mm(h, w_ref[:, OFF_Q + c:OFF_Q + c + GROUP_W]), qg_ref[...])
        kn = head_norm(_mm(h, w_ref[:, OFF_K + c:OFF_K + c + GROUP_W]), kg_ref[...])
        vv = _mm(h, w_ref[:, OFF_V + c:OFF_V + c + GROUP_W])
        return qn, kn, vv

    for g, d in enumerate(dilations):
        hb_rows = next(lru_stages, None)
        rows = tm // d
        if d == 1:
            hg = hn
        else:
            hg = jnp.concatenate([strided_rows(r, rows, d) for r in range(d)], axis=0).astype(BF16)
        parts = qkv(hg, g)
        for t, o_ref in zip(parts, qkv_refs[3 * g:3 * g + 3]):
            for r in range(d):
                o_ref[:, r * GROUP_W:(r + 1) * GROUP_W] = t[r * rows:(r + 1) * rows].astype(o_ref.dtype)

    if fuse_lru:
        conv_ref[...] = ctail_ref[...]
        hlast_ref[...] = hcar_ref[...]
        for j, rows in enumerate(hb_rows):
            for c in range(D_RNN // LANES_V7X):
                hbs_ref[c, pl.ds(j, SUBLANES_V7X, stride=pitch), :] = rows[:, c * LANES_V7X:(c + 1) * LANES_V7X]
        hb_ref[...] = jnp.concatenate(
            [jnp.concatenate([hbs_ref[c, s * pitch:s * pitch + nj, :] for s in range(SUBLANES_V7X)], axis=0)
             for c in range(D_RNN // LANES_V7X)], axis=1).astype(hb_ref.dtype)

    ga_ref[...] = _mm(hn, w_ref[:, OFF_GA:OFF_GA + D_MODEL])
    gb2_ref[...] = _mm(hn, w_ref[:, OFF_GB2:OFF_GB2 + D_MODEL])

    @pl.when(step >= tail_first_step)
    def _():
        for g in range(N_GROUPS):
            _, kn, vv = qkv(hn, g)
            kvt_ref[g, :, 0:GROUP_W] = kn
            kvt_ref[g, :, GROUP_W:2 * GROUP_W] = vv


def _inproj(x, g1, w_in, qg, kg, mavg, lru=None, *, tm, tail, qkv_dtype, dilations):
    m = x.shape[0]
    nt = m // tm
    tail_first_step = nt - tail // tm
    row = lambda w: pl.BlockSpec((tm, w), lambda i: (i, 0))
    fixed = lambda r, w: pl.BlockSpec((r, w), lambda i: (0, 0))
    qkv_specs, qkv_shapes = [], []
    for d in dilations:
        qkv_specs += [pl.BlockSpec((tm // d, d * GROUP_W), lambda i: (i, 0))] * 3
        qkv_shapes += [jax.ShapeDtypeStruct((m // d, d * GROUP_W), qkv_dtype)] * 3
    kvt_spec = pl.BlockSpec((N_GROUPS, tm, 2 * GROUP_W),
                            lambda i: (0, jnp.maximum(i - tail_first_step, 0), 0))
    scratch = [pltpu.VMEM((D_MODEL // LANES_V7X, tm, LANES_V7X), F32)]
    tail_rows = (CONV_W - 1) * SUBLANES_V7X
    if lru is None:
        lru = ()
        rnn_specs = (row(D_RNN), row(D_RNN))
        rnn_shapes = (jax.ShapeDtypeStruct((m, D_RNN), F32),) * 2
    else:
        rnn_specs = (row(D_RNN), fixed(tail_rows, D_RNN), fixed(1, D_RNN))
        rnn_shapes = (jax.ShapeDtypeStruct((m, D_RNN), BF16),
                      jax.ShapeDtypeStruct((tail_rows, D_RNN), F32),
                      jax.ShapeDtypeStruct((1, D_RNN), F32))
        padded = tm + SUBLANES_V7X * SUBLANES_V7X
        scratch += [pltpu.VMEM((D_RNN // LANES_V7X, padded, LANES_V7X), F32),
                    pltpu.VMEM((D_MODEL // LANES_V7X, padded, LANES_V7X), F32),
                    pltpu.VMEM((tail_rows, D_RNN), F32),
                    pltpu.VMEM((1, D_RNN), F32)]
    out_shape = tuple(qkv_shapes) + rnn_shapes + (
        jax.ShapeDtypeStruct((m, D_MODEL), F32),
        jax.ShapeDtypeStruct((m, D_MODEL), F32),
        jax.ShapeDtypeStruct((N_GROUPS, tail, 2 * GROUP_W), F32),
    )
    consts = (g1, w_in, qg, kg, mavg) + tuple(lru)
    return pl.pallas_call(
        functools.partial(_inproj_kernel, dilations=dilations, tail_first_step=tail_first_step,
                          fuse_lru=bool(lru)),
        grid=(nt,),
        in_specs=[row(D_MODEL)] + [_const_spec(c.shape) for c in consts],
        out_specs=tuple(qkv_specs) + rnn_specs + (row(D_MODEL), row(D_MODEL), kvt_spec),
        out_shape=out_shape,
        scratch_shapes=scratch,
        compiler_params=_params(("arbitrary",)),
        name="inproj_lru" if lru else "inproj",
    )(x, *consts)


ATTN_BLOCKS_PER_STEP = 8


def _attn_prompt_kernel(q_ref, kc_ref, kp_ref, vc_ref, vp_ref, o_ref, lse_ref):
    step = pl.program_id(1)
    qi = lax.broadcasted_iota(jnp.int32, (BLOCK, 2 * BLOCK), 0)
    kj = lax.broadcasted_iota(jnp.int32, (BLOCK, 2 * BLOCK), 1)
    dist = BLOCK + qi - kj
    band = (dist >= 0) & (dist <= BLOCK)
    first_band = band & ((step > 0) | (kj >= BLOCK))
    band2 = jnp.concatenate([band, band], axis=0)
    first_band2 = jnp.concatenate([first_band, first_band], axis=0)
    first_head = lax.broadcasted_iota(jnp.int32, (1, LANES_V7X), 1) < HEAD_DIM
    stat_lane = lax.broadcasted_iota(jnp.int32, (BLOCK, LANES_V7X), 1)
    zero = jnp.zeros((), BF16)

    for blk in range(q_ref.shape[0] // BLOCK):
        rows = slice(blk * BLOCK, (blk + 1) * BLOCK)
        prev_rows = slice((blk - 1) * BLOCK, blk * BLOCK)
        valid = band2 if blk else first_band2
        lse_tile = jnp.zeros((BLOCK, LANES_V7X), F32)
        for pair in range(HEADS_PER_GROUP // 2):
            cols = slice(pair * LANES_V7X, (pair + 1) * LANES_V7X)
            qp = q_ref[rows, cols]
            k_prev = kc_ref[prev_rows, cols] if blk else kp_ref[:, cols]
            v_prev = vc_ref[prev_rows, cols] if blk else vp_ref[:, cols]
            kk = jnp.concatenate([k_prev, kc_ref[rows, cols]], axis=0)
            vv = jnp.concatenate([v_prev, vc_ref[rows, cols]], axis=0)
            q2 = jnp.concatenate([jnp.where(first_head, qp, zero), jnp.where(first_head, zero, qp)], axis=0)
            s = jnp.where(valid, _mm_nt(q2, kk), NEG_INF)
            mx = jnp.max(s, axis=-1, keepdims=True)
            p = jnp.exp2(s - mx)
            den = jnp.sum(p, axis=-1, keepdims=True)
            pv = _mm(p.astype(BF16), vv) * (1.0 / den)
            o_ref[rows, cols] = jnp.where(first_head, pv[:BLOCK], pv[BLOCK:]).astype(o_ref.dtype)
            lse = mx + jnp.log2(den)
            lse_tile = jnp.where(stat_lane == 2 * pair, lse[:BLOCK],
                                 jnp.where(stat_lane == 2 * pair + 1, lse[BLOCK:], lse_tile))
        lse_ref[rows, :] = lse_tile


def _attn_prompt(q, k, v, g, dilation):
    m_len = q.shape[0]
    nbs = ATTN_BLOCKS_PER_STEP
    rows = nbs * BLOCK
    cur = pl.BlockSpec((rows, GROUP_W), lambda r, b: (b, r))
    prev = pl.BlockSpec((BLOCK, GROUP_W), lambda r, b: (jnp.maximum(b * nbs - 1, 0), r))
    return pl.pallas_call(
        _attn_prompt_kernel,
        grid=(dilation, m_len // rows),
        in_specs=[cur, cur, prev, cur, prev],
        out_specs=(pl.BlockSpec((rows, GROUP_W), lambda r, b: (b, r)),
                   pl.BlockSpec((rows, LANES_V7X), lambda r, b: (b, r))),
        out_shape=(jax.ShapeDtypeStruct((m_len, dilation * GROUP_W), BF16),
                   jax.ShapeDtypeStruct((m_len, dilation * LANES_V7X), F32)),
        compiler_params=_params(("arbitrary", "arbitrary")),
        name=f"attn_prompt_g{g}",
    )(q, k, k, v, v)


def _attn_sample_kernel(q_ref, k_ref, v_ref, c0_ref, c1_ref, c2_ref, o_ref):
    b = pl.program_id(0)
    head_row = lax.broadcasted_iota(jnp.int32, (HEADS_PER_GROUP, GROUP_W), 0)
    head_lane = lax.broadcasted_iota(jnp.int32, (HEADS_PER_GROUP, GROUP_W), 1) // HEAD_DIM
    own = head_row == head_lane
    parts = []
    for g, c_ref in enumerate((c0_ref, c1_ref, c2_ref)):
        window, dilation = GROUPS[g]
        bf = lambda t: t.astype(BF16).astype(F32)
        qmat = jnp.where(own, jnp.broadcast_to(q_ref[g, pl.ds(b, 1), :], (HEADS_PER_GROUP, GROUP_W)), 0.0)
        qmat = qmat.astype(BF16)
        knew = bf(k_ref[g, pl.ds(b, 1), :])
        vnew = bf(v_ref[g, pl.ds(b, 1), :])
        pos = lax.broadcasted_iota(jnp.int32, (HEADS_PER_GROUP, window), 1)
        s = _mm(qmat, c_ref[0:GROUP_W, :].astype(BF16))
        s = jnp.where((pos & (dilation - 1)) == 0, s, NEG_INF)
        s_new = jnp.sum(qmat.astype(F32) * knew, axis=-1, keepdims=True)
        mx = jnp.maximum(jnp.max(s, axis=-1, keepdims=True), s_new)
        p = jnp.exp2(s - mx)
        p_new = jnp.exp2(s_new - mx)
        den = jnp.sum(p, axis=-1, keepdims=True) + p_new
        numer = _mm_nt(p.astype(BF16), c_ref[GROUP_W:2 * GROUP_W, :].astype(BF16)) + bf(p_new) * vnew
        parts.append((jnp.where(own, numer, 0.0), mx, den))
    m_all = functools.reduce(jnp.maximum, [mx for _, mx, _ in parts])
    ws = [jnp.exp2(mx - m_all) for _, mx, _ in parts]
    num = sum(n * w for (n, _, _), w in zip(parts, ws))
    den = sum(d * w for (_, _, d), w in zip(parts, ws))
    o_ref[pl.ds(b, 1), :] = jnp.sum(num / den, axis=0, keepdims=True)


def _attn_sample(q, k, v, caches):
    nb = q.shape[1]
    specs = [pl.BlockSpec((None,) + c.shape[1:], lambda b: (b, 0, 0)) for c in caches]
    full = pl.BlockSpec((N_GROUPS, nb, GROUP_W), lambda b: (0, 0, 0))
    return pl.pallas_call(
        _attn_sample_kernel,
        grid=(nb,),
        in_specs=[full, full, full] + specs,
        out_specs=pl.BlockSpec((nb, GROUP_W), lambda b: (0, 0)),
        out_shape=jax.ShapeDtypeStruct((nb, GROUP_W), F32),
        compiler_params=_params(("arbitrary",)),
        name="attn_sample",
    )(q, k, v, *caches)


def _gelu_tanh(x):
    cdf = 0.5 * (1.0 + jnp.tanh(0.7978845608028654 * (x + 0.044715 * (x * x * x))))
    return x * cdf


def _softplus(x):
    return jnp.maximum(x, 0.0) + jnp.log1p(jnp.exp(-jnp.abs(x)))


def _lru_gate_logits(xc, wrg_ref, brg_ref, wig_ref, big_ref):
    xcb = xc.astype(BF16)

    def logits(w_ref, b_ref):
        cols = [_mm(xcb[:, n * MXU_DIM_V7X:(n + 1) * MXU_DIM_V7X], w_ref[n])
                for n in range(D_RNN // MXU_DIM_V7X)]
        return jnp.concatenate(cols, axis=1) + b_ref[...]

    return logits(wrg_ref, brg_ref), logits(wig_ref, big_ref)


def _lru_coeffs(xc, r_logits, i_logits, lam_ref):
    sigmoid = lambda v: 0.5 + 0.5 * jnp.tanh(0.5 * v)
    r = sigmoid(r_logits)
    i = sigmoid(i_logits)
    neg_rate = LRU_C * _softplus(-lam_ref[...])
    a = jnp.exp2(r * (-LOG2_E * neg_rate))
    t = jnp.tanh(r * neg_rate)
    y = 2.0 * t / (1.0 + t)
    root = jnp.where(y > 0.0, y * lax.rsqrt(y), 0.0)
    return a, root * i * xc


def _pair_blocks(w):
    z = jnp.zeros_like(w[0::2])
    top = jnp.concatenate([w[0::2], z], axis=2)
    bot = jnp.concatenate([z, w[1::2]], axis=2)
    return jnp.concatenate([top, bot], axis=1)


def _lru_tile(xb, gb, lru_refs, ctail_ref, hcar_ref):
    cw_ref, cb_ref, wrg_ref, brg_ref, wig_ref, big_ref, lam_ref = lru_refs
    sl = SUBLANES_V7X
    tm = xb.shape[0]
    nj = tm // sl
    sub = lax.broadcasted_iota(jnp.int32, (sl, D_RNN), 0)
    vrow = lambda t, j: t[j * sl:(j + 1) * sl]

    taps = CONV_W - 1
    wrap = [pltpu.roll(jnp.where(sub == sl - 1, vrow(ctail_ref[...], i), vrow(xb, nj - taps + i)), 1, 0)
            for i in range(taps)]
    ctail_ref[...] = xb[tm - taps * sl:]
    xc = cb_ref[...] + xb * cw_ref[taps:taps + 1, :]
    for k in range(1, CONV_W):
        shifted = jnp.concatenate(wrap[taps - k:] + [xb[:tm - k * sl]], axis=0)
        xc = xc + shifted * cw_ref[taps - k:taps - k + 1, :]

    r_logits, i_logits = _lru_gate_logits(xc, wrg_ref, brg_ref, wig_ref, big_ref)
    yield
    a, b = _lru_coeffs(xc, r_logits, i_logits, lam_ref)
    gate = _gelu_tanh(gb)
    yield

    hl, acc = vrow(b, 0), vrow(a, 0)
    hls, accs = [hl], [acc]
    for j in range(1, nj):
        hl = vrow(a, j) * hl + vrow(b, j)
        acc = vrow(a, j) * acc
        hls.append(hl)
        accs.append(acc)

    for s in (1, 2, 4):
        keep = sub >= s
        acc_prev = jnp.where(keep, pltpu.roll(acc, s, 0), 1.0)
        hl_prev = jnp.where(keep, pltpu.roll(hl, s, 0), 0.0)
        hl = acc * hl_prev + hl
        acc = acc * acc_prev
    carry = hcar_ref[...]
    h_end = hl + acc * carry
    h_in = jnp.where(sub == 0, carry, pltpu.roll(h_end, 1, 0))
    hcar_ref[...] = h_end[sl - 1:sl]

    yield [(hls[j] + accs[j] * h_in) * vrow(gate, j) for j in range(nj)]


def _rglru_sample_kernel(xb_ref, gb_ref, sc_ref, h0_ref, cw_ref, cb_ref, wrg_ref, brg_ref, wig_ref,
                         big_ref, lam_ref, hb_ref, conv_ref, h_ref):
    xb = xb_ref[...]
    taps = [sc_ref[:, j * D_RNN:(j + 1) * D_RNN] for j in range(CONV_W - 1)] + [xb]
    xc = cb_ref[...] + sum(t * cw_ref[j:j + 1, :] for j, t in enumerate(taps))
    a, b = _lru_coeffs(xc, *_lru_gate_logits(xc, wrg_ref, brg_ref, wig_ref, big_ref), lam_ref)
    h = a * h0_ref[...] + b
    h_ref[...] = h
    hb_ref[...] = (h * _gelu_tanh(gb_ref[...])).astype(hb_ref.dtype)
    for j in range(CONV_W - 1):
        conv_ref[:, j * D_RNN:(j + 1) * D_RNN] = taps[j + 1]


def _rglru_sample(xb, gb, state_conv, h0, cw, cb, wrg, brg, wig, big, lam):
    nb = xb.shape[0]
    args = (xb, gb, state_conv, h0, cw, cb, wrg, brg, wig, big, lam)
    return pl.pallas_call(
        _rglru_sample_kernel,
        grid=(1,),
        in_specs=[_const_spec(a.shape) for a in args],
        out_specs=(_const_spec((nb, D_RNN)), _const_spec((nb, (CONV_W - 1) * D_RNN)),
                   _const_spec((nb, D_RNN))),
        out_shape=(jax.ShapeDtypeStruct((nb, D_RNN), BF16),
                   jax.ShapeDtypeStruct((nb, (CONV_W - 1) * D_RNN), F32),
                   jax.ShapeDtypeStruct((nb, D_RNN), F32)),
        compiler_params=_params(("arbitrary",)),
        name="rglru_sample",
    )(*args)


def _post_kernel(*refs, dilations):
    n_parts = len(dilations)
    attn_refs = refs[:2 * n_parts] if n_parts > 1 else refs[:1]
    rest = refs[len(attn_refs):]
    (hb_ref, ga_ref, gb2_ref, x_ref, expand_ref, bg_ref, woa_ref, wor_ref, wout_ref, g2_ref,
     wfi_ref, wfo_ref, y_ref) = rest[:13]
    tm = x_ref.shape[0]

    if n_parts > 1:
        o_nat_ref, lse_nat_ref = rest[13:]

        def natural(src_ref, dst_ref, width, d):
            if d == 1:
                return src_ref[...].astype(F32)
            n_slabs = width // LANES_V7X
            for r in range(d):
                for c in range(n_slabs):
                    lanes = slice(r * width + c * LANES_V7X, r * width + (c + 1) * LANES_V7X)
                    dst_ref[c, pl.ds(r, tm // d, stride=d), :] = src_ref[:, lanes].astype(F32)
            return jnp.concatenate([dst_ref[c] for c in range(n_slabs)], axis=1)

        o_refs, lse_refs = attn_refs[:n_parts], attn_refs[n_parts:]
        lses = [natural(r, lse_nat_ref, LANES_V7X, d) for r, d in zip(lse_refs, dilations)]
        mx = functools.reduce(jnp.maximum, lses)
        es = [jnp.exp2(l - mx) for l in lses]
        inv = 1.0 / sum(es)
        attn = 0.0
        for o_ref, e, d in zip(o_refs, es, dilations):
            cw = e * inv
            hi = cw.astype(BF16)
            lo = (cw - hi.astype(F32)).astype(BF16)
            wide = _mm(hi, expand_ref[...]) + _mm(lo, expand_ref[...])
            attn = attn + wide * natural(o_ref, o_nat_ref, GROUP_W, d)
    else:
        attn = attn_refs[0][...]

    ya = _mm(attn.astype(BF16), woa_ref[...])
    yb = _mm(hb_ref[...], wor_ref[...])
    merged = (jax.nn.sigmoid(ga_ref[...] + bg_ref[0:1, :]) * ya
              + jax.nn.sigmoid(gb2_ref[...] + bg_ref[1:2, :]) * yb)
    x1 = x_ref[...] + _mm(merged.astype(BF16), wout_ref[...])
    hn2 = _rms_norm_rows(x1, g2_ref[...]).astype(BF16)
    gu = _mm(hn2, wfi_ref[...])
    act = jax.nn.silu(gu[:, :D_FF]) * gu[:, D_FF:]
    y_ref[...] = x1 + _mm(act.astype(BF16), wfo_ref[...])


def _post(attn_parts, dilations, hb, ga, gb2, x, expand, bg, woa, wor, wout, g2, wfi, wfo, *, tm):
    m = x.shape[0]
    row = lambda w: pl.BlockSpec((tm, w), lambda i: (i, 0))
    n_parts = len(attn_parts)
    scratch = []
    if n_parts > 1:
        blocked = lambda w, d: pl.BlockSpec((tm // d, d * w), lambda i: (i, 0))
        attn_args = [o for o, _ in attn_parts] + [l for _, l in attn_parts]
        attn_specs = ([blocked(GROUP_W, d) for d in dilations]
                      + [blocked(LANES_V7X, d) for d in dilations])
        scratch = [pltpu.VMEM((GROUP_W // LANES_V7X, tm, LANES_V7X), F32),
                   pltpu.VMEM((1, tm, LANES_V7X), F32)]
    else:
        attn_args, attn_specs = list(attn_parts), [row(GROUP_W)]
    consts = (expand, bg, woa, wor, wout, g2, wfi, wfo)
    return pl.pallas_call(
        functools.partial(_post_kernel, dilations=dilations),
        grid=(m // tm,),
        in_specs=attn_specs + [row(D_RNN), row(D_MODEL), row(D_MODEL), row(D_MODEL)]
        + [_const_spec(c.shape) for c in consts],
        out_specs=row(D_MODEL),
        out_shape=jax.ShapeDtypeStruct((m, D_MODEL), F32),
        scratch_shapes=scratch,
        compiler_params=_params(("arbitrary",)),
        name="post_prompt" if n_parts > 1 else "post_sample",
    )(*attn_args, hb, ga, gb2, x, *consts)


def kernel(x_prompt, x_sample, cache_kv_w128, cache_kv_w512, cache_kv_w2048, state_conv, state_h,
           norm1_g, w_in, b_gate, q_norm_g, k_norm_g, conv_w, conv_b, w_rg, b_rg, w_ig, b_ig,
           lru_lambda, w_o_attn, w_o_rnn, w_out, norm2_g, w_ffn_in, w_ffn_out):
    assert x_prompt.shape[0] == 1 and norm1_g.shape[0] == 1 and x_sample.shape[1] == 1
    seq = x_prompt.shape[1]
    nb = x_sample.shape[0]
    layer = 0

    idx = jnp.arange(MXU_DIM_V7X) // HEAD_DIM
    mavg = jnp.where(idx[:, None] == idx[None, :], 1.0 / HEAD_DIM, 0.0).astype(BF16)
    expand = (jnp.arange(LANES_V7X)[:, None] == (jnp.arange(GROUP_W) // HEAD_DIM)[None, :]).astype(BF16)

    row2 = lambda t: t[layer].reshape(1, -1)
    g1, g2 = row2(norm1_g), row2(norm2_g)
    qg = jnp.tile(q_norm_g[layer], HEADS_PER_GROUP).reshape(1, GROUP_W) * (ATTN_SCALE * LOG2_E)
    kg = jnp.tile(k_norm_g[layer], HEADS_PER_GROUP).reshape(1, GROUP_W)
    w_in_b = w_in[layer].astype(BF16)
    wrg, wig = _pair_blocks(w_rg[layer].astype(BF16)), _pair_blocks(w_ig[layer].astype(BF16))
    woa, wor, wout = (w[layer].astype(BF16) for w in (w_o_attn, w_o_rnn, w_out))
    wfi, wfo = w_ffn_in[layer].astype(BF16), w_ffn_out[layer].astype(BF16)
    lru = (conv_w[layer], row2(conv_b), wrg, row2(b_rg), wig, row2(b_ig), row2(lru_lambda))
    post_w = (expand, b_gate[layer], woa, wor, wout, g2, wfi, wfo)

    xp = x_prompt[0]
    *qkv, hb, conv_rows, h_last, ga, gb2, kvt = _inproj(
        xp, g1, w_in_b, qg, kg, mavg, lru, tm=256, tail=MAX_WINDOW, qkv_dtype=BF16, dilations=DILATIONS)
    parts = [_attn_prompt(*qkv[3 * g:3 * g + 3], g, d) for g, d in enumerate(DILATIONS)]
    y_p = _post(parts, DILATIONS, hb, ga, gb2, xp, *post_w, tm=256)
    conv_p = conv_rows[SUBLANES_V7X - 1::SUBLANES_V7X]

    xs = x_sample[:, 0]
    *qkvs, xbs, gbs, gas, gb2s, kvts = _inproj(xs, g1, w_in_b, qg, kg, mavg, tm=nb, tail=nb,
                                               qkv_dtype=F32, dilations=(1,) * N_GROUPS)
    qs, ks, vs = (jnp.stack(qkvs[j::3]) for j in range(3))
    caches = [jnp.transpose(c[layer], (0, 2, 3, 4, 1)).reshape(nb, 2 * GROUP_W, c.shape[2])
              for c in (cache_kv_w128, cache_kv_w512, cache_kv_w2048)]
    attn_s = _attn_sample(qs, ks, vs, caches)
    hbs, conv_s, h_s = _rglru_sample(xbs, gbs, state_conv[layer].reshape(nb, -1), state_h[layer], *lru)
    y_s = _post([attn_s], (1,), hbs, gas, gb2s, xs, *post_w, tm=nb)

    kv_shape = lambda rows: (1, 1, rows, 2, HEADS_PER_GROUP, HEAD_DIM)
    kv_prompt = [kvt[g, MAX_WINDOW - min(w, seq):].reshape(kv_shape(min(w, seq)))
                 for g, (w, _) in enumerate(GROUPS)]
    kv_sample = [kvts[g].reshape(1, nb, 1, 2, HEADS_PER_GROUP, HEAD_DIM) for g in range(N_GROUPS)]
    return (y_p[None], y_s[:, None],
            kv_prompt[0], kv_prompt[1], kv_prompt[2],
            conv_p[None, None], h_last[None],
            kv_sample[0], kv_sample[1], kv_sample[2],
            conv_s.reshape(1, nb, CONV_W - 1, D_RNN), h_s[None])
```

```python
import functools

import jax
import jax.numpy as jnp
from jax import lax
from jax.experimental import pallas as pl
from jax.experimental.pallas import tpu as pltpu

F32 = jnp.float32
BF16 = jnp.bfloat16

D_MODEL = 1024
HEAD_DIM = 64
HEADS_PER_GROUP = 8
GROUPS = ((128, 1), (512, 4), (2048, 16))
DILATIONS = tuple(d for _, d in GROUPS)
N_GROUPS = len(GROUPS)
GROUP_W = HEADS_PER_GROUP * HEAD_DIM
QKV_WIDTH = N_GROUPS * GROUP_W
BLOCK = 128
ATTN_SCALE = HEAD_DIM ** -0.5
LOG2_E = 1.4426950408889634
NEG_INF = -1e30
D_RNN = 1280
RNN_BLOCKS = 10
RNN_BLOCK_W = D_RNN // RNN_BLOCKS
CONV_W = 4
LRU_C = 8.0
D_FF = 2816
RMS_EPS = 1e-6
MAX_WINDOW = max(w for w, _ in GROUPS)

LANES_V7X = 128
SUBLANES_V7X = 8
MXU_DIM_V7X = 256
VMEM_LIMIT_BYTES = 56 * 1024 * 1024

OFF_Q, OFF_K, OFF_V = 0, QKV_WIDTH, 2 * QKV_WIDTH
OFF_XB = 3 * QKV_WIDTH
OFF_GB = OFF_XB + D_RNN
OFF_GA = OFF_GB + D_RNN
OFF_GB2 = OFF_GA + D_MODEL


def _mm(a, b):
    return jnp.dot(a, b, preferred_element_type=F32)


def _mm_nt(a, b):
    return lax.dot_general(a, b, (((1,), (1,)), ((), ())), preferred_element_type=F32)


def _rms_norm_rows(x, g):
    return x * lax.rsqrt(jnp.mean(x * x, axis=-1, keepdims=True) + RMS_EPS) * g


def _const_spec(shape):
    nd = len(shape)
    return pl.BlockSpec(shape, lambda *_: (0,) * nd, pipeline_mode=pl.Buffered(1))


def _params(sem):
    return pltpu.CompilerParams(dimension_semantics=sem, vmem_limit_bytes=VMEM_LIMIT_BYTES)


def _inproj_kernel(x_ref, g1_ref, w_ref, qg_ref, kg_ref, *refs, dilations, tail_first_step,
                   fuse_lru):
    if fuse_lru:
        lru_refs, refs = refs[:7], refs[7:]
    qkv_refs = refs[:3 * N_GROUPS]
    if fuse_lru:
        (hb_ref, conv_ref, hlast_ref, ga_ref, gb2_ref, kvt_ref,
         hn_ref, hbs_ref, hnp_ref, ctail_ref, hcar_ref) = refs[3 * N_GROUPS:]
    else:
        xb_ref, gbr_ref, ga_ref, gb2_ref, kvt_ref, hn_ref = refs[3 * N_GROUPS:]
    step = pl.program_id(0)
    tm = x_ref.shape[0]
    hn32 = _rms_norm_rows(x_ref[...], g1_ref[...])
    hn = hn32.astype(BF16)
    n_slabs = D_MODEL // LANES_V7X
    for c in range(n_slabs):
        hn_ref[c] = hn32[:, c * LANES_V7X:(c + 1) * LANES_V7X]

    def strided_rows(start, size, stride):
        return jnp.concatenate([hn_ref[c, pl.ds(start, size, stride=stride), :] for c in range(n_slabs)],
                               axis=1)

    if fuse_lru:
        @pl.when(step == 0)
        def _():
            ctail_ref[...] = jnp.zeros(ctail_ref.shape, F32)
            hcar_ref[...] = jnp.zeros(hcar_ref.shape, F32)

        nj = tm // SUBLANES_V7X
        pitch = nj + SUBLANES_V7X
        for s in range(SUBLANES_V7X):
            for c in range(n_slabs):
                hnp_ref[c, s * pitch:s * pitch + nj, :] = hn32[s * nj:(s + 1) * nj, c * LANES_V7X:(c + 1) * LANES_V7X]
        ht = jnp.concatenate(
            [jnp.concatenate([hnp_ref[c, pl.ds(j, SUBLANES_V7X, stride=pitch), :] for c in range(n_slabs)],
                             axis=1) for j in range(nj)], axis=0).astype(BF16)
        per_vreg = LRU_CHUNK // LANES_V7X
        for n in range(N_LRU_CHUNKS):
            c0 = n * LRU_CHUNK
            hb_rows = _lru_tile(_mm(ht, w_ref[:, OFF_XB + c0:OFF_XB + c0 + LRU_CHUNK]),
                                _mm(ht, w_ref[:, OFF_GB + c0:OFF_GB + c0 + LRU_CHUNK]),
                                n, lru_refs, ctail_ref, hcar_ref)
            for j, rows in enumerate(hb_rows):
                for c in range(per_vreg):
                    hbs_ref[n * per_vreg + c, pl.ds(j, SUBLANES_V7X, stride=pitch), :] = (
                        rows[:, c * LANES_V7X:(c + 1) * LANES_V7X])
        conv_ref[...] = ctail_ref[...]
        hlast_ref[...] = hcar_ref[...]
        hb_ref[...] = jnp.concatenate(
            [jnp.concatenate([hbs_ref[c, s * pitch:s * pitch + nj, :] for s in range(SUBLANES_V7X)], axis=0)
             for c in range(D_RNN // LANES_V7X)], axis=1).astype(hb_ref.dtype)
    else:
        xb_ref[...] = _mm(hn, w_ref[:, OFF_XB:OFF_XB + D_RNN])
        gbr_ref[...] = _mm(hn, w_ref[:, OFF_GB:OFF_GB + D_RNN])

    def head_norm(t, gain):
        first_head = lax.broadcasted_iota(jnp.int32, (1, LANES_V7X), 1) < HEAD_DIM
        cols = []
        for c in range(0, GROUP_W, LANES_V7X):
            x = t[:, c:c + LANES_V7X]
            xx = x * x
            s0 = jnp.sum(jnp.where(first_head, xx, 0.0), axis=-1, keepdims=True)
            s1 = jnp.sum(jnp.where(first_head, 0.0, xx), axis=-1, keepdims=True)
            ms = jnp.where(first_head, s0, s1) * (1.0 / HEAD_DIM)
            cols.append(x * lax.rsqrt(ms + RMS_EPS))
        return jnp.concatenate(cols, axis=1) * gain

    def qkv(h, g):
        c = g * GROUP_W
        qn = head_norm(_mm(h, w_ref[:, OFF_Q + c:OFF_Q + c + GROUP_W]), qg_ref[...])
        kn = head_norm(_mm(h, w_ref[:, OFF_K + c:OFF_K + c + GROUP_W]), kg_ref[...])
        vv = _mm(h, w_ref[:, OFF_V + c:OFF_V + c + GROUP_W])
        return qn, kn, vv

    in_order = {}
    for g, d in enumerate(dilations):
        rows = tm // d
        if d == 1:
            hg = hn
        else:
            hg = jnp.concatenate([strided_rows(r, rows, d) for r in range(d)], axis=0).astype(BF16)
        parts = qkv(hg, g)
        if d == 1:
            in_order[g] = parts
        for t, o_ref in zip(parts, qkv_refs[3 * g:3 * g + 3]):
            for r in range(d):
                o_ref[:, r * GROUP_W:(r + 1) * GROUP_W] = t[r * rows:(r + 1) * rows].astype(o_ref.dtype)

    ga_ref[...] = _mm(hn, w_ref[:, OFF_GA:OFF_GA + D_MODEL])
    gb2_ref[...] = _mm(hn, w_ref[:, OFF_GB2:OFF_GB2 + D_MODEL])

    @pl.when(step >= tail_first_step)
    def _():
        for g in range(N_GROUPS):
            _, kn, vv = in_order[g] if g in in_order else qkv(hn, g)
            kvt_ref[g, :, 0:GROUP_W] = kn
            kvt_ref[g, :, GROUP_W:2 * GROUP_W] = vv


def _inproj(x, g1, w_in, qg, kg, lru=None, *, tm, tail, qkv_dtype, dilations):
    m = x.shape[0]
    nt = m // tm
    tail_first_step = nt - tail // tm
    row = lambda w: pl.BlockSpec((tm, w), lambda i: (i, 0))
    fixed = lambda r, w: pl.BlockSpec((r, w), lambda i: (0, 0))
    qkv_specs, qkv_shapes = [], []
    for d in dilations:
        qkv_specs += [pl.BlockSpec((tm // d, d * GROUP_W), lambda i: (i, 0))] * 3
        qkv_shapes += [jax.ShapeDtypeStruct((m // d, d * GROUP_W), qkv_dtype)] * 3
    kvt_spec = pl.BlockSpec((N_GROUPS, tm, 2 * GROUP_W),
                            lambda i: (0, jnp.maximum(i - tail_first_step, 0), 0))
    scratch = [pltpu.VMEM((D_MODEL // LANES_V7X, tm, LANES_V7X), F32)]
    tail_rows = (CONV_W - 1) * SUBLANES_V7X
    if lru is None:
        lru = ()
        rnn_specs = (row(D_RNN), row(D_RNN))
        rnn_shapes = (jax.ShapeDtypeStruct((m, D_RNN), F32),) * 2
    else:
        rnn_specs = (row(D_RNN), fixed(tail_rows, D_RNN), fixed(1, D_RNN))
        rnn_shapes = (jax.ShapeDtypeStruct((m, D_RNN), BF16),
                      jax.ShapeDtypeStruct((tail_rows, D_RNN), F32),
                      jax.ShapeDtypeStruct((1, D_RNN), F32))
        padded = tm + SUBLANES_V7X * SUBLANES_V7X
        scratch += [pltpu.VMEM((D_RNN // LANES_V7X, padded, LANES_V7X), F32),
                    pltpu.VMEM((D_MODEL // LANES_V7X, padded, LANES_V7X), F32),
                    pltpu.VMEM((tail_rows, D_RNN), F32),
                    pltpu.VMEM((1, D_RNN), F32)]
    out_shape = tuple(qkv_shapes) + rnn_shapes + (
        jax.ShapeDtypeStruct((m, D_MODEL), F32),
        jax.ShapeDtypeStruct((m, D_MODEL), F32),
        jax.ShapeDtypeStruct((N_GROUPS, tail, 2 * GROUP_W), F32),
    )
    consts = (g1, w_in, qg, kg) + tuple(lru)
    return pl.pallas_call(
        functools.partial(_inproj_kernel, dilations=dilations, tail_first_step=tail_first_step,
                          fuse_lru=bool(lru)),
        grid=(nt,),
        in_specs=[row(D_MODEL)] + [_const_spec(c.shape) for c in consts],
        out_specs=tuple(qkv_specs) + rnn_specs + (row(D_MODEL), row(D_MODEL), kvt_spec),
        out_shape=out_shape,
        scratch_shapes=scratch,
        compiler_params=_params(("arbitrary",)),
        name="inproj_lru" if lru else "inproj",
    )(x, *consts)


ATTN_BLOCKS_PER_STEP = 8


def _attn_prompt_kernel(q_ref, kc_ref, kp_ref, vc_ref, vp_ref, o_ref, lse_ref):
    step = pl.program_id(1)
    qi = lax.broadcasted_iota(jnp.int32, (BLOCK, 2 * BLOCK), 0)
    kj = lax.broadcasted_iota(jnp.int32, (BLOCK, 2 * BLOCK), 1)
    dist = BLOCK + qi - kj
    band = (dist >= 0) & (dist <= BLOCK)
    first_band = band & ((step > 0) | (kj >= BLOCK))
    band2 = jnp.concatenate([band, band], axis=0)
    first_band2 = jnp.concatenate([first_band, first_band], axis=0)
    first_head = lax.broadcasted_iota(jnp.int32, (1, LANES_V7X), 1) < HEAD_DIM
    stat_lane = lax.broadcasted_iota(jnp.int32, (BLOCK, LANES_V7X), 1)
    zero = jnp.zeros((), BF16)

    for blk in range(q_ref.shape[0] // BLOCK):
        rows = slice(blk * BLOCK, (blk + 1) * BLOCK)
        prev_rows = slice((blk - 1) * BLOCK, blk * BLOCK)
        valid = band2 if blk else first_band2
        lse_tile = jnp.zeros((BLOCK, LANES_V7X), F32)
        for pair in range(HEADS_PER_GROUP // 2):
            cols = slice(pair * LANES_V7X, (pair + 1) * LANES_V7X)
            qp = q_ref[rows, cols]
            k_prev = kc_ref[prev_rows, cols] if blk else kp_ref[:, cols]
            v_prev = vc_ref[prev_rows, cols] if blk else vp_ref[:, cols]
            kk = jnp.concatenate([k_prev, kc_ref[rows, cols]], axis=0)
            vv = jnp.concatenate([v_prev, vc_ref[rows, cols]], axis=0)
            q2 = jnp.concatenate([jnp.where(first_head, qp, zero), jnp.where(first_head, zero, qp)], axis=0)
            s = jnp.where(valid, _mm_nt(q2, kk), NEG_INF)
            mx = jnp.max(s, axis=-1, keepdims=True)
            p = jnp.exp2(s - mx)
            den = jnp.sum(p, axis=-1, keepdims=True)
            pv = _mm(p.astype(BF16), vv) * (1.0 / den)
            o_ref[rows, cols] = jnp.where(first_head, pv[:BLOCK], pv[BLOCK:]).astype(o_ref.dtype)
            lse = mx + jnp.log2(den)
            lse_tile = jnp.where(stat_lane == 2 * pair, lse[:BLOCK],
                                 jnp.where(stat_lane == 2 * pair + 1, lse[BLOCK:], lse_tile))
        lse_ref[rows, :] = lse_tile


def _attn_prompt(q, k, v, g, dilation):
    m_len = q.shape[0]
    nbs = ATTN_BLOCKS_PER_STEP
    rows = nbs * BLOCK
    cur = pl.BlockSpec((rows, GROUP_W), lambda r, b: (b, r))
    prev = pl.BlockSpec((BLOCK, GROUP_W), lambda r, b: (jnp.maximum(b * nbs - 1, 0), r))
    return pl.pallas_call(
        _attn_prompt_kernel,
        grid=(dilation, m_len // rows),
        in_specs=[cur, cur, prev, cur, prev],
        out_specs=(pl.BlockSpec((rows, GROUP_W), lambda r, b: (b, r)),
                   pl.BlockSpec((rows, LANES_V7X), lambda r, b: (b, r))),
        out_shape=(jax.ShapeDtypeStruct((m_len, dilation * GROUP_W), BF16),
                   jax.ShapeDtypeStruct((m_len, dilation * LANES_V7X), F32)),
        compiler_params=_params(("arbitrary", "arbitrary")),
        name=f"attn_prompt_g{g}",
    )(q, k, k, v, v)


def _attn_sample_kernel(q_ref, k_ref, v_ref, c0_ref, c1_ref, c2_ref, o_ref):
    b = pl.program_id(0)
    head_row = lax.broadcasted_iota(jnp.int32, (HEADS_PER_GROUP, GROUP_W), 0)
    head_lane = lax.broadcasted_iota(jnp.int32, (HEADS_PER_GROUP, GROUP_W), 1) // HEAD_DIM
    own = head_row == head_lane
    parts = []
    for g, c_ref in enumerate((c0_ref, c1_ref, c2_ref)):
        window, dilation = GROUPS[g]
        bf = lambda t: t.astype(BF16).astype(F32)
        qmat = jnp.where(own, jnp.broadcast_to(q_ref[g, pl.ds(b, 1), :], (HEADS_PER_GROUP, GROUP_W)), 0.0)
        qmat = qmat.astype(BF16)
        knew = bf(k_ref[g, pl.ds(b, 1), :])
        vnew = bf(v_ref[g, pl.ds(b, 1), :])
        pos = lax.broadcasted_iota(jnp.int32, (HEADS_PER_GROUP, window), 1)
        s = _mm(qmat, c_ref[0:GROUP_W, :].astype(BF16))
        s = jnp.where((pos & (dilation - 1)) == 0, s, NEG_INF)
        s_new = jnp.sum(qmat.astype(F32) * knew, axis=-1, keepdims=True)
        mx = jnp.maximum(jnp.max(s, axis=-1, keepdims=True), s_new)
        p = jnp.exp2(s - mx)
        p_new = jnp.exp2(s_new - mx)
        den = jnp.sum(p, axis=-1, keepdims=True) + p_new
        numer = _mm_nt(p.astype(BF16), c_ref[GROUP_W:2 * GROUP_W, :].astype(BF16)) + bf(p_new) * vnew
        parts.append((jnp.where(own, numer, 0.0), mx, den))
    m_all = functools.reduce(jnp.maximum, [mx for _, mx, _ in parts])
    ws = [jnp.exp2(mx - m_all) for _, mx, _ in parts]
    num = sum(n * w for (n, _, _), w in zip(parts, ws))
    den = sum(d * w for (_, _, d), w in zip(parts, ws))
    o_ref[pl.ds(b, 1), :] = jnp.sum(num / den, axis=0, keepdims=True)


def _attn_sample(q, k, v, caches):
    nb = q.shape[1]
    specs = [pl.BlockSpec((None,) + c.shape[1:], lambda b: (b, 0, 0)) for c in caches]
    full = pl.BlockSpec((N_GROUPS, nb, GROUP_W), lambda b: (0, 0, 0))
    return pl.pallas_call(
        _attn_sample_kernel,
        grid=(nb,),
        in_specs=[full, full, full] + specs,
        out_specs=pl.BlockSpec((nb, GROUP_W), lambda b: (0, 0)),
        out_shape=jax.ShapeDtypeStruct((nb, GROUP_W), F32),
        compiler_params=_params(("arbitrary",)),
        name="attn_sample",
    )(q, k, v, *caches)


def _gelu_tanh(x):
    cdf = 0.5 * (1.0 + jnp.tanh(0.7978845608028654 * (x + 0.044715 * (x * x * x))))
    return x * cdf


def _softplus(x):
    return jnp.maximum(x, 0.0) + jnp.log1p(jnp.exp(-jnp.abs(x)))


LRU_CHUNK = 2 * RNN_BLOCK_W
N_LRU_CHUNKS = D_RNN // LRU_CHUNK


def _lru_lanes(n):
    return slice(n * LRU_CHUNK, (n + 1) * LRU_CHUNK)


def _lru_coeffs(xc, n, wrg_ref, brg_ref, wig_ref, big_ref, lam_ref):
    lanes = _lru_lanes(n)
    sigmoid = lambda v: 0.5 + 0.5 * jnp.tanh(0.5 * v)
    xcb = xc.astype(BF16)
    r = sigmoid(_mm(xcb, wrg_ref[n]) + brg_ref[:, lanes])
    i = sigmoid(_mm(xcb, wig_ref[n]) + big_ref[:, lanes])
    neg_rate = LRU_C * _softplus(-lam_ref[:, lanes])
    a = jnp.exp2(r * (-LOG2_E * neg_rate))
    t = jnp.tanh(r * neg_rate)
    y = 2.0 * t / (1.0 + t)
    root = jnp.where(y > 0.0, y * lax.rsqrt(y), 0.0)
    return a, root * i * xc


def _pair_blocks(w):
    z = jnp.zeros_like(w[0::2])
    top = jnp.concatenate([w[0::2], z], axis=2)
    bot = jnp.concatenate([z, w[1::2]], axis=2)
    return jnp.concatenate([top, bot], axis=1)


def _lru_tile(xb, gb, n, lru_refs, ctail_ref, hcar_ref):
    cw_ref, cb_ref, wrg_ref, brg_ref, wig_ref, big_ref, lam_ref = lru_refs
    lanes = _lru_lanes(n)
    sl = SUBLANES_V7X
    tm = xb.shape[0]
    nj = tm // sl
    sub = lax.broadcasted_iota(jnp.int32, (sl, LRU_CHUNK), 0)
    vrow = lambda t, j: t[j * sl:(j + 1) * sl]

    taps = CONV_W - 1
    wrap = [pltpu.roll(jnp.where(sub == sl - 1, vrow(ctail_ref[:, lanes], i), vrow(xb, nj - taps + i)), 1, 0)
            for i in range(taps)]
    ctail_ref[:, lanes] = xb[tm - taps * sl:]
    xc = cb_ref[:, lanes] + xb * cw_ref[taps:taps + 1, lanes]
    for k in range(1, CONV_W):
        shifted = jnp.concatenate(wrap[taps - k:] + [xb[:tm - k * sl]], axis=0)
        xc = xc + shifted * cw_ref[taps - k:taps - k + 1, lanes]

    a, b = _lru_coeffs(xc, n, wrg_ref, brg_ref, wig_ref, big_ref, lam_ref)
    gate = _gelu_tanh(gb)

    hl, acc = vrow(b, 0), vrow(a, 0)
    hls, accs = [hl], [acc]
    for j in range(1, nj):
        hl = vrow(a, j) * hl + vrow(b, j)
        acc = vrow(a, j) * acc
        hls.append(hl)
        accs.append(acc)

    for s in (1, 2, 4):
        keep = sub >= s
        acc_prev = jnp.where(keep, pltpu.roll(acc, s, 0), 1.0)
        hl_prev = jnp.where(keep, pltpu.roll(hl, s, 0), 0.0)
        hl = acc * hl_prev + hl
        acc = acc * acc_prev
    carry = hcar_ref[:, lanes]
    h_end = hl + acc * carry
    h_in = jnp.where(sub == 0, carry, pltpu.roll(h_end, 1, 0))
    hcar_ref[:, lanes] = h_end[sl - 1:sl]

    return [(hls[j] + accs[j] * h_in) * vrow(gate, j) for j in range(nj)]


def _rglru_sample_kernel(xb_ref, gb_ref, sc_ref, h0_ref, cw_ref, cb_ref, wrg_ref, brg_ref, wig_ref,
                         big_ref, lam_ref, hb_ref, conv_ref, h_ref):
    xb = xb_ref[...]
    taps = [sc_ref[:, j * D_RNN:(j + 1) * D_RNN] for j in range(CONV_W - 1)] + [xb]
    xc = cb_ref[...] + sum(t * cw_ref[j:j + 1, :] for j, t in enumerate(taps))
    for n in range(N_LRU_CHUNKS):
        lanes = _lru_lanes(n)
        a, b = _lru_coeffs(xc[:, lanes], n, wrg_ref, brg_ref, wig_ref, big_ref, lam_ref)
        h = a * h0_ref[:, lanes] + b
        h_ref[:, lanes] = h
        hb_ref[:, lanes] = (h * _gelu_tanh(gb_ref[:, lanes])).astype(hb_ref.dtype)
    for j in range(CONV_W - 1):
        conv_ref[:, j * D_RNN:(j + 1) * D_RNN] = taps[j + 1]


def _rglru_sample(xb, gb, state_conv, h0, cw, cb, wrg, brg, wig, big, lam):
    nb = xb.shape[0]
    args = (xb, gb, state_conv, h0, cw, cb, wrg, brg, wig, big, lam)
    return pl.pallas_call(
        _rglru_sample_kernel,
        grid=(1,),
        in_specs=[_const_spec(a.shape) for a in args],
        out_specs=(_const_spec((nb, D_RNN)), _const_spec((nb, (CONV_W - 1) * D_RNN)),
                   _const_spec((nb, D_RNN))),
        out_shape=(jax.ShapeDtypeStruct((nb, D_RNN), BF16),
                   jax.ShapeDtypeStruct((nb, (CONV_W - 1) * D_RNN), F32),
                   jax.ShapeDtypeStruct((nb, D_RNN), F32)),
        compiler_params=_params(("arbitrary",)),
        name="rglru_sample",
    )(*args)


def _post_kernel(*refs, dilations):
    n_parts = len(dilations)
    attn_refs = refs[:2 * n_parts] if n_parts > 1 else refs[:1]
    rest = refs[len(attn_refs):]
    (hb_ref, ga_ref, gb2_ref, x_ref, expand_ref, bg_ref, woa_ref, wor_ref, wout_ref, g2_ref,
     wfi_ref, wfo_ref, y_ref) = rest[:13]
    tm = x_ref.shape[0]

    if n_parts > 1:
        o_nat_ref, lse_nat_ref = rest[13:]

        def natural(src_ref, dst_ref, width, d):
            if d == 1:
                return src_ref[...].astype(F32)
            n_slabs = width // LANES_V7X
            for r in range(d):
                for c in range(n_slabs):
                    lanes = slice(r * width + c * LANES_V7X, r * width + (c + 1) * LANES_V7X)
                    dst_ref[c, pl.ds(r, tm // d, stride=d), :] = src_ref[:, lanes].astype(F32)
            return jnp.concatenate([dst_ref[c] for c in range(n_slabs)], axis=1)

        o_refs, lse_refs = attn_refs[:n_parts], attn_refs[n_parts:]
        lses = [natural(r, lse_nat_ref, LANES_V7X, d) for r, d in zip(lse_refs, dilations)]
        mx = functools.reduce(jnp.maximum, lses)
        es = [jnp.exp2(l - mx) for l in lses]
        inv = 1.0 / sum(es)
        attn = 0.0
        for o_ref, e, d in zip(o_refs, es, dilations):
            cw = e * inv
            hi = cw.astype(BF16)
            lo = (cw - hi.astype(F32)).astype(BF16)
            wide = _mm(jnp.concatenate([hi, lo], axis=1), expand_ref[...])
            attn = attn + wide * natural(o_ref, o_nat_ref, GROUP_W, d)
    else:
        attn = attn_refs[0][...]

    ya = _mm(attn.astype(BF16), woa_ref[...])
    yb = _mm(hb_ref[...], wor_ref[...])
    merged = (jax.nn.sigmoid(ga_ref[...] + bg_ref[0:1, :]) * ya
              + jax.nn.sigmoid(gb2_ref[...] + bg_ref[1:2, :]) * yb)
    x1 = x_ref[...] + _mm(merged.astype(BF16), wout_ref[...])
    hn2 = _rms_norm_rows(x1, g2_ref[...]).astype(BF16)
    gu = _mm(hn2, wfi_ref[...])
    act = jax.nn.silu(gu[:, :D_FF]) * gu[:, D_FF:]
    y_ref[...] = x1 + _mm(act.astype(BF16), wfo_ref[...])


def _post(attn_parts, dilations, hb, ga, gb2, x, expand, bg, woa, wor, wout, g2, wfi, wfo, *, tm):
    m = x.shape[0]
    row = lambda w: pl.BlockSpec((tm, w), lambda i: (i, 0))
    n_parts = len(attn_parts)
    scratch = []
    if n_parts > 1:
        blocked = lambda w, d: pl.BlockSpec((tm // d, d * w), lambda i: (i, 0))
        attn_args = [o for o, _ in attn_parts] + [l for _, l in attn_parts]
        attn_specs = ([blocked(GROUP_W, d) for d in dilations]
                      + [blocked(LANES_V7X, d) for d in dilations])
        scratch = [pltpu.VMEM((GROUP_W // LANES_V7X, tm, LANES_V7X), F32),
                   pltpu.VMEM((1, tm, LANES_V7X), F32)]
    else:
        attn_args, attn_specs = list(attn_parts), [row(GROUP_W)]
    consts = (expand, bg, woa, wor, wout, g2, wfi, wfo)
    return pl.pallas_call(
        functools.partial(_post_kernel, dilations=dilations),
        grid=(m // tm,),
        in_specs=attn_specs + [row(D_RNN), row(D_MODEL), row(D_MODEL), row(D_MODEL)]
        + [_const_spec(c.shape) for c in consts],
        out_specs=row(D_MODEL),
        out_shape=jax.ShapeDtypeStruct((m, D_MODEL), F32),
        scratch_shapes=scratch,
        compiler_params=_params(("arbitrary",)),
        name="post_prompt" if n_parts > 1 else "post_sample",
    )(*attn_args, hb, ga, gb2, x, *consts)


def kernel(x_prompt, x_sample, cache_kv_w128, cache_kv_w512, cache_kv_w2048, state_conv, state_h,
           norm1_g, w_in, b_gate, q_norm_g, k_norm_g, conv_w, conv_b, w_rg, b_rg, w_ig, b_ig,
           lru_lambda, w_o_attn, w_o_rnn, w_out, norm2_g, w_ffn_in, w_ffn_out):
    assert x_prompt.shape[0] == 1 and norm1_g.shape[0] == 1 and x_sample.shape[1] == 1
    seq = x_prompt.shape[1]
    nb = x_sample.shape[0]
    layer = 0

    expand = (jnp.arange(LANES_V7X)[:, None] == (jnp.arange(GROUP_W) // HEAD_DIM)[None, :]).astype(BF16)
    expand = jnp.concatenate([expand, expand], axis=0)

    row2 = lambda t: t[layer].reshape(1, -1)
    g1, g2 = row2(norm1_g), row2(norm2_g)
    qg = jnp.tile(q_norm_g[layer], HEADS_PER_GROUP).reshape(1, GROUP_W) * (ATTN_SCALE * LOG2_E)
    kg = jnp.tile(k_norm_g[layer], HEADS_PER_GROUP).reshape(1, GROUP_W)
    w_in_b = w_in[layer].astype(BF16)
    wrg, wig = _pair_blocks(w_rg[layer].astype(BF16)), _pair_blocks(w_ig[layer].astype(BF16))
    woa, wor, wout = (w[layer].astype(BF16) for w in (w_o_attn, w_o_rnn, w_out))
    wfi, wfo = w_ffn_in[layer].astype(BF16), w_ffn_out[layer].astype(BF16)
    lru = (conv_w[layer], row2(conv_b), wrg, row2(b_rg), wig, row2(b_ig), row2(lru_lambda))
    post_w = (expand, b_gate[layer], woa, wor, wout, g2, wfi, wfo)

    xp = x_prompt[0]
    *qkv, hb, conv_rows, h_last, ga, gb2, kvt = _inproj(
        xp, g1, w_in_b, qg, kg, lru, tm=256, tail=MAX_WINDOW, qkv_dtype=BF16, dilations=DILATIONS)
    parts = [_attn_prompt(*qkv[3 * g:3 * g + 3], g, d) for g, d in enumerate(DILATIONS)]
    y_p = _post(parts, DILATIONS, hb, ga, gb2, xp, *post_w, tm=256)
    conv_p = conv_rows[SUBLANES_V7X - 1::SUBLANES_V7X]

    xs = x_sample[:, 0]
    *qkvs, xbs, gbs, gas, gb2s, kvts = _inproj(xs, g1, w_in_b, qg, kg, tm=nb, tail=nb,
                                               qkv_dtype=F32, dilations=(1,) * N_GROUPS)
    qs, ks, vs = (jnp.stack(qkvs[j::3]) for j in range(3))
    caches = [jnp.transpose(c[layer], (0, 2, 3, 4, 1)).reshape(nb, 2 * GROUP_W, c.shape[2])
              for c in (cache_kv_w128, cache_kv_w512, cache_kv_w2048)]
    attn_s = _attn_sample(qs, ks, vs, caches)
    hbs, conv_s, h_s = _rglru_sample(xbs, gbs, state_conv[layer].reshape(nb, -1), state_h[layer], *lru)
    y_s = _post([attn_s], (1,), hbs, gas, gb2s, xs, *post_w, tm=nb)

    kv_shape = lambda rows: (1, 1, rows, 2, HEADS_PER_GROUP, HEAD_DIM)
    kv_prompt = [kvt[g, MAX_WINDOW - min(w, seq):].reshape(kv_shape(min(w, seq)))
                 for g, (w, _) in enumerate(GROUPS)]
    kv_sample = [kvts[g].reshape(1, nb, 1, 2, HEADS_PER_GROUP, HEAD_DIM) for g in range(N_GROUPS)]
    return (y_p[None], y_s[:, None],
            kv_prompt[0], kv_prompt[1], kv_prompt[2],
            conv_p[None, None], h_last[None],
            kv_sample[0], kv_sample[1], kv_sample[2],
            conv_s.reshape(1, nb, CONV_W - 1, D_RNN), h_s[None])
```

```python
import functools

import jax
import jax.numpy as jnp
from jax import lax
from jax.experimental import pallas as pl
from jax.experimental.pallas import tpu as pltpu

F32 = jnp.float32
BF16 = jnp.bfloat16

D_MODEL = 1024
HEAD_DIM = 64
HEADS_PER_GROUP = 8
GROUPS = ((128, 1), (512, 4), (2048, 16))
DILATIONS = tuple(d for _, d in GROUPS)
N_GROUPS = len(GROUPS)
GROUP_W = HEADS_PER_GROUP * HEAD_DIM
QKV_WIDTH = N_GROUPS * GROUP_W
BLOCK = 128
ATTN_SCALE = HEAD_DIM ** -0.5
LOG2_E = 1.4426950408889634
NEG_INF = -1e30
D_RNN = 1280
RNN_BLOCKS = 10
RNN_BLOCK_W = D_RNN // RNN_BLOCKS
CONV_W = 4
LRU_C = 8.0
D_FF = 2816
RMS_EPS = 1e-6
MAX_WINDOW = max(w for w, _ in GROUPS)

LANES_V7X = 128
SUBLANES_V7X = 8
MXU_DIM_V7X = 256
VMEM_LIMIT_BYTES = 56 * 1024 * 1024

OFF_Q, OFF_K, OFF_V = 0, QKV_WIDTH, 2 * QKV_WIDTH
OFF_XB = 3 * QKV_WIDTH
OFF_GB = OFF_XB + D_RNN
OFF_GA = OFF_GB + D_RNN
OFF_GB2 = OFF_GA + D_MODEL


def _mm(a, b):
    return jnp.dot(a, b, preferred_element_type=F32)


def _mm_nt(a, b):
    return lax.dot_general(a, b, (((1,), (1,)), ((), ())), preferred_element_type=F32)


def _rms_norm_rows(x, g):
    return x * lax.rsqrt(jnp.mean(x * x, axis=-1, keepdims=True) + RMS_EPS) * g


def _const_spec(shape):
    nd = len(shape)
    return pl.BlockSpec(shape, lambda *_: (0,) * nd, pipeline_mode=pl.Buffered(1))


def _params(sem):
    return pltpu.CompilerParams(dimension_semantics=sem, vmem_limit_bytes=VMEM_LIMIT_BYTES)


def _inproj_kernel(x_ref, g1_ref, w_ref, qg_ref, kg_ref, *refs, dilations, tail_first_step,
                   fuse_lru):
    if fuse_lru:
        lru_refs, refs = refs[:7], refs[7:]
    qkv_refs = refs[:3 * N_GROUPS]
    if fuse_lru:
        (hb_ref, conv_ref, hlast_ref, ga_ref, gb2_ref, kvt_ref,
         hn_ref, hbs_ref, hnp_ref, ctail_ref, hcar_ref) = refs[3 * N_GROUPS:]
    else:
        xb_ref, gbr_ref, ga_ref, gb2_ref, kvt_ref, hn_ref = refs[3 * N_GROUPS:]
    step = pl.program_id(0)
    tm = x_ref.shape[0]
    hn32 = _rms_norm_rows(x_ref[...], g1_ref[...])
    hn = hn32.astype(BF16)
    n_slabs = D_MODEL // LANES_V7X
    for c in range(n_slabs):
        hn_ref[c] = hn32[:, c * LANES_V7X:(c + 1) * LANES_V7X]

    def strided_rows(start, size, stride):
        return jnp.concatenate([hn_ref[c, pl.ds(start, size, stride=stride), :] for c in range(n_slabs)],
                               axis=1)

    if fuse_lru:
        @pl.when(step == 0)
        def _():
            ctail_ref[...] = jnp.zeros(ctail_ref.shape, F32)
            hcar_ref[...] = jnp.zeros(hcar_ref.shape, F32)

        nj = tm // SUBLANES_V7X
        pitch = nj + SUBLANES_V7X
        for s in range(SUBLANES_V7X):
            for c in range(n_slabs):
                hnp_ref[c, s * pitch:s * pitch + nj, :] = hn32[s * nj:(s + 1) * nj, c * LANES_V7X:(c + 1) * LANES_V7X]
        ht = jnp.concatenate(
            [jnp.concatenate([hnp_ref[c, pl.ds(j, SUBLANES_V7X, stride=pitch), :] for c in range(n_slabs)],
                             axis=1) for j in range(nj)], axis=0).astype(BF16)
        per_vreg = LRU_CHUNK // LANES_V7X
        for n in range(N_LRU_CHUNKS):
            c0 = n * LRU_CHUNK
            hb_rows = _lru_tile(_mm(ht, w_ref[:, OFF_XB + c0:OFF_XB + c0 + LRU_CHUNK]),
                                _mm(ht, w_ref[:, OFF_GB + c0:OFF_GB + c0 + LRU_CHUNK]),
                                n, lru_refs, ctail_ref, hcar_ref)
            for j, rows in enumerate(hb_rows):
                for c in range(per_vreg):
                    hbs_ref[n * per_vreg + c, pl.ds(j, SUBLANES_V7X, stride=pitch), :] = (
                        rows[:, c * LANES_V7X:(c + 1) * LANES_V7X])
        conv_ref[...] = ctail_ref[...]
        hlast_ref[...] = hcar_ref[...]
        hb_ref[...] = jnp.concatenate(
            [jnp.concatenate([hbs_ref[c, s * pitch:s * pitch + nj, :] for s in range(SUBLANES_V7X)], axis=0)
             for c in range(D_RNN // LANES_V7X)], axis=1).astype(hb_ref.dtype)
    else:
        xb_ref[...] = _mm(hn, w_ref[:, OFF_XB:OFF_XB + D_RNN])
        gbr_ref[...] = _mm(hn, w_ref[:, OFF_GB:OFF_GB + D_RNN])

    def head_norm(t, gain):
        first_head = lax.broadcasted_iota(jnp.int32, (1, LANES_V7X), 1) < HEAD_DIM
        cols = []
        for c in range(0, GROUP_W, LANES_V7X):
            x = t[:, c:c + LANES_V7X]
            xx = x * x
            s0 = jnp.sum(jnp.where(first_head, xx, 0.0), axis=-1, keepdims=True)
            s1 = jnp.sum(jnp.where(first_head, 0.0, xx), axis=-1, keepdims=True)
            ms = jnp.where(first_head, s0, s1) * (1.0 / HEAD_DIM)
            cols.append(x * lax.rsqrt(ms + RMS_EPS))
        return jnp.concatenate(cols, axis=1) * gain

    def qkv(h, g):
        c = g * GROUP_W
        qn = head_norm(_mm(h, w_ref[:, OFF_Q + c:OFF_Q + c + GROUP_W]), qg_ref[...])
        kn = head_norm(_mm(h, w_ref[:, OFF_K + c:OFF_K + c + GROUP_W]), kg_ref[...])
        vv = _mm(h, w_ref[:, OFF_V + c:OFF_V + c + GROUP_W])
        return qn, kn, vv

    in_order = {}
    for g, d in enumerate(dilations):
        rows = tm // d
        if d == 1:
            hg = hn
        else:
            hg = jnp.concatenate([strided_rows(r, rows, d) for r in range(d)], axis=0).astype(BF16)
        parts = qkv(hg, g)
        if d == 1:
            in_order[g] = parts
        for t, o_ref in zip(parts, qkv_refs[3 * g:3 * g + 3]):
            for r in range(d):
                o_ref[:, r * GROUP_W:(r + 1) * GROUP_W] = t[r * rows:(r + 1) * rows].astype(o_ref.dtype)

    ga_ref[...] = _mm(hn, w_ref[:, OFF_GA:OFF_GA + D_MODEL])
    gb2_ref[...] = _mm(hn, w_ref[:, OFF_GB2:OFF_GB2 + D_MODEL])

    @pl.when(step >= tail_first_step)
    def _():
        for g in range(N_GROUPS):
            _, kn, vv = in_order[g] if g in in_order else qkv(hn, g)
            kvt_ref[g, :, 0:GROUP_W] = kn
            kvt_ref[g, :, GROUP_W:2 * GROUP_W] = vv


def _inproj(x, g1, w_in, qg, kg, lru=None, *, tm, tail, qkv_dtype, dilations):
    m = x.shape[0]
    nt = m // tm
    tail_first_step = nt - tail // tm
    row = lambda w: pl.BlockSpec((tm, w), lambda i: (i, 0))
    fixed = lambda r, w: pl.BlockSpec((r, w), lambda i: (0, 0))
    qkv_specs, qkv_shapes = [], []
    for d in dilations:
        qkv_specs += [pl.BlockSpec((tm // d, d * GROUP_W), lambda i: (i, 0))] * 3
        qkv_shapes += [jax.ShapeDtypeStruct((m // d, d * GROUP_W), qkv_dtype)] * 3
    kvt_spec = pl.BlockSpec((N_GROUPS, tm, 2 * GROUP_W),
                            lambda i: (0, jnp.maximum(i - tail_first_step, 0), 0))
    scratch = [pltpu.VMEM((D_MODEL // LANES_V7X, tm, LANES_V7X), F32)]
    tail_rows = (CONV_W - 1) * SUBLANES_V7X
    if lru is None:
        lru = ()
        rnn_specs = (row(D_RNN), row(D_RNN))
        rnn_shapes = (jax.ShapeDtypeStruct((m, D_RNN), F32),) * 2
    else:
        rnn_specs = (row(D_RNN), fixed(tail_rows, D_RNN), fixed(1, D_RNN))
        rnn_shapes = (jax.ShapeDtypeStruct((m, D_RNN), BF16),
                      jax.ShapeDtypeStruct((tail_rows, D_RNN), F32),
                      jax.ShapeDtypeStruct((1, D_RNN), F32))
        padded = tm + SUBLANES_V7X * SUBLANES_V7X
        scratch += [pltpu.VMEM((D_RNN // LANES_V7X, padded, LANES_V7X), F32),
                    pltpu.VMEM((D_MODEL // LANES_V7X, padded, LANES_V7X), F32),
                    pltpu.VMEM((tail_rows, D_RNN), F32),
                    pltpu.VMEM((1, D_RNN), F32)]
    out_shape = tuple(qkv_shapes) + rnn_shapes + (
        jax.ShapeDtypeStruct((m, D_MODEL), F32),
        jax.ShapeDtypeStruct((m, D_MODEL), F32),
        jax.ShapeDtypeStruct((N_GROUPS, tail, 2 * GROUP_W), F32),
    )
    consts = (g1, w_in, qg, kg) + tuple(lru)
    return pl.pallas_call(
        functools.partial(_inproj_kernel, dilations=dilations, tail_first_step=tail_first_step,
                          fuse_lru=bool(lru)),
        grid=(nt,),
        in_specs=[row(D_MODEL)] + [_const_spec(c.shape) for c in consts],
        out_specs=tuple(qkv_specs) + rnn_specs + (row(D_MODEL), row(D_MODEL), kvt_spec),
        out_shape=out_shape,
        scratch_shapes=scratch,
        compiler_params=_params(("arbitrary",)),
        name="inproj_lru" if lru else "inproj",
    )(x, *consts)


ATTN_BLOCKS_PER_STEP = 8


def _attn_prompt_kernel(q_ref, kc_ref, kp_ref, vc_ref, vp_ref, *refs, sample_first_row):
    if sample_first_row is None:
        o_ref, lse_ref = refs
    else:
        qs_ref, ks_ref, vs_ref, c0_ref, c1_ref, c2_ref, o_ref, lse_ref, os_ref = refs
        flat_step = pl.program_id(0) * pl.num_programs(1) + pl.program_id(1)
        _attn_sample_row(flat_step, sample_first_row + flat_step, qs_ref, ks_ref, vs_ref,
                         (c0_ref, c1_ref, c2_ref), os_ref)
    step = pl.program_id(1)
    qi = lax.broadcasted_iota(jnp.int32, (BLOCK, 2 * BLOCK), 0)
    kj = lax.broadcasted_iota(jnp.int32, (BLOCK, 2 * BLOCK), 1)
    dist = BLOCK + qi - kj
    band = (dist >= 0) & (dist <= BLOCK)
    first_band = band & ((step > 0) | (kj >= BLOCK))
    band2 = jnp.concatenate([band, band], axis=0)
    first_band2 = jnp.concatenate([first_band, first_band], axis=0)
    first_head = lax.broadcasted_iota(jnp.int32, (1, LANES_V7X), 1) < HEAD_DIM
    stat_lane = lax.broadcasted_iota(jnp.int32, (BLOCK, LANES_V7X), 1)
    zero = jnp.zeros((), BF16)

    for blk in range(q_ref.shape[0] // BLOCK):
        rows = slice(blk * BLOCK, (blk + 1) * BLOCK)
        prev_rows = slice((blk - 1) * BLOCK, blk * BLOCK)
        valid = band2 if blk else first_band2
        lse_tile = jnp.zeros((BLOCK, LANES_V7X), F32)
        for pair in range(HEADS_PER_GROUP // 2):
            cols = slice(pair * LANES_V7X, (pair + 1) * LANES_V7X)
            qp = q_ref[rows, cols]
            k_prev = kc_ref[prev_rows, cols] if blk else kp_ref[:, cols]
            v_prev = vc_ref[prev_rows, cols] if blk else vp_ref[:, cols]
            kk = jnp.concatenate([k_prev, kc_ref[rows, cols]], axis=0)
            vv = jnp.concatenate([v_prev, vc_ref[rows, cols]], axis=0)
            q2 = jnp.concatenate([jnp.where(first_head, qp, zero), jnp.where(first_head, zero, qp)], axis=0)
            s = jnp.where(valid, _mm_nt(q2, kk), NEG_INF)
            mx = jnp.max(s, axis=-1, keepdims=True)
            p = jnp.exp2(s - mx)
            den = jnp.sum(p, axis=-1, keepdims=True)
            pv = _mm(p.astype(BF16), vv) * (1.0 / den)
            o_ref[rows, cols] = jnp.where(first_head, pv[:BLOCK], pv[BLOCK:]).astype(o_ref.dtype)
            lse = mx + jnp.log2(den)
            lse_tile = jnp.where(stat_lane == 2 * pair, lse[:BLOCK],
                                 jnp.where(stat_lane == 2 * pair + 1, lse[BLOCK:], lse_tile))
        lse_ref[rows, :] = lse_tile


def _attn_prompt(q, k, v, g, dilation, sample=None):
    m_len = q.shape[0]
    nbs = ATTN_BLOCKS_PER_STEP
    rows = nbs * BLOCK
    n_inner = m_len // rows
    cur = pl.BlockSpec((rows, GROUP_W), lambda r, b: (b, r))
    prev = pl.BlockSpec((BLOCK, GROUP_W), lambda r, b: (jnp.maximum(b * nbs - 1, 0), r))
    in_specs, args = [cur, cur, prev, cur, prev], [q, k, k, v, v]
    out_specs = [pl.BlockSpec((rows, GROUP_W), lambda r, b: (b, r)),
                 pl.BlockSpec((rows, LANES_V7X), lambda r, b: (b, r))]
    out_shape = [jax.ShapeDtypeStruct((m_len, dilation * GROUP_W), BF16),
                 jax.ShapeDtypeStruct((m_len, dilation * LANES_V7X), F32)]
    first_row = None
    if sample is not None:
        qs, ks, vs, caches, first_row = sample
        steps = dilation * n_inner
        full = pl.BlockSpec(qs.shape, lambda r, b: (0, 0, 0))
        in_specs += [full, full, full] + [
            pl.BlockSpec((None,) + c.shape[1:], lambda r, b: (first_row + r * n_inner + b, 0, 0))
            for c in caches]
        args += [qs, ks, vs, *caches]
        out_specs.append(pl.BlockSpec((steps, GROUP_W), lambda r, b: (0, 0)))
        out_shape.append(jax.ShapeDtypeStruct((steps, GROUP_W), F32))
    return pl.pallas_call(
        functools.partial(_attn_prompt_kernel, sample_first_row=first_row),
        grid=(dilation, n_inner),
        in_specs=in_specs,
        out_specs=tuple(out_specs),
        out_shape=tuple(out_shape),
        compiler_params=_params(("arbitrary", "arbitrary")),
        name=f"attn_prompt_g{g}",
    )(*args)


def _attn_sample_row(out_row, b, q_ref, k_ref, v_ref, cache_refs, o_ref):
    head_row = lax.broadcasted_iota(jnp.int32, (HEADS_PER_GROUP, GROUP_W), 0)
    head_lane = lax.broadcasted_iota(jnp.int32, (HEADS_PER_GROUP, GROUP_W), 1) // HEAD_DIM
    own = head_row == head_lane
    parts = []
    for g, c_ref in enumerate(cache_refs):
        window, dilation = GROUPS[g]
        bf = lambda t: t.astype(BF16).astype(F32)
        qmat = jnp.where(own, jnp.broadcast_to(q_ref[g, pl.ds(b, 1), :], (HEADS_PER_GROUP, GROUP_W)), 0.0)
        qmat = qmat.astype(BF16)
        knew = bf(k_ref[g, pl.ds(b, 1), :])
        vnew = bf(v_ref[g, pl.ds(b, 1), :])
        pos = lax.broadcasted_iota(jnp.int32, (HEADS_PER_GROUP, window), 1)
        s = _mm(qmat, c_ref[0:GROUP_W, :].astype(BF16))
        s = jnp.where((pos & (dilation - 1)) == 0, s, NEG_INF)
        s_new = jnp.sum(qmat.astype(F32) * knew, axis=-1, keepdims=True)
        mx = jnp.maximum(jnp.max(s, axis=-1, keepdims=True), s_new)
        p = jnp.exp2(s - mx)
        p_new = jnp.exp2(s_new - mx)
        den = jnp.sum(p, axis=-1, keepdims=True) + p_new
        numer = _mm_nt(p.astype(BF16), c_ref[GROUP_W:2 * GROUP_W, :].astype(BF16)) + bf(p_new) * vnew
        parts.append((jnp.where(own, numer, 0.0), mx, den))
    m_all = functools.reduce(jnp.maximum, [mx for _, mx, _ in parts])
    ws = [jnp.exp2(mx - m_all) for _, mx, _ in parts]
    num = sum(n * w for (n, _, _), w in zip(parts, ws))
    den = sum(d * w for (_, _, d), w in zip(parts, ws))
    o_ref[pl.ds(out_row, 1), :] = jnp.sum(num / den, axis=0, keepdims=True)


def _gelu_tanh(x):
    cdf = 0.5 * (1.0 + jnp.tanh(0.7978845608028654 * (x + 0.044715 * (x * x * x))))
    return x * cdf


def _softplus(x):
    return jnp.maximum(x, 0.0) + jnp.log1p(jnp.exp(-jnp.abs(x)))


LRU_CHUNK = 2 * RNN_BLOCK_W
N_LRU_CHUNKS = D_RNN // LRU_CHUNK


def _lru_lanes(n):
    return slice(n * LRU_CHUNK, (n + 1) * LRU_CHUNK)


def _lru_coeffs(xc, n, wrg_ref, brg_ref, wig_ref, big_ref, lam_ref):
    lanes = _lru_lanes(n)
    sigmoid = lambda v: 0.5 + 0.5 * jnp.tanh(0.5 * v)
    xcb = xc.astype(BF16)
    r = sigmoid(_mm(xcb, wrg_ref[n]) + brg_ref[:, lanes])
    i = sigmoid(_mm(xcb, wig_ref[n]) + big_ref[:, lanes])
    neg_rate = LRU_C * _softplus(-lam_ref[:, lanes])
    a = jnp.exp2(r * (-LOG2_E * neg_rate))
    t = jnp.tanh(r * neg_rate)
    y = 2.0 * t / (1.0 + t)
    root = jnp.where(y > 0.0, y * lax.rsqrt(y), 0.0)
    return a, root * i * xc


def _pair_blocks(w):
    z = jnp.zeros_like(w[0::2])
    top = jnp.concatenate([w[0::2], z], axis=2)
    bot = jnp.concatenate([z, w[1::2]], axis=2)
    return jnp.concatenate([top, bot], axis=1)


def _lru_tile(xb, gb, n, lru_refs, ctail_ref, hcar_ref):
    cw_ref, cb_ref, wrg_ref, brg_ref, wig_ref, big_ref, lam_ref = lru_refs
    lanes = _lru_lanes(n)
    sl = SUBLANES_V7X
    tm = xb.shape[0]
    nj = tm // sl
    sub = lax.broadcasted_iota(jnp.int32, (sl, LRU_CHUNK), 0)
    vrow = lambda t, j: t[j * sl:(j + 1) * sl]

    taps = CONV_W - 1
    wrap = [pltpu.roll(jnp.where(sub == sl - 1, vrow(ctail_ref[:, lanes], i), vrow(xb, nj - taps + i)), 1, 0)
            for i in range(taps)]
    ctail_ref[:, lanes] = xb[tm - taps * sl:]
    xc = cb_ref[:, lanes] + xb * cw_ref[taps:taps + 1, lanes]
    for k in range(1, CONV_W):
        shifted = jnp.concatenate(wrap[taps - k:] + [xb[:tm - k * sl]], axis=0)
        xc = xc + shifted * cw_ref[taps - k:taps - k + 1, lanes]

    a, b = _lru_coeffs(xc, n, wrg_ref, brg_ref, wig_ref, big_ref, lam_ref)
    gate = _gelu_tanh(gb)

    hl, acc = vrow(b, 0), vrow(a, 0)
    hls, accs = [hl], [acc]
    for j in range(1, nj):
        hl = vrow(a, j) * hl + vrow(b, j)
        acc = vrow(a, j) * acc
        hls.append(hl)
        accs.append(acc)

    for s in (1, 2, 4):
        keep = sub >= s
        acc_prev = jnp.where(keep, pltpu.roll(acc, s, 0), 1.0)
        hl_prev = jnp.where(keep, pltpu.roll(hl, s, 0), 0.0)
        hl = acc * hl_prev + hl
        acc = acc * acc_prev
    carry = hcar_ref[:, lanes]
    h_end = hl + acc * carry
    h_in = jnp.where(sub == 0, carry, pltpu.roll(h_end, 1, 0))
    hcar_ref[:, lanes] = h_end[sl - 1:sl]

    return [(hls[j] + accs[j] * h_in) * vrow(gate, j) for j in range(nj)]


def _rglru_sample_kernel(xb_ref, gb_ref, sc_ref, h0_ref, cw_ref, cb_ref, wrg_ref, brg_ref, wig_ref,
                         big_ref, lam_ref, hb_ref, conv_ref, h_ref):
    xb = xb_ref[...]
    taps = [sc_ref[:, j * D_RNN:(j + 1) * D_RNN] for j in range(CONV_W - 1)] + [xb]
    xc = cb_ref[...] + sum(t * cw_ref[j:j + 1, :] for j, t in enumerate(taps))
    for n in range(N_LRU_CHUNKS):
        lanes = _lru_lanes(n)
        a, b = _lru_coeffs(xc[:, lanes], n, wrg_ref, brg_ref, wig_ref, big_ref, lam_ref)
        h = a * h0_ref[:, lanes] + b
        h_ref[:, lanes] = h
        hb_ref[:, lanes] = (h * _gelu_tanh(gb_ref[:, lanes])).astype(hb_ref.dtype)
    for j in range(CONV_W - 1):
        conv_ref[:, j * D_RNN:(j + 1) * D_RNN] = taps[j + 1]


def _rglru_sample(xb, gb, state_conv, h0, cw, cb, wrg, brg, wig, big, lam):
    nb = xb.shape[0]
    args = (xb, gb, state_conv, h0, cw, cb, wrg, brg, wig, big, lam)
    return pl.pallas_call(
        _rglru_sample_kernel,
        grid=(1,),
        in_specs=[_const_spec(a.shape) for a in args],
        out_specs=(_const_spec((nb, D_RNN)), _const_spec((nb, (CONV_W - 1) * D_RNN)),
                   _const_spec((nb, D_RNN))),
        out_shape=(jax.ShapeDtypeStruct((nb, D_RNN), BF16),
                   jax.ShapeDtypeStruct((nb, (CONV_W - 1) * D_RNN), F32),
                   jax.ShapeDtypeStruct((nb, D_RNN), F32)),
        compiler_params=_params(("arbitrary",)),
        name="rglru_sample",
    )(*args)


def _post_kernel(*refs, dilations):
    n_parts = len(dilations)
    attn_refs = refs[:2 * n_parts] if n_parts > 1 else refs[:1]
    rest = refs[len(attn_refs):]
    (hb_ref, ga_ref, gb2_ref, x_ref, expand_ref, bg_ref, woa_ref, wor_ref, wout_ref, g2_ref,
     wfi_ref, wfo_ref, y_ref) = rest[:13]
    tm = x_ref.shape[0]

    if n_parts > 1:
        o_nat_ref, lse_nat_ref = rest[13:]

        def natural(src_ref, dst_ref, width, d):
            if d == 1:
                return src_ref[...].astype(F32)
            n_slabs = width // LANES_V7X
            for r in range(d):
                for c in range(n_slabs):
                    lanes = slice(r * width + c * LANES_V7X, r * width + (c + 1) * LANES_V7X)
                    dst_ref[c, pl.ds(r, tm // d, stride=d), :] = src_ref[:, lanes].astype(F32)
            return jnp.concatenate([dst_ref[c] for c in range(n_slabs)], axis=1)

        o_refs, lse_refs = attn_refs[:n_parts], attn_refs[n_parts:]
        lses = [natural(r, lse_nat_ref, LANES_V7X, d) for r, d in zip(lse_refs, dilations)]
        mx = functools.reduce(jnp.maximum, lses)
        es = [jnp.exp2(l - mx) for l in lses]
        inv = 1.0 / sum(es)
        attn = 0.0
        for o_ref, e, d in zip(o_refs, es, dilations):
            cw = e * inv
            hi = cw.astype(BF16)
            lo = (cw - hi.astype(F32)).astype(BF16)
            wide = _mm(jnp.concatenate([hi, lo], axis=1), expand_ref[...])
            attn = attn + wide * natural(o_ref, o_nat_ref, GROUP_W, d)
    else:
        attn = attn_refs[0][...]

    ya = _mm(attn.astype(BF16), woa_ref[...])
    yb = _mm(hb_ref[...], wor_ref[...])
    merged = (jax.nn.sigmoid(ga_ref[...] + bg_ref[0:1, :]) * ya
              + jax.nn.sigmoid(gb2_ref[...] + bg_ref[1:2, :]) * yb)
    x1 = x_ref[...] + _mm(merged.astype(BF16), wout_ref[...])
    hn2 = _rms_norm_rows(x1, g2_ref[...]).astype(BF16)
    gu = _mm(hn2, wfi_ref[...])
    act = jax.nn.silu(gu[:, :D_FF]) * gu[:, D_FF:]
    y_ref[...] = x1 + _mm(act.astype(BF16), wfo_ref[...])


def _post(attn_parts, dilations, hb, ga, gb2, x, expand, bg, woa, wor, wout, g2, wfi, wfo, *, tm):
    m = x.shape[0]
    row = lambda w: pl.BlockSpec((tm, w), lambda i: (i, 0))
    n_parts = len(attn_parts)
    scratch = []
    if n_parts > 1:
        blocked = lambda w, d: pl.BlockSpec((tm // d, d * w), lambda i: (i, 0))
        attn_args = [o for o, _ in attn_parts] + [l for _, l in attn_parts]
        attn_specs = ([blocked(GROUP_W, d) for d in dilations]
                      + [blocked(LANES_V7X, d) for d in dilations])
        scratch = [pltpu.VMEM((GROUP_W // LANES_V7X, tm, LANES_V7X), F32),
                   pltpu.VMEM((1, tm, LANES_V7X), F32)]
    else:
        attn_args, attn_specs = list(attn_parts), [row(GROUP_W)]
    consts = (expand, bg, woa, wor, wout, g2, wfi, wfo)
    return pl.pallas_call(
        functools.partial(_post_kernel, dilations=dilations),
        grid=(m // tm,),
        in_specs=attn_specs + [row(D_RNN), row(D_MODEL), row(D_MODEL), row(D_MODEL)]
        + [_const_spec(c.shape) for c in consts],
        out_specs=row(D_MODEL),
        out_shape=jax.ShapeDtypeStruct((m, D_MODEL), F32),
        scratch_shapes=scratch,
        compiler_params=_params(("arbitrary",)),
        name="post_prompt" if n_parts > 1 else "post_sample",
    )(*attn_args, hb, ga, gb2, x, *consts)


def kernel(x_prompt, x_sample, cache_kv_w128, cache_kv_w512, cache_kv_w2048, state_conv, state_h,
           norm1_g, w_in, b_gate, q_norm_g, k_norm_g, conv_w, conv_b, w_rg, b_rg, w_ig, b_ig,
           lru_lambda, w_o_attn, w_o_rnn, w_out, norm2_g, w_ffn_in, w_ffn_out):
    assert x_prompt.shape[0] == 1 and norm1_g.shape[0] == 1 and x_sample.shape[1] == 1
    seq = x_prompt.shape[1]
    nb = x_sample.shape[0]
    layer = 0

    expand = (jnp.arange(LANES_V7X)[:, None] == (jnp.arange(GROUP_W) // HEAD_DIM)[None, :]).astype(BF16)
    expand = jnp.concatenate([expand, expand], axis=0)

    row2 = lambda t: t[layer].reshape(1, -1)
    g1, g2 = row2(norm1_g), row2(norm2_g)
    qg = jnp.tile(q_norm_g[layer], HEADS_PER_GROUP).reshape(1, GROUP_W) * (ATTN_SCALE * LOG2_E)
    kg = jnp.tile(k_norm_g[layer], HEADS_PER_GROUP).reshape(1, GROUP_W)
    w_in_b = w_in[layer].astype(BF16)
    wrg, wig = _pair_blocks(w_rg[layer].astype(BF16)), _pair_blocks(w_ig[layer].astype(BF16))
    woa, wor, wout = (w[layer].astype(BF16) for w in (w_o_attn, w_o_rnn, w_out))
    wfi, wfo = w_ffn_in[layer].astype(BF16), w_ffn_out[layer].astype(BF16)
    lru = (conv_w[layer], row2(conv_b), wrg, row2(b_rg), wig, row2(b_ig), row2(lru_lambda))
    post_w = (expand, b_gate[layer], woa, wor, wout, g2, wfi, wfo)

    xs = x_sample[:, 0]
    *qkvs, xbs, gbs, gas, gb2s, kvts = _inproj(xs, g1, w_in_b, qg, kg, tm=nb, tail=nb,
                                               qkv_dtype=F32, dilations=(1,) * N_GROUPS)
    qs, ks, vs = (jnp.stack(qkvs[j::3]) for j in range(3))
    caches = [jnp.transpose(c[layer], (0, 2, 3, 4, 1)).reshape(nb, 2 * GROUP_W, c.shape[2])
              for c in (cache_kv_w128, cache_kv_w512, cache_kv_w2048)]

    xp = x_prompt[0]
    *qkv, hb, conv_rows, h_last, ga, gb2, kvt = _inproj(
        xp, g1, w_in_b, qg, kg, lru, tm=256, tail=MAX_WINDOW, qkv_dtype=BF16, dilations=DILATIONS)
    half = nb // 2
    parts, attn_s = [], []
    for g, d in enumerate(DILATIONS):
        sample = (qs, ks, vs, caches, (g - 1) * half) if g else None
        *part, = _attn_prompt(*qkv[3 * g:3 * g + 3], g, d, sample)
        parts.append(part[:2])
        attn_s += part[2:]
    attn_s = jnp.concatenate(attn_s, axis=0)
    assert attn_s.shape == (nb, GROUP_W)
    y_p = _post(parts, DILATIONS, hb, ga, gb2, xp, *post_w, tm=256)
    conv_p = conv_rows[SUBLANES_V7X - 1::SUBLANES_V7X]

    hbs, conv_s, h_s = _rglru_sample(xbs, gbs, state_conv[layer].reshape(nb, -1), state_h[layer], *lru)
    y_s = _post([attn_s], (1,), hbs, gas, gb2s, xs, *post_w, tm=nb)

    kv_shape = lambda rows: (1, 1, rows, 2, HEADS_PER_GROUP, HEAD_DIM)
    kv_prompt = [kvt[g, MAX_WINDOW - min(w, seq):].reshape(kv_shape(min(w, seq)))
                 for g, (w, _) in enumerate(GROUPS)]
    kv_sample = [kvts[g].reshape(1, nb, 1, 2, HEADS_PER_GROUP, HEAD_DIM) for g in range(N_GROUPS)]
    return (y_p[None], y_s[:, None],
            kv_prompt[0], kv_prompt[1], kv_prompt[2],
            conv_p[None, None], h_last[None],
            kv_sample[0], kv_sample[1], kv_sample[2],
            conv_s.reshape(1, nb, CONV_W - 1, D_RNN), h_s[None])
```

```python
import functools

import jax
import jax.numpy as jnp
from jax import lax
from jax.experimental import pallas as pl
from jax.experimental.pallas import tpu as pltpu

F32 = jnp.float32
BF16 = jnp.bfloat16

D_MODEL = 1024
HEAD_DIM = 64
HEADS_PER_GROUP = 8
GROUPS = ((128, 1), (512, 4), (2048, 16))
DILATIONS = tuple(d for _, d in GROUPS)
N_GROUPS = len(GROUPS)
GROUP_W = HEADS_PER_GROUP * HEAD_DIM
QKV_WIDTH = N_GROUPS * GROUP_W
BLOCK = 128
ATTN_SCALE = HEAD_DIM ** -0.5
LOG2_E = 1.4426950408889634
NEG_INF = -1e30
D_RNN = 1280
RNN_BLOCKS = 10
RNN_BLOCK_W = D_RNN // RNN_BLOCKS
CONV_W = 4
LRU_C = 8.0
D_FF = 2816
RMS_EPS = 1e-6
MAX_WINDOW = max(w for w, _ in GROUPS)

LANES_V7X = 128
SUBLANES_V7X = 8
MXU_DIM_V7X = 256
VMEM_LIMIT_BYTES = 56 * 1024 * 1024

OFF_Q, OFF_K, OFF_V = 0, QKV_WIDTH, 2 * QKV_WIDTH
OFF_XB = 3 * QKV_WIDTH
OFF_GB = OFF_XB + D_RNN
OFF_GA = OFF_GB + D_RNN
OFF_GB2 = OFF_GA + D_MODEL


def _mm(a, b):
    return jnp.dot(a, b, preferred_element_type=F32)


def _mm_nt(a, b):
    return lax.dot_general(a, b, (((1,), (1,)), ((), ())), preferred_element_type=F32)


def _rms_norm_rows(x, g):
    return x * lax.rsqrt(jnp.mean(x * x, axis=-1, keepdims=True) + RMS_EPS) * g


def _const_spec(shape):
    nd = len(shape)
    return pl.BlockSpec(shape, lambda *_: (0,) * nd, pipeline_mode=pl.Buffered(1))


def _params(sem):
    return pltpu.CompilerParams(dimension_semantics=sem, vmem_limit_bytes=VMEM_LIMIT_BYTES)


def _inproj_kernel(x_ref, g1_ref, w_ref, qg_ref, kg_ref, *refs, dilations, tail_first_step,
                   fuse_lru):
    if fuse_lru:
        lru_refs, refs = refs[:7], refs[7:]
    qkv_refs = refs[:3 * N_GROUPS]
    if fuse_lru:
        (hb_ref, conv_ref, hlast_ref, ga_ref, gb2_ref, kvt_ref,
         hn_ref, hbs_ref, hnp_ref, ctail_ref, hcar_ref) = refs[3 * N_GROUPS:]
    else:
        xb_ref, gbr_ref, ga_ref, gb2_ref, kvt_ref, hn_ref = refs[3 * N_GROUPS:]
    step = pl.program_id(0)
    tm = x_ref.shape[0]
    hn32 = _rms_norm_rows(x_ref[...], g1_ref[...])
    hn = hn32.astype(BF16)
    n_slabs = D_MODEL // LANES_V7X
    for c in range(n_slabs):
        hn_ref[c] = hn32[:, c * LANES_V7X:(c + 1) * LANES_V7X]

    def strided_rows(start, size, stride):
        return jnp.concatenate([hn_ref[c, pl.ds(start, size, stride=stride), :] for c in range(n_slabs)],
                               axis=1)

    if fuse_lru:
        @pl.when(step == 0)
        def _():
            ctail_ref[...] = jnp.zeros(ctail_ref.shape, F32)
            hcar_ref[...] = jnp.zeros(hcar_ref.shape, F32)

        nj = tm // SUBLANES_V7X
        pitch = nj + SUBLANES_V7X
        for s in range(SUBLANES_V7X):
            for c in range(n_slabs):
                hnp_ref[c, s * pitch:s * pitch + nj, :] = hn32[s * nj:(s + 1) * nj, c * LANES_V7X:(c + 1) * LANES_V7X]
        ht = jnp.concatenate(
            [jnp.concatenate([hnp_ref[c, pl.ds(j, SUBLANES_V7X, stride=pitch), :] for c in range(n_slabs)],
                             axis=1) for j in range(nj)], axis=0).astype(BF16)
        per_vreg = LRU_CHUNK // LANES_V7X
        for n in range(N_LRU_CHUNKS):
            c0 = n * LRU_CHUNK
            hb_rows = _lru_tile(_mm(ht, w_ref[:, OFF_XB + c0:OFF_XB + c0 + LRU_CHUNK]),
                                _mm(ht, w_ref[:, OFF_GB + c0:OFF_GB + c0 + LRU_CHUNK]),
                                n, lru_refs, ctail_ref, hcar_ref)
            for j, rows in enumerate(hb_rows):
                for c in range(per_vreg):
                    hbs_ref[n * per_vreg + c, pl.ds(j, SUBLANES_V7X, stride=pitch), :] = (
                        rows[:, c * LANES_V7X:(c + 1) * LANES_V7X])
        conv_ref[...] = ctail_ref[...]
        hlast_ref[...] = hcar_ref[...]
        hb_ref[...] = jnp.concatenate(
            [jnp.concatenate([hbs_ref[c, s * pitch:s * pitch + nj, :] for s in range(SUBLANES_V7X)], axis=0)
             for c in range(D_RNN // LANES_V7X)], axis=1).astype(hb_ref.dtype)
    else:
        xb_ref[...] = _mm(hn, w_ref[:, OFF_XB:OFF_XB + D_RNN])
        gbr_ref[...] = _mm(hn, w_ref[:, OFF_GB:OFF_GB + D_RNN])

    def head_norm(t, gain):
        first_head = lax.broadcasted_iota(jnp.int32, (1, LANES_V7X), 1) < HEAD_DIM
        cols = []
        for c in range(0, GROUP_W, LANES_V7X):
            x = t[:, c:c + LANES_V7X]
            xx = x * x
            s0 = jnp.sum(jnp.where(first_head, xx, 0.0), axis=-1, keepdims=True)
            s1 = jnp.sum(jnp.where(first_head, 0.0, xx), axis=-1, keepdims=True)
            ms = jnp.where(first_head, s0, s1) * (1.0 / HEAD_DIM)
            cols.append(x * lax.rsqrt(ms + RMS_EPS))
        return jnp.concatenate(cols, axis=1) * gain

    def qkv(h, g):
        c = g * GROUP_W
        qn = head_norm(_mm(h, w_ref[:, OFF_Q + c:OFF_Q + c + GROUP_W]), qg_ref[...])
        kn = head_norm(_mm(h, w_ref[:, OFF_K + c:OFF_K + c + GROUP_W]), kg_ref[...])
        vv = _mm(h, w_ref[:, OFF_V + c:OFF_V + c + GROUP_W])
        return qn, kn, vv

    in_order = {}
    for g, d in enumerate(dilations):
        rows = tm // d
        if d == 1:
            hg = hn
        else:
            hg = jnp.concatenate([strided_rows(r, rows, d) for r in range(d)], axis=0).astype(BF16)
        parts = qkv(hg, g)
        if d == 1:
            in_order[g] = parts
        for t, o_ref in zip(parts, qkv_refs[3 * g:3 * g + 3]):
            for r in range(d):
                o_ref[:, r * GROUP_W:(r + 1) * GROUP_W] = t[r * rows:(r + 1) * rows].astype(o_ref.dtype)

    ga_ref[...] = _mm(hn, w_ref[:, OFF_GA:OFF_GA + D_MODEL])
    gb2_ref[...] = _mm(hn, w_ref[:, OFF_GB2:OFF_GB2 + D_MODEL])

    @pl.when(step >= tail_first_step)
    def _():
        for g in range(N_GROUPS):
            _, kn, vv = in_order[g] if g in in_order else qkv(hn, g)
            kvt_ref[g, :, 0:GROUP_W] = kn
            kvt_ref[g, :, GROUP_W:2 * GROUP_W] = vv


def _inproj(x, g1, w_in, qg, kg, lru=None, *, tm, tail, qkv_dtype, dilations):
    m = x.shape[0]
    nt = m // tm
    tail_first_step = nt - tail // tm
    row = lambda w: pl.BlockSpec((tm, w), lambda i: (i, 0))
    fixed = lambda r, w: pl.BlockSpec((r, w), lambda i: (0, 0))
    qkv_specs, qkv_shapes = [], []
    for d in dilations:
        qkv_specs += [pl.BlockSpec((tm // d, d * GROUP_W), lambda i: (i, 0))] * 3
        qkv_shapes += [jax.ShapeDtypeStruct((m // d, d * GROUP_W), qkv_dtype)] * 3
    kvt_spec = pl.BlockSpec((N_GROUPS, tm, 2 * GROUP_W),
                            lambda i: (0, jnp.maximum(i - tail_first_step, 0), 0))
    scratch = [pltpu.VMEM((D_MODEL // LANES_V7X, tm, LANES_V7X), F32)]
    tail_rows = (CONV_W - 1) * SUBLANES_V7X
    if lru is None:
        lru = ()
        rnn_specs = (row(D_RNN), row(D_RNN))
        rnn_shapes = (jax.ShapeDtypeStruct((m, D_RNN), F32),) * 2
    else:
        rnn_specs = (row(D_RNN), fixed(tail_rows, D_RNN), fixed(1, D_RNN))
        rnn_shapes = (jax.ShapeDtypeStruct((m, D_RNN), BF16),
                      jax.ShapeDtypeStruct((tail_rows, D_RNN), F32),
                      jax.ShapeDtypeStruct((1, D_RNN), F32))
        padded = tm + SUBLANES_V7X * SUBLANES_V7X
        scratch += [pltpu.VMEM((D_RNN // LANES_V7X, padded, LANES_V7X), F32),
                    pltpu.VMEM((D_MODEL // LANES_V7X, padded, LANES_V7X), F32),
                    pltpu.VMEM((tail_rows, D_RNN), F32),
                    pltpu.VMEM((1, D_RNN), F32)]
    out_shape = tuple(qkv_shapes) + rnn_shapes + (
        jax.ShapeDtypeStruct((m, D_MODEL), F32),
        jax.ShapeDtypeStruct((m, D_MODEL), F32),
        jax.ShapeDtypeStruct((N_GROUPS, tail, 2 * GROUP_W), F32),
    )
    consts = (g1, w_in, qg, kg) + tuple(lru)
    return pl.pallas_call(
        functools.partial(_inproj_kernel, dilations=dilations, tail_first_step=tail_first_step,
                          fuse_lru=bool(lru)),
        grid=(nt,),
        in_specs=[row(D_MODEL)] + [_const_spec(c.shape) for c in consts],
        out_specs=tuple(qkv_specs) + rnn_specs + (row(D_MODEL), row(D_MODEL), kvt_spec),
        out_shape=out_shape,
        scratch_shapes=scratch,
        compiler_params=_params(("arbitrary",)),
        name="inproj_lru" if lru else "inproj",
    )(x, *consts)


ATTN_BLOCKS_PER_STEP = 8


def _attn_prompt_kernel(q_ref, kc_ref, kp_ref, vc_ref, vp_ref, *refs, sample_first_row):
    if sample_first_row is None:
        o_ref, lse_ref = refs
    else:
        qs_ref, ks_ref, vs_ref, c0_ref, c1_ref, c2_ref, o_ref, lse_ref, os_ref = refs
        flat_step = pl.program_id(0) * pl.num_programs(1) + pl.program_id(1)
        _attn_sample_row(flat_step, sample_first_row + flat_step, qs_ref, ks_ref, vs_ref,
                         (c0_ref, c1_ref, c2_ref), os_ref)
    step = pl.program_id(1)
    qi = lax.broadcasted_iota(jnp.int32, (BLOCK, 2 * BLOCK), 0)
    kj = lax.broadcasted_iota(jnp.int32, (BLOCK, 2 * BLOCK), 1)
    dist = BLOCK + qi - kj
    band = (dist >= 0) & (dist <= BLOCK)
    first_band = band & ((step > 0) | (kj >= BLOCK))
    band2 = jnp.concatenate([band, band], axis=0)
    first_band2 = jnp.concatenate([first_band, first_band], axis=0)
    first_head = lax.broadcasted_iota(jnp.int32, (1, LANES_V7X), 1) < HEAD_DIM
    zero = jnp.zeros((), BF16)

    for blk in range(q_ref.shape[0] // BLOCK):
        rows = slice(blk * BLOCK, (blk + 1) * BLOCK)
        prev_rows = slice((blk - 1) * BLOCK, blk * BLOCK)
        valid = band2 if blk else first_band2
        lse_ref[rows, :] = jnp.zeros((BLOCK, LANES_V7X), F32)
        for pair in range(HEADS_PER_GROUP // 2):
            cols = slice(pair * LANES_V7X, (pair + 1) * LANES_V7X)
            qp = q_ref[rows, cols]
            k_prev = kc_ref[prev_rows, cols] if blk else kp_ref[:, cols]
            v_prev = vc_ref[prev_rows, cols] if blk else vp_ref[:, cols]
            kk = jnp.concatenate([k_prev, kc_ref[rows, cols]], axis=0)
            vv = jnp.concatenate([v_prev, vc_ref[rows, cols]], axis=0)
            q2 = jnp.concatenate([jnp.where(first_head, qp, zero), jnp.where(first_head, zero, qp)], axis=0)
            s = jnp.where(valid, _mm_nt(q2, kk), NEG_INF)
            mx = jnp.max(s, axis=-1, keepdims=True)
            p = jnp.exp2(s - mx)
            den = jnp.sum(p, axis=-1, keepdims=True)
            pv = _mm(p.astype(BF16), vv)
            o_ref[rows, cols] = jnp.where(first_head, pv[:BLOCK], pv[BLOCK:]).astype(o_ref.dtype)
            for e, (lo, hi) in enumerate(((0, BLOCK), (BLOCK, 2 * BLOCK))):
                head = 2 * pair + e
                lse_ref[rows, head:head + 1] = mx[lo:hi]
                lse_ref[rows, HEADS_PER_GROUP + head:HEADS_PER_GROUP + head + 1] = den[lo:hi]


def _attn_prompt(q, k, v, g, dilation, sample=None):
    m_len = q.shape[0]
    nbs = ATTN_BLOCKS_PER_STEP
    rows = nbs * BLOCK
    n_inner = m_len // rows
    cur = pl.BlockSpec((rows, GROUP_W), lambda r, b: (b, r))
    prev = pl.BlockSpec((BLOCK, GROUP_W), lambda r, b: (jnp.maximum(b * nbs - 1, 0), r))
    in_specs, args = [cur, cur, prev, cur, prev], [q, k, k, v, v]
    out_specs = [pl.BlockSpec((rows, GROUP_W), lambda r, b: (b, r)),
                 pl.BlockSpec((rows, LANES_V7X), lambda r, b: (b, r))]
    out_shape = [jax.ShapeDtypeStruct((m_len, dilation * GROUP_W), BF16),
                 jax.ShapeDtypeStruct((m_len, dilation * LANES_V7X), F32)]
    first_row = None
    if sample is not None:
        qs, ks, vs, caches, first_row = sample
        steps = dilation * n_inner
        full = pl.BlockSpec(qs.shape, lambda r, b: (0, 0, 0))
        in_specs += [full, full, full] + [
            pl.BlockSpec((None,) + c.shape[1:], lambda r, b: (first_row + r * n_inner + b, 0, 0))
            for c in caches]
        args += [qs, ks, vs, *caches]
        out_specs.append(pl.BlockSpec((steps, GROUP_W), lambda r, b: (0, 0)))
        out_shape.append(jax.ShapeDtypeStruct((steps, GROUP_W), F32))
    return pl.pallas_call(
        functools.partial(_attn_prompt_kernel, sample_first_row=first_row),
        grid=(dilation, n_inner),
        in_specs=in_specs,
        out_specs=tuple(out_specs),
        out_shape=tuple(out_shape),
        compiler_params=_params(("arbitrary", "arbitrary")),
        name=f"attn_prompt_g{g}",
    )(*args)


def _attn_sample_row(out_row, b, q_ref, k_ref, v_ref, cache_refs, o_ref):
    head_row = lax.broadcasted_iota(jnp.int32, (HEADS_PER_GROUP, GROUP_W), 0)
    head_lane = lax.broadcasted_iota(jnp.int32, (HEADS_PER_GROUP, GROUP_W), 1) // HEAD_DIM
    own = head_row == head_lane
    parts = []
    for g, c_ref in enumerate(cache_refs):
        window, dilation = GROUPS[g]
        bf = lambda t: t.astype(BF16).astype(F32)
        qmat = jnp.where(own, jnp.broadcast_to(q_ref[g, pl.ds(b, 1), :], (HEADS_PER_GROUP, GROUP_W)), 0.0)
        qmat = qmat.astype(BF16)
        knew = bf(k_ref[g, pl.ds(b, 1), :])
        vnew = bf(v_ref[g, pl.ds(b, 1), :])
        pos = lax.broadcasted_iota(jnp.int32, (HEADS_PER_GROUP, window), 1)
        s = _mm(qmat, c_ref[0:GROUP_W, :].astype(BF16))
        s = jnp.where((pos & (dilation - 1)) == 0, s, NEG_INF)
        s_new = jnp.sum(qmat.astype(F32) * knew, axis=-1, keepdims=True)
        mx = jnp.maximum(jnp.max(s, axis=-1, keepdims=True), s_new)
        p = jnp.exp2(s - mx)
        p_new = jnp.exp2(s_new - mx)
        den = jnp.sum(p, axis=-1, keepdims=True) + p_new
        numer = _mm_nt(p.astype(BF16), c_ref[GROUP_W:2 * GROUP_W, :].astype(BF16)) + bf(p_new) * vnew
        parts.append((jnp.where(own, numer, 0.0), mx, den))
    m_all = functools.reduce(jnp.maximum, [mx for _, mx, _ in parts])
    ws = [jnp.exp2(mx - m_all) for _, mx, _ in parts]
    num = sum(n * w for (n, _, _), w in zip(parts, ws))
    den = sum(d * w for (_, _, d), w in zip(parts, ws))
    o_ref[pl.ds(out_row, 1), :] = jnp.sum(num / den, axis=0, keepdims=True)


def _gelu_tanh(x):
    cdf = 0.5 * (1.0 + jnp.tanh(0.7978845608028654 * (x + 0.044715 * (x * x * x))))
    return x * cdf


def _softplus(x):
    return jnp.maximum(x, 0.0) + jnp.log1p(jnp.exp(-jnp.abs(x)))


LRU_CHUNK = 2 * RNN_BLOCK_W
N_LRU_CHUNKS = D_RNN // LRU_CHUNK


def _lru_lanes(n):
    return slice(n * LRU_CHUNK, (n + 1) * LRU_CHUNK)


def _lru_coeffs(xc, n, wrg_ref, brg_ref, wig_ref, big_ref, lam_ref):
    lanes = _lru_lanes(n)
    sigmoid = lambda v: 0.5 + 0.5 * jnp.tanh(0.5 * v)
    xcb = xc.astype(BF16)
    r = sigmoid(_mm(xcb, wrg_ref[n]) + brg_ref[:, lanes])
    i = sigmoid(_mm(xcb, wig_ref[n]) + big_ref[:, lanes])
    neg_rate = LRU_C * _softplus(-lam_ref[:, lanes])
    a = jnp.exp2(r * (-LOG2_E * neg_rate))
    t = jnp.tanh(r * neg_rate)
    y = 2.0 * t / (1.0 + t)
    root = jnp.where(y > 0.0, y * lax.rsqrt(y), 0.0)
    return a, root * i * xc


def _pair_blocks(w):
    z = jnp.zeros_like(w[0::2])
    top = jnp.concatenate([w[0::2], z], axis=2)
    bot = jnp.concatenate([z, w[1::2]], axis=2)
    return jnp.concatenate([top, bot], axis=1)


def _lru_tile(xb, gb, n, lru_refs, ctail_ref, hcar_ref):
    cw_ref, cb_ref, wrg_ref, brg_ref, wig_ref, big_ref, lam_ref = lru_refs
    lanes = _lru_lanes(n)
    sl = SUBLANES_V7X
    tm = xb.shape[0]
    nj = tm // sl
    sub = lax.broadcasted_iota(jnp.int32, (sl, LRU_CHUNK), 0)
    vrow = lambda t, j: t[j * sl:(j + 1) * sl]

    taps = CONV_W - 1
    wrap = [pltpu.roll(jnp.where(sub == sl - 1, vrow(ctail_ref[:, lanes], i), vrow(xb, nj - taps + i)), 1, 0)
            for i in range(taps)]
    ctail_ref[:, lanes] = xb[tm - taps * sl:]
    xc = cb_ref[:, lanes] + xb * cw_ref[taps:taps + 1, lanes]
    for k in range(1, CONV_W):
        shifted = jnp.concatenate(wrap[taps - k:] + [xb[:tm - k * sl]], axis=0)
        xc = xc + shifted * cw_ref[taps - k:taps - k + 1, lanes]

    a, b = _lru_coeffs(xc, n, wrg_ref, brg_ref, wig_ref, big_ref, lam_ref)
    gate = _gelu_tanh(gb)

    hl, acc = vrow(b, 0), vrow(a, 0)
    hls, accs = [hl], [acc]
    for j in range(1, nj):
        hl = vrow(a, j) * hl + vrow(b, j)
        acc = vrow(a, j) * acc
        hls.append(hl)
        accs.append(acc)

    for s in (1, 2, 4):
        keep = sub >= s
        acc_prev = jnp.where(keep, pltpu.roll(acc, s, 0), 1.0)
        hl_prev = jnp.where(keep, pltpu.roll(hl, s, 0), 0.0)
        hl = acc * hl_prev + hl
        acc = acc * acc_prev
    carry = hcar_ref[:, lanes]
    h_end = hl + acc * carry
    h_in = jnp.where(sub == 0, carry, pltpu.roll(h_end, 1, 0))
    hcar_ref[:, lanes] = h_end[sl - 1:sl]

    return [(hls[j] + accs[j] * h_in) * vrow(gate, j) for j in range(nj)]


def _rglru_sample_kernel(xb_ref, gb_ref, sc_ref, h0_ref, cw_ref, cb_ref, wrg_ref, brg_ref, wig_ref,
                         big_ref, lam_ref, hb_ref, conv_ref, h_ref):
    xb = xb_ref[...]
    taps = [sc_ref[:, j * D_RNN:(j + 1) * D_RNN] for j in range(CONV_W - 1)] + [xb]
    xc = cb_ref[...] + sum(t * cw_ref[j:j + 1, :] for j, t in enumerate(taps))
    for n in range(N_LRU_CHUNKS):
        lanes = _lru_lanes(n)
        a, b = _lru_coeffs(xc[:, lanes], n, wrg_ref, brg_ref, wig_ref, big_ref, lam_ref)
        h = a * h0_ref[:, lanes] + b
        h_ref[:, lanes] = h
        hb_ref[:, lanes] = (h * _gelu_tanh(gb_ref[:, lanes])).astype(hb_ref.dtype)
    for j in range(CONV_W - 1):
        conv_ref[:, j * D_RNN:(j + 1) * D_RNN] = taps[j + 1]


def _rglru_sample(xb, gb, state_conv, h0, cw, cb, wrg, brg, wig, big, lam):
    nb = xb.shape[0]
    args = (xb, gb, state_conv, h0, cw, cb, wrg, brg, wig, big, lam)
    return pl.pallas_call(
        _rglru_sample_kernel,
        grid=(1,),
        in_specs=[_const_spec(a.shape) for a in args],
        out_specs=(_const_spec((nb, D_RNN)), _const_spec((nb, (CONV_W - 1) * D_RNN)),
                   _const_spec((nb, D_RNN))),
        out_shape=(jax.ShapeDtypeStruct((nb, D_RNN), BF16),
                   jax.ShapeDtypeStruct((nb, (CONV_W - 1) * D_RNN), F32),
                   jax.ShapeDtypeStruct((nb, D_RNN), F32)),
        compiler_params=_params(("arbitrary",)),
        name="rglru_sample",
    )(*args)


def _post_kernel(*refs, dilations):
    n_parts = len(dilations)
    attn_refs = refs[:2 * n_parts] if n_parts > 1 else refs[:1]
    rest = refs[len(attn_refs):]
    (hb_ref, ga_ref, gb2_ref, x_ref, expand_ref, bg_ref, woa_ref, wor_ref, wout_ref, g2_ref,
     wfi_ref, wfo_ref, y_ref) = rest[:13]
    tm = x_ref.shape[0]

    if n_parts > 1:
        o_nat_ref, lse_nat_ref = rest[13:]

        def natural(src_ref, dst_ref, width, d):
            if d == 1:
                return src_ref[...].astype(F32)
            n_slabs = width // LANES_V7X
            for r in range(d):
                for c in range(n_slabs):
                    lanes = slice(r * width + c * LANES_V7X, r * width + (c + 1) * LANES_V7X)
                    dst_ref[c, pl.ds(r, tm // d, stride=d), :] = src_ref[:, lanes].astype(F32)
            return jnp.concatenate([dst_ref[c] for c in range(n_slabs)], axis=1)

        o_refs, stat_refs = attn_refs[:n_parts], attn_refs[n_parts:]
        stats = [natural(r, lse_nat_ref, LANES_V7X, d) for r, d in zip(stat_refs, dilations)]
        m_all = functools.reduce(jnp.maximum, stats)
        ws = [jnp.exp2(st - m_all) for st in stats]
        dens = [pltpu.roll(st, LANES_V7X - HEADS_PER_GROUP, 1) for st in stats]
        inv = 1.0 / sum(w * dn for w, dn in zip(ws, dens))
        is_head = lax.broadcasted_iota(jnp.int32, (1, LANES_V7X), 1) < HEADS_PER_GROUP
        attn = 0.0
        for o_ref, w, d in zip(o_refs, ws, dilations):
            cw = jnp.where(is_head, w * inv, 0.0)
            hi = cw.astype(BF16)
            lo = (cw - hi.astype(F32)).astype(BF16)
            wide = _mm(jnp.concatenate([hi, lo], axis=1), expand_ref[...])
            attn = attn + wide * natural(o_ref, o_nat_ref, GROUP_W, d)
    else:
        attn = attn_refs[0][...]

    ya = _mm(attn.astype(BF16), woa_ref[...])
    yb = _mm(hb_ref[...], wor_ref[...])
    merged = (jax.nn.sigmoid(ga_ref[...] + bg_ref[0:1, :]) * ya
              + jax.nn.sigmoid(gb2_ref[...] + bg_ref[1:2, :]) * yb)
    x1 = x_ref[...] + _mm(merged.astype(BF16), wout_ref[...])
    hn2 = _rms_norm_rows(x1, g2_ref[...]).astype(BF16)
    gu = _mm(hn2, wfi_ref[...])
    act = jax.nn.silu(gu[:, :D_FF]) * gu[:, D_FF:]
    y_ref[...] = x1 + _mm(act.astype(BF16), wfo_ref[...])


def _post(attn_parts, dilations, hb, ga, gb2, x, expand, bg, woa, wor, wout, g2, wfi, wfo, *, tm):
    m = x.shape[0]
    row = lambda w: pl.BlockSpec((tm, w), lambda i: (i, 0))
    n_parts = len(attn_parts)
    scratch = []
    if n_parts > 1:
        blocked = lambda w, d: pl.BlockSpec((tm // d, d * w), lambda i: (i, 0))
        attn_args = [o for o, _ in attn_parts] + [l for _, l in attn_parts]
        attn_specs = ([blocked(GROUP_W, d) for d in dilations]
                      + [blocked(LANES_V7X, d) for d in dilations])
        scratch = [pltpu.VMEM((GROUP_W // LANES_V7X, tm, LANES_V7X), F32),
                   pltpu.VMEM((1, tm, LANES_V7X), F32)]
    else:
        attn_args, attn_specs = list(attn_parts), [row(GROUP_W)]
    consts = (expand, bg, woa, wor, wout, g2, wfi, wfo)
    return pl.pallas_call(
        functools.partial(_post_kernel, dilations=dilations),
        grid=(m // tm,),
        in_specs=attn_specs + [row(D_RNN), row(D_MODEL), row(D_MODEL), row(D_MODEL)]
        + [_const_spec(c.shape) for c in consts],
        out_specs=row(D_MODEL),
        out_shape=jax.ShapeDtypeStruct((m, D_MODEL), F32),
        scratch_shapes=scratch,
        compiler_params=_params(("arbitrary",)),
        name="post_prompt" if n_parts > 1 else "post_sample",
    )(*attn_args, hb, ga, gb2, x, *consts)


def kernel(x_prompt, x_sample, cache_kv_w128, cache_kv_w512, cache_kv_w2048, state_conv, state_h,
           norm1_g, w_in, b_gate, q_norm_g, k_norm_g, conv_w, conv_b, w_rg, b_rg, w_ig, b_ig,
           lru_lambda, w_o_attn, w_o_rnn, w_out, norm2_g, w_ffn_in, w_ffn_out):
    assert x_prompt.shape[0] == 1 and norm1_g.shape[0] == 1 and x_sample.shape[1] == 1
    seq = x_prompt.shape[1]
    nb = x_sample.shape[0]
    layer = 0

    expand = (jnp.arange(LANES_V7X)[:, None] == (jnp.arange(GROUP_W) // HEAD_DIM)[None, :]).astype(BF16)
    expand = jnp.concatenate([expand, expand], axis=0)

    row2 = lambda t: t[layer].reshape(1, -1)
    g1, g2 = row2(norm1_g), row2(norm2_g)
    qg = jnp.tile(q_norm_g[layer], HEADS_PER_GROUP).reshape(1, GROUP_W) * (ATTN_SCALE * LOG2_E)
    kg = jnp.tile(k_norm_g[layer], HEADS_PER_GROUP).reshape(1, GROUP_W)
    w_in_b = w_in[layer].astype(BF16)
    wrg, wig = _pair_blocks(w_rg[layer].astype(BF16)), _pair_blocks(w_ig[layer].astype(BF16))
    woa, wor, wout = (w[layer].astype(BF16) for w in (w_o_attn, w_o_rnn, w_out))
    wfi, wfo = w_ffn_in[layer].astype(BF16), w_ffn_out[layer].astype(BF16)
    lru = (conv_w[layer], row2(conv_b), wrg, row2(b_rg), wig, row2(b_ig), row2(lru_lambda))
    post_w = (expand, b_gate[layer], woa, wor, wout, g2, wfi, wfo)

    xs = x_sample[:, 0]
    *qkvs, xbs, gbs, gas, gb2s, kvts = _inproj(xs, g1, w_in_b, qg, kg, tm=nb, tail=nb,
                                               qkv_dtype=F32, dilations=(1,) * N_GROUPS)
    qs, ks, vs = (jnp.stack(qkvs[j::3]) for j in range(3))
    caches = [jnp.transpose(c[layer], (0, 2, 3, 4, 1)).reshape(nb, 2 * GROUP_W, c.shape[2])
              for c in (cache_kv_w128, cache_kv_w512, cache_kv_w2048)]

    xp = x_prompt[0]
    *qkv, hb, conv_rows, h_last, ga, gb2, kvt = _inproj(
        xp, g1, w_in_b, qg, kg, lru, tm=256, tail=MAX_WINDOW, qkv_dtype=BF16, dilations=DILATIONS)
    half = nb // 2
    parts, attn_s = [], []
    for g, d in enumerate(DILATIONS):
        sample = (qs, ks, vs, caches, (g - 1) * half) if g else None
        *part, = _attn_prompt(*qkv[3 * g:3 * g + 3], g, d, sample)
        parts.append(part[:2])
        attn_s += part[2:]
    attn_s = jnp.concatenate(attn_s, axis=0)
    assert attn_s.shape == (nb, GROUP_W)
    y_p = _post(parts, DILATIONS, hb, ga, gb2, xp, *post_w, tm=256)
    conv_p = conv_rows[SUBLANES_V7X - 1::SUBLANES_V7X]

    hbs, conv_s, h_s = _rglru_sample(xbs, gbs, state_conv[layer].reshape(nb, -1), state_h[layer], *lru)
    y_s = _post([attn_s], (1,), hbs, gas, gb2s, xs, *post_w, tm=nb)

    kv_shape = lambda rows: (1, 1, rows, 2, HEADS_PER_GROUP, HEAD_DIM)
    kv_prompt = [kvt[g, MAX_WINDOW - min(w, seq):].reshape(kv_shape(min(w, seq)))
                 for g, (w, _) in enumerate(GROUPS)]
    kv_sample = [kvts[g].reshape(1, nb, 1, 2, HEADS_PER_GROUP, HEAD_DIM) for g in range(N_GROUPS)]
    return (y_p[None], y_s[:, None],
            kv_prompt[0], kv_prompt[1], kv_prompt[2],
            conv_p[None, None], h_last[None],
            kv_sample[0], kv_sample[1], kv_sample[2],
            conv_s.reshape(1, nb, CONV_W - 1, D_RNN), h_s[None])
```

```python
import functools

import jax
import jax.numpy as jnp
from jax import lax
from jax.experimental import pallas as pl
from jax.experimental.pallas import tpu as pltpu

F32 = jnp.float32
BF16 = jnp.bfloat16

D_MODEL = 1024
HEAD_DIM = 64
HEADS_PER_GROUP = 8
GROUPS = ((128, 1), (512, 4), (2048, 16))
DILATIONS = tuple(d for _, d in GROUPS)
N_GROUPS = len(GROUPS)
GROUP_W = HEADS_PER_GROUP * HEAD_DIM
QKV_WIDTH = N_GROUPS * GROUP_W
BLOCK = 128
ATTN_SCALE = HEAD_DIM ** -0.5
LOG2_E = 1.4426950408889634
NEG_INF = -1e30
D_RNN = 1280
RNN_BLOCKS = 10
RNN_BLOCK_W = D_RNN // RNN_BLOCKS
CONV_W = 4
LRU_C = 8.0
D_FF = 2816
RMS_EPS = 1e-6
MAX_WINDOW = max(w for w, _ in GROUPS)

LANES_V7X = 128
SUBLANES_V7X = 8
MXU_DIM_V7X = 256
VMEM_LIMIT_BYTES = 56 * 1024 * 1024

OFF_Q, OFF_K, OFF_V = 0, QKV_WIDTH, 2 * QKV_WIDTH
OFF_XB = 3 * QKV_WIDTH
OFF_GB = OFF_XB + D_RNN
OFF_GA = OFF_GB + D_RNN
OFF_GB2 = OFF_GA + D_MODEL


def _mm(a, b):
    return jnp.dot(a, b, preferred_element_type=F32)


def _mm_nt(a, b):
    return lax.dot_general(a, b, (((1,), (1,)), ((), ())), preferred_element_type=F32)


def _rms_norm_rows(x, g):
    return x * lax.rsqrt(jnp.mean(x * x, axis=-1, keepdims=True) + RMS_EPS) * g


def _const_spec(shape):
    nd = len(shape)
    return pl.BlockSpec(shape, lambda *_: (0,) * nd, pipeline_mode=pl.Buffered(1))


def _params(sem):
    return pltpu.CompilerParams(dimension_semantics=sem, vmem_limit_bytes=VMEM_LIMIT_BYTES)


def _inproj_kernel(x_ref, g1_ref, w_ref, qg_ref, kg_ref, *refs, dilations, tail_first_step,
                   fuse_lru, n_cast):
    if fuse_lru:
        lru_refs, refs = refs[:7], refs[7:]
    cast_in, refs = refs[:n_cast], refs[n_cast:]
    qkv_refs, refs = refs[:3 * N_GROUPS], refs[3 * N_GROUPS:]
    if fuse_lru:
        hb_ref, conv_ref, hlast_ref, ga_ref, gb2_ref, kvt_ref = refs[:6]
        cast_out = refs[6:6 + n_cast]
        hn_ref, hbs_ref, hnp_ref, ctail_ref, hcar_ref = refs[6 + n_cast:]
    else:
        xb_ref, gbr_ref, ga_ref, gb2_ref, kvt_ref = refs[:5]
        cast_out = refs[5:5 + n_cast]
        (hn_ref,) = refs[5 + n_cast:]
    for src_ref, dst_ref in zip(cast_in, cast_out):
        dst_ref[...] = src_ref[...].astype(dst_ref.dtype)
    step = pl.program_id(0)
    tm = x_ref.shape[0]
    hn32 = _rms_norm_rows(x_ref[...], g1_ref[...])
    hn = hn32.astype(BF16)
    n_slabs = D_MODEL // LANES_V7X
    for c in range(n_slabs):
        hn_ref[c] = hn32[:, c * LANES_V7X:(c + 1) * LANES_V7X]

    def strided_rows(start, size, stride):
        return jnp.concatenate([hn_ref[c, pl.ds(start, size, stride=stride), :] for c in range(n_slabs)],
                               axis=1)

    if fuse_lru:
        @pl.when(step == 0)
        def _():
            ctail_ref[...] = jnp.zeros(ctail_ref.shape, F32)
            hcar_ref[...] = jnp.zeros(hcar_ref.shape, F32)

        nj = tm // SUBLANES_V7X
        pitch = nj + SUBLANES_V7X
        for s in range(SUBLANES_V7X):
            for c in range(n_slabs):
                hnp_ref[c, s * pitch:s * pitch + nj, :] = hn32[s * nj:(s + 1) * nj, c * LANES_V7X:(c + 1) * LANES_V7X]
        ht = jnp.concatenate(
            [jnp.concatenate([hnp_ref[c, pl.ds(j, SUBLANES_V7X, stride=pitch), :] for c in range(n_slabs)],
                             axis=1) for j in range(nj)], axis=0).astype(BF16)
        per_vreg = LRU_CHUNK // LANES_V7X
        for n in range(N_LRU_CHUNKS):
            c0 = n * LRU_CHUNK
            hb_rows = _lru_tile(_mm(ht, w_ref[:, OFF_XB + c0:OFF_XB + c0 + LRU_CHUNK]),
                                _mm(ht, w_ref[:, OFF_GB + c0:OFF_GB + c0 + LRU_CHUNK]),
                                n, lru_refs, ctail_ref, hcar_ref)
            for j, rows in enumerate(hb_rows):
                for c in range(per_vreg):
                    hbs_ref[n * per_vreg + c, pl.ds(j, SUBLANES_V7X, stride=pitch), :] = (
                        rows[:, c * LANES_V7X:(c + 1) * LANES_V7X])
        conv_ref[...] = ctail_ref[...]
        hlast_ref[...] = hcar_ref[...]
        hb_ref[...] = jnp.concatenate(
            [jnp.concatenate([hbs_ref[c, s * pitch:s * pitch + nj, :] for s in range(SUBLANES_V7X)], axis=0)
             for c in range(D_RNN // LANES_V7X)], axis=1).astype(hb_ref.dtype)
    else:
        xb_ref[...] = _mm(hn, w_ref[:, OFF_XB:OFF_XB + D_RNN])
        gbr_ref[...] = _mm(hn, w_ref[:, OFF_GB:OFF_GB + D_RNN])

    def head_norm(t, gain):
        first_head = lax.broadcasted_iota(jnp.int32, (1, LANES_V7X), 1) < HEAD_DIM
        cols = []
        for c in range(0, GROUP_W, LANES_V7X):
            x = t[:, c:c + LANES_V7X]
            xx = x * x
            s0 = jnp.sum(jnp.where(first_head, xx, 0.0), axis=-1, keepdims=True)
            s1 = jnp.sum(jnp.where(first_head, 0.0, xx), axis=-1, keepdims=True)
            ms = jnp.where(first_head, s0, s1) * (1.0 / HEAD_DIM)
            cols.append(x * lax.rsqrt(ms + RMS_EPS))
        return jnp.concatenate(cols, axis=1) * gain

    def qkv(h, g):
        c = g * GROUP_W
        qn = head_norm(_mm(h, w_ref[:, OFF_Q + c:OFF_Q + c + GROUP_W]), qg_ref[...])
        kn = head_norm(_mm(h, w_ref[:, OFF_K + c:OFF_K + c + GROUP_W]), kg_ref[...])
        vv = _mm(h, w_ref[:, OFF_V + c:OFF_V + c + GROUP_W])
        return qn, kn, vv

    in_order = {}
    for g, d in enumerate(dilations):
        rows = tm // d
        if d == 1:
            hg = hn
        else:
            hg = jnp.concatenate([strided_rows(r, rows, d) for r in range(d)], axis=0).astype(BF16)
        parts = qkv(hg, g)
        if d == 1:
            in_order[g] = parts
        for t, o_ref in zip(parts, qkv_refs[3 * g:3 * g + 3]):
            for r in range(d):
                o_ref[:, r * GROUP_W:(r + 1) * GROUP_W] = t[r * rows:(r + 1) * rows].astype(o_ref.dtype)

    ga_ref[...] = _mm(hn, w_ref[:, OFF_GA:OFF_GA + D_MODEL])
    gb2_ref[...] = _mm(hn, w_ref[:, OFF_GB2:OFF_GB2 + D_MODEL])

    @pl.when(step >= tail_first_step)
    def _():
        for g in range(N_GROUPS):
            _, kn, vv = in_order[g] if g in in_order else qkv(hn, g)
            kvt_ref[g, :, 0:GROUP_W] = kn
            kvt_ref[g, :, GROUP_W:2 * GROUP_W] = vv


def _inproj(x, g1, w_in, qg, kg, lru=None, to_bf16=(), *, tm, tail, qkv_dtype, dilations):
    m = x.shape[0]
    nt = m // tm
    tail_first_step = nt - tail // tm
    row = lambda w: pl.BlockSpec((tm, w), lambda i: (i, 0))
    fixed = lambda r, w: pl.BlockSpec((r, w), lambda i: (0, 0))
    qkv_specs, qkv_shapes = [], []
    for d in dilations:
        qkv_specs += [pl.BlockSpec((tm // d, d * GROUP_W), lambda i: (i, 0))] * 3
        qkv_shapes += [jax.ShapeDtypeStruct((m // d, d * GROUP_W), qkv_dtype)] * 3
    kvt_spec = pl.BlockSpec((N_GROUPS, tm, 2 * GROUP_W),
                            lambda i: (0, jnp.maximum(i - tail_first_step, 0), 0))
    scratch = [pltpu.VMEM((D_MODEL // LANES_V7X, tm, LANES_V7X), F32)]
    tail_rows = (CONV_W - 1) * SUBLANES_V7X
    if lru is None:
        lru = ()
        rnn_specs = (row(D_RNN), row(D_RNN))
        rnn_shapes = (jax.ShapeDtypeStruct((m, D_RNN), F32),) * 2
    else:
        rnn_specs = (row(D_RNN), fixed(tail_rows, D_RNN), fixed(1, D_RNN))
        rnn_shapes = (jax.ShapeDtypeStruct((m, D_RNN), BF16),
                      jax.ShapeDtypeStruct((tail_rows, D_RNN), F32),
                      jax.ShapeDtypeStruct((1, D_RNN), F32))
        padded = tm + SUBLANES_V7X * SUBLANES_V7X
        scratch += [pltpu.VMEM((D_RNN // LANES_V7X, padded, LANES_V7X), F32),
                    pltpu.VMEM((D_MODEL // LANES_V7X, padded, LANES_V7X), F32),
                    pltpu.VMEM((tail_rows, D_RNN), F32),
                    pltpu.VMEM((1, D_RNN), F32)]
    out_shape = tuple(qkv_shapes) + rnn_shapes + (
        jax.ShapeDtypeStruct((m, D_MODEL), F32),
        jax.ShapeDtypeStruct((m, D_MODEL), F32),
        jax.ShapeDtypeStruct((N_GROUPS, tail, 2 * GROUP_W), F32),
    )
    consts = (g1, w_in, qg, kg) + tuple(lru)
    cast_specs, cast_shapes = [], []
    bf16_rows = 2 * SUBLANES_V7X
    for w in to_bf16:
        nblk = nt
        while w.shape[0] % (nblk * bf16_rows):
            nblk //= 2
        cast_specs.append(pl.BlockSpec((w.shape[0] // nblk, w.shape[1]),
                                       lambda i, per=nt // nblk: (i // per, 0)))
        cast_shapes.append(jax.ShapeDtypeStruct(w.shape, BF16))
    return pl.pallas_call(
        functools.partial(_inproj_kernel, dilations=dilations, tail_first_step=tail_first_step,
                          fuse_lru=bool(lru), n_cast=len(to_bf16)),
        grid=(nt,),
        in_specs=[row(D_MODEL)] + [_const_spec(c.shape) for c in consts] + cast_specs,
        out_specs=tuple(qkv_specs) + rnn_specs + (row(D_MODEL), row(D_MODEL), kvt_spec) + tuple(cast_specs),
        out_shape=out_shape + tuple(cast_shapes),
        scratch_shapes=scratch,
        compiler_params=_params(("arbitrary",)),
        name="inproj_lru" if lru else "inproj",
    )(x, *consts, *to_bf16)


ATTN_BLOCKS_PER_STEP = 8


def _attn_prompt_kernel(q_ref, kc_ref, kp_ref, vc_ref, vp_ref, *refs, sample_first_row):
    if sample_first_row is None:
        o_ref, lse_ref = refs
    else:
        qs_ref, ks_ref, vs_ref, c0_ref, c1_ref, c2_ref, o_ref, lse_ref, os_ref = refs
        flat_step = pl.program_id(0) * pl.num_programs(1) + pl.program_id(1)
        _attn_sample_row(flat_step, sample_first_row + flat_step, qs_ref, ks_ref, vs_ref,
                         (c0_ref, c1_ref, c2_ref), os_ref)
    step = pl.program_id(1)
    qi = lax.broadcasted_iota(jnp.int32, (BLOCK, 2 * BLOCK), 0)
    kj = lax.broadcasted_iota(jnp.int32, (BLOCK, 2 * BLOCK), 1)
    dist = BLOCK + qi - kj
    band = (dist >= 0) & (dist <= BLOCK)
    first_band = band & ((step > 0) | (kj >= BLOCK))
    band2 = jnp.concatenate([band, band], axis=0)
    first_band2 = jnp.concatenate([first_band, first_band], axis=0)
    first_head = lax.broadcasted_iota(jnp.int32, (1, LANES_V7X), 1) < HEAD_DIM
    zero = jnp.zeros((), BF16)

    for blk in range(q_ref.shape[0] // BLOCK):
        rows = slice(blk * BLOCK, (blk + 1) * BLOCK)
        prev_rows = slice((blk - 1) * BLOCK, blk * BLOCK)
        valid = band2 if blk else first_band2
        lse_ref[rows, :] = jnp.zeros((BLOCK, LANES_V7X), F32)
        for pair in range(HEADS_PER_GROUP // 2):
            cols = slice(pair * LANES_V7X, (pair + 1) * LANES_V7X)
            qp = q_ref[rows, cols]
            k_prev = kc_ref[prev_rows, cols] if blk else kp_ref[:, cols]
            v_prev = vc_ref[prev_rows, cols] if blk else vp_ref[:, cols]
            kk = jnp.concatenate([k_prev, kc_ref[rows, cols]], axis=0)
            vv = jnp.concatenate([v_prev, vc_ref[rows, cols]], axis=0)
            q2 = jnp.concatenate([jnp.where(first_head, qp, zero), jnp.where(first_head, zero, qp)], axis=0)
            s = jnp.where(valid, _mm_nt(q2, kk), NEG_INF)
            mx = jnp.max(s, axis=-1, keepdims=True)
            p = jnp.exp2(s - mx)
            den = jnp.sum(p, axis=-1, keepdims=True)
            pv = _mm(p.astype(BF16), vv)
            o_ref[rows, cols] = jnp.where(first_head, pv[:BLOCK], pv[BLOCK:]).astype(o_ref.dtype)
            for e, (lo, hi) in enumerate(((0, BLOCK), (BLOCK, 2 * BLOCK))):
                head = 2 * pair + e
                lse_ref[rows, head:head + 1] = mx[lo:hi]
                lse_ref[rows, HEADS_PER_GROUP + head:HEADS_PER_GROUP + head + 1] = den[lo:hi]


def _attn_prompt(q, k, v, g, dilation, sample=None):
    m_len = q.shape[0]
    nbs = ATTN_BLOCKS_PER_STEP
    rows = nbs * BLOCK
    n_inner = m_len // rows
    cur = pl.BlockSpec((rows, GROUP_W), lambda r, b: (b, r))
    prev = pl.BlockSpec((BLOCK, GROUP_W), lambda r, b: (jnp.maximum(b * nbs - 1, 0), r))
    in_specs, args = [cur, cur, prev, cur, prev], [q, k, k, v, v]
    out_specs = [pl.BlockSpec((rows, GROUP_W), lambda r, b: (b, r)),
                 pl.BlockSpec((rows, LANES_V7X), lambda r, b: (b, r))]
    out_shape = [jax.ShapeDtypeStruct((m_len, dilation * GROUP_W), BF16),
                 jax.ShapeDtypeStruct((m_len, dilation * LANES_V7X), F32)]
    first_row = None
    if sample is not None:
        qs, ks, vs, caches, first_row = sample
        steps = dilation * n_inner
        full = pl.BlockSpec(qs.shape, lambda r, b: (0, 0, 0))
        in_specs += [full, full, full] + [
            pl.BlockSpec((None,) + c.shape[1:], lambda r, b: (first_row + r * n_inner + b, 0, 0))
            for c in caches]
        args += [qs, ks, vs, *caches]
        out_specs.append(pl.BlockSpec((steps, GROUP_W), lambda r, b: (0, 0)))
        out_shape.append(jax.ShapeDtypeStruct((steps, GROUP_W), F32))
    return pl.pallas_call(
        functools.partial(_attn_prompt_kernel, sample_first_row=first_row),
        grid=(dilation, n_inner),
        in_specs=in_specs,
        out_specs=tuple(out_specs),
        out_shape=tuple(out_shape),
        compiler_params=_params(("arbitrary", "arbitrary")),
        name=f"attn_prompt_g{g}",
    )(*args)


def _attn_sample_row(out_row, b, q_ref, k_ref, v_ref, cache_refs, o_ref):
    head_row = lax.broadcasted_iota(jnp.int32, (HEADS_PER_GROUP, GROUP_W), 0)
    head_lane = lax.broadcasted_iota(jnp.int32, (HEADS_PER_GROUP, GROUP_W), 1) // HEAD_DIM
    own = head_row == head_lane
    parts = []
    for g, c_ref in enumerate(cache_refs):
        window, dilation = GROUPS[g]
        bf = lambda t: t.astype(BF16).astype(F32)
        qmat = jnp.where(own, jnp.broadcast_to(q_ref[g, pl.ds(b, 1), :], (HEADS_PER_GROUP, GROUP_W)), 0.0)
        qmat = qmat.astype(BF16)
        knew = bf(k_ref[g, pl.ds(b, 1), :])
        vnew = bf(v_ref[g, pl.ds(b, 1), :])
        pos = lax.broadcasted_iota(jnp.int32, (HEADS_PER_GROUP, window), 1)
        s = _mm(qmat, c_ref[0:GROUP_W, :].astype(BF16))
        s = jnp.where((pos & (dilation - 1)) == 0, s, NEG_INF)
        s_new = jnp.sum(qmat.astype(F32) * knew, axis=-1, keepdims=True)
        mx = jnp.maximum(jnp.max(s, axis=-1, keepdims=True), s_new)
        p = jnp.exp2(s - mx)
        p_new = jnp.exp2(s_new - mx)
        den = jnp.sum(p, axis=-1, keepdims=True) + p_new
        numer = _mm_nt(p.astype(BF16), c_ref[GROUP_W:2 * GROUP_W, :].astype(BF16)) + bf(p_new) * vnew
        parts.append((jnp.where(own, numer, 0.0), mx, den))
    m_all = functools.reduce(jnp.maximum, [mx for _, mx, _ in parts])
    ws = [jnp.exp2(mx - m_all) for _, mx, _ in parts]
    num = sum(n * w for (n, _, _), w in zip(parts, ws))
    den = sum(d * w for (_, _, d), w in zip(parts, ws))
    o_ref[pl.ds(out_row, 1), :] = jnp.sum(num / den, axis=0, keepdims=True)


def _gelu_tanh(x):
    cdf = 0.5 * (1.0 + jnp.tanh(0.7978845608028654 * (x + 0.044715 * (x * x * x))))
    return x * cdf


def _softplus(x):
    return jnp.maximum(x, 0.0) + jnp.log1p(jnp.exp(-jnp.abs(x)))


LRU_CHUNK = 2 * RNN_BLOCK_W
N_LRU_CHUNKS = D_RNN // LRU_CHUNK


def _lru_lanes(n):
    return slice(n * LRU_CHUNK, (n + 1) * LRU_CHUNK)


def _lru_coeffs(xc, n, wrg_ref, brg_ref, wig_ref, big_ref, lam_ref):
    lanes = _lru_lanes(n)
    sigmoid = lambda v: 0.5 + 0.5 * jnp.tanh(0.5 * v)
    xcb = xc.astype(BF16)
    r = sigmoid(_mm(xcb, wrg_ref[n]) + brg_ref[:, lanes])
    i = sigmoid(_mm(xcb, wig_ref[n]) + big_ref[:, lanes])
    neg_rate = LRU_C * _softplus(-lam_ref[:, lanes])
    a = jnp.exp2(r * (-LOG2_E * neg_rate))
    t = jnp.tanh(r * neg_rate)
    y = 2.0 * t / (1.0 + t)
    root = jnp.where(y > 0.0, y * lax.rsqrt(y), 0.0)
    return a, root * i * xc


def _pair_blocks(w):
    z = jnp.zeros_like(w[0::2])
    top = jnp.concatenate([w[0::2], z], axis=2)
    bot = jnp.concatenate([z, w[1::2]], axis=2)
    return jnp.concatenate([top, bot], axis=1)


def _lru_tile(xb, gb, n, lru_refs, ctail_ref, hcar_ref):
    cw_ref, cb_ref, wrg_ref, brg_ref, wig_ref, big_ref, lam_ref = lru_refs
    lanes = _lru_lanes(n)
    sl = SUBLANES_V7X
    tm = xb.shape[0]
    nj = tm // sl
    sub = lax.broadcasted_iota(jnp.int32, (sl, LRU_CHUNK), 0)
    vrow = lambda t, j: t[j * sl:(j + 1) * sl]

    taps = CONV_W - 1
    wrap = [pltpu.roll(jnp.where(sub == sl - 1, vrow(ctail_ref[:, lanes], i), vrow(xb, nj - taps + i)), 1, 0)
            for i in range(taps)]
    ctail_ref[:, lanes] = xb[tm - taps * sl:]
    xc = cb_ref[:, lanes] + xb * cw_ref[taps:taps + 1, lanes]
    for k in range(1, CONV_W):
        shifted = jnp.concatenate(wrap[taps - k:] + [xb[:tm - k * sl]], axis=0)
        xc = xc + shifted * cw_ref[taps - k:taps - k + 1, lanes]

    a, b = _lru_coeffs(xc, n, wrg_ref, brg_ref, wig_ref, big_ref, lam_ref)
    gate = _gelu_tanh(gb)

    hl, acc = vrow(b, 0), vrow(a, 0)
    hls, accs = [hl], [acc]
    for j in range(1, nj):
        hl = vrow(a, j) * hl + vrow(b, j)
        acc = vrow(a, j) * acc
        hls.append(hl)
        accs.append(acc)

    for s in (1, 2, 4):
        keep = sub >= s
        acc_prev = jnp.where(keep, pltpu.roll(acc, s, 0), 1.0)
        hl_prev = jnp.where(keep, pltpu.roll(hl, s, 0), 0.0)
        hl = acc * hl_prev + hl
        acc = acc * acc_prev
    carry = hcar_ref[:, lanes]
    h_end = hl + acc * carry
    h_in = jnp.where(sub == 0, carry, pltpu.roll(h_end, 1, 0))
    hcar_ref[:, lanes] = h_end[sl - 1:sl]

    return [(hls[j] + accs[j] * h_in) * vrow(gate, j) for j in range(nj)]


def _rglru_sample_kernel(xb_ref, gb_ref, sc_ref, h0_ref, cw_ref, cb_ref, wrg_ref, brg_ref, wig_ref,
                         big_ref, lam_ref, hb_ref, conv_ref, h_ref):
    xb = xb_ref[...]
    taps = [sc_ref[:, j * D_RNN:(j + 1) * D_RNN] for j in range(CONV_W - 1)] + [xb]
    xc = cb_ref[...] + sum(t * cw_ref[j:j + 1, :] for j, t in enumerate(taps))
    for n in range(N_LRU_CHUNKS):
        lanes = _lru_lanes(n)
        a, b = _lru_coeffs(xc[:, lanes], n, wrg_ref, brg_ref, wig_ref, big_ref, lam_ref)
        h = a * h0_ref[:, lanes] + b
        h_ref[:, lanes] = h
        hb_ref[:, lanes] = (h * _gelu_tanh(gb_ref[:, lanes])).astype(hb_ref.dtype)
    for j in range(CONV_W - 1):
        conv_ref[:, j * D_RNN:(j + 1) * D_RNN] = taps[j + 1]


def _rglru_sample(xb, gb, state_conv, h0, cw, cb, wrg, brg, wig, big, lam):
    nb = xb.shape[0]
    args = (xb, gb, state_conv, h0, cw, cb, wrg, brg, wig, big, lam)
    return pl.pallas_call(
        _rglru_sample_kernel,
        grid=(1,),
        in_specs=[_const_spec(a.shape) for a in args],
        out_specs=(_const_spec((nb, D_RNN)), _const_spec((nb, (CONV_W - 1) * D_RNN)),
                   _const_spec((nb, D_RNN))),
        out_shape=(jax.ShapeDtypeStruct((nb, D_RNN), BF16),
                   jax.ShapeDtypeStruct((nb, (CONV_W - 1) * D_RNN), F32),
                   jax.ShapeDtypeStruct((nb, D_RNN), F32)),
        compiler_params=_params(("arbitrary",)),
        name="rglru_sample",
    )(*args)


def _post_kernel(*refs, dilations):
    n_parts = len(dilations)
    attn_refs = refs[:2 * n_parts] if n_parts > 1 else refs[:1]
    rest = refs[len(attn_refs):]
    (hb_ref, ga_ref, gb2_ref, x_ref, expand_ref, bg_ref, woa_ref, wor_ref, wout_ref, g2_ref,
     wfi_ref, wfo_ref, y_ref) = rest[:13]
    tm = x_ref.shape[0]

    if n_parts > 1:
        o_nat_ref, lse_nat_ref = rest[13:]

        def natural(src_ref, dst_ref, width, d):
            if d == 1:
                return src_ref[...].astype(F32)
            n_slabs = width // LANES_V7X
            for r in range(d):
                for c in range(n_slabs):
                    lanes = slice(r * width + c * LANES_V7X, r * width + (c + 1) * LANES_V7X)
                    dst_ref[c, pl.ds(r, tm // d, stride=d), :] = src_ref[:, lanes].astype(F32)
            return jnp.concatenate([dst_ref[c] for c in range(n_slabs)], axis=1)

        o_refs, stat_refs = attn_refs[:n_parts], attn_refs[n_parts:]
        stats = [natural(r, lse_nat_ref, LANES_V7X, d) for r, d in zip(stat_refs, dilations)]
        m_all = functools.reduce(jnp.maximum, stats)
        ws = [jnp.exp2(st - m_all) for st in stats]
        dens = [pltpu.roll(st, LANES_V7X - HEADS_PER_GROUP, 1) for st in stats]
        inv = 1.0 / sum(w * dn for w, dn in zip(ws, dens))
        is_head = lax.broadcasted_iota(jnp.int32, (1, LANES_V7X), 1) < HEADS_PER_GROUP
        attn = 0.0
        for o_ref, w, d in zip(o_refs, ws, dilations):
            cw = jnp.where(is_head, w * inv, 0.0)
            hi = cw.astype(BF16)
            lo = (cw - hi.astype(F32)).astype(BF16)
            wide = _mm(jnp.concatenate([hi, lo], axis=1), expand_ref[...])
            attn = attn + wide * natural(o_ref, o_nat_ref, GROUP_W, d)
    else:
        attn = attn_refs[0][...]

    ya = _mm(attn.astype(BF16), woa_ref[...])
    yb = _mm(hb_ref[...], wor_ref[...])
    merged = (jax.nn.sigmoid(ga_ref[...] + bg_ref[0:1, :]) * ya
              + jax.nn.sigmoid(gb2_ref[...] + bg_ref[1:2, :]) * yb)
    x1 = x_ref[...] + _mm(merged.astype(BF16), wout_ref[...])
    hn2 = _rms_norm_rows(x1, g2_ref[...]).astype(BF16)
    gu = _mm(hn2, wfi_ref[...])
    act = jax.nn.silu(gu[:, :D_FF]) * gu[:, D_FF:]
    y_ref[...] = x1 + _mm(act.astype(BF16), wfo_ref[...])


def _post(attn_parts, dilations, hb, ga, gb2, x, expand, bg, woa, wor, wout, g2, wfi, wfo, *, tm):
    m = x.shape[0]
    row = lambda w: pl.BlockSpec((tm, w), lambda i: (i, 0))
    n_parts = len(attn_parts)
    scratch = []
    if n_parts > 1:
        blocked = lambda w, d: pl.BlockSpec((tm // d, d * w), lambda i: (i, 0))
        attn_args = [o for o, _ in attn_parts] + [l for _, l in attn_parts]
        attn_specs = ([blocked(GROUP_W, d) for d in dilations]
                      + [blocked(LANES_V7X, d) for d in dilations])
        scratch = [pltpu.VMEM((GROUP_W // LANES_V7X, tm, LANES_V7X), F32),
                   pltpu.VMEM((1, tm, LANES_V7X), F32)]
    else:
        attn_args, attn_specs = list(attn_parts), [row(GROUP_W)]
    consts = (expand, bg, woa, wor, wout, g2, wfi, wfo)
    return pl.pallas_call(
        functools.partial(_post_kernel, dilations=dilations),
        grid=(m // tm,),
        in_specs=attn_specs + [row(D_RNN), row(D_MODEL), row(D_MODEL), row(D_MODEL)]
        + [_const_spec(c.shape) for c in consts],
        out_specs=row(D_MODEL),
        out_shape=jax.ShapeDtypeStruct((m, D_MODEL), F32),
        scratch_shapes=scratch,
        compiler_params=_params(("arbitrary",)),
        name="post_prompt" if n_parts > 1 else "post_sample",
    )(*attn_args, hb, ga, gb2, x, *consts)


def kernel(x_prompt, x_sample, cache_kv_w128, cache_kv_w512, cache_kv_w2048, state_conv, state_h,
           norm1_g, w_in, b_gate, q_norm_g, k_norm_g, conv_w, conv_b, w_rg, b_rg, w_ig, b_ig,
           lru_lambda, w_o_attn, w_o_rnn, w_out, norm2_g, w_ffn_in, w_ffn_out):
    assert x_prompt.shape[0] == 1 and norm1_g.shape[0] == 1 and x_sample.shape[1] == 1
    seq = x_prompt.shape[1]
    nb = x_sample.shape[0]
    layer = 0

    expand = (jnp.arange(LANES_V7X)[:, None] == (jnp.arange(GROUP_W) // HEAD_DIM)[None, :]).astype(BF16)
    expand = jnp.concatenate([expand, expand], axis=0)

    row2 = lambda t: t[layer].reshape(1, -1)
    g1, g2 = row2(norm1_g), row2(norm2_g)
    qg = jnp.tile(q_norm_g[layer], HEADS_PER_GROUP).reshape(1, GROUP_W) * (ATTN_SCALE * LOG2_E)
    kg = jnp.tile(k_norm_g[layer], HEADS_PER_GROUP).reshape(1, GROUP_W)
    w_in_b = w_in[layer].astype(BF16)
    wrg, wig = _pair_blocks(w_rg[layer].astype(BF16)), _pair_blocks(w_ig[layer].astype(BF16))
    lru = (conv_w[layer], row2(conv_b), wrg, row2(b_rg), wig, row2(b_ig), row2(lru_lambda))
    post_f32 = tuple(w[layer] for w in (w_o_attn, w_o_rnn, w_out, w_ffn_in, w_ffn_out))

    xs = x_sample[:, 0]
    *qkvs, xbs, gbs, gas, gb2s, kvts = _inproj(xs, g1, w_in_b, qg, kg, tm=nb, tail=nb,
                                               qkv_dtype=F32, dilations=(1,) * N_GROUPS)
    qs, ks, vs = (jnp.stack(qkvs[j::3]) for j in range(3))
    caches = [jnp.transpose(c[layer], (0, 2, 3, 4, 1)).reshape(nb, 2 * GROUP_W, c.shape[2])
              for c in (cache_kv_w128, cache_kv_w512, cache_kv_w2048)]

    xp = x_prompt[0]
    *qkv, hb, conv_rows, h_last, ga, gb2, kvt, woa, wor, wout, wfi, wfo = _inproj(
        xp, g1, w_in_b, qg, kg, lru, post_f32, tm=256, tail=MAX_WINDOW, qkv_dtype=BF16,
        dilations=DILATIONS)
    post_w = (expand, b_gate[layer], woa, wor, wout, g2, wfi, wfo)
    half = nb // 2
    parts, attn_s = [], []
    for g, d in enumerate(DILATIONS):
        sample = (qs, ks, vs, caches, (g - 1) * half) if g else None
        *part, = _attn_prompt(*qkv[3 * g:3 * g + 3], g, d, sample)
        parts.append(part[:2])
        attn_s += part[2:]
    attn_s = jnp.concatenate(attn_s, axis=0)
    assert attn_s.shape == (nb, GROUP_W)
    y_p = _post(parts, DILATIONS, hb, ga, gb2, xp, *post_w, tm=256)
    conv_p = conv_rows[SUBLANES_V7X - 1::SUBLANES_V7X]

    hbs, conv_s, h_s = _rglru_sample(xbs, gbs, state_conv[layer].reshape(nb, -1), state_h[layer], *lru)
    y_s = _post([attn_s], (1,), hbs, gas, gb2s, xs, *post_w, tm=nb)

    kv_shape = lambda rows: (1, 1, rows, 2, HEADS_PER_GROUP, HEAD_DIM)
    kv_prompt = [kvt[g, MAX_WINDOW - min(w, seq):].reshape(kv_shape(min(w, seq)))
                 for g, (w, _) in enumerate(GROUPS)]
    kv_sample = [kvts[g].reshape(1, nb, 1, 2, HEADS_PER_GROUP, HEAD_DIM) for g in range(N_GROUPS)]
    return (y_p[None], y_s[:, None],
            kv_prompt[0], kv_prompt[1], kv_prompt[2],
            conv_p[None, None], h_last[None],
            kv_sample[0], kv_sample[1], kv_sample[2],
            conv_s.reshape(1, nb, CONV_W - 1, D_RNN), h_s[None])
```

```python
import functools

import jax
import jax.numpy as jnp
from jax import lax
from jax.experimental import pallas as pl
from jax.experimental.pallas import tpu as pltpu

F32 = jnp.float32
BF16 = jnp.bfloat16

D_MODEL = 1024
HEAD_DIM = 64
HEADS_PER_GROUP = 8
GROUPS = ((128, 1), (512, 4), (2048, 16))
DILATIONS = tuple(d for _, d in GROUPS)
N_GROUPS = len(GROUPS)
GROUP_W = HEADS_PER_GROUP * HEAD_DIM
QKV_WIDTH = N_GROUPS * GROUP_W
BLOCK = 128
ATTN_SCALE = HEAD_DIM ** -0.5
LOG2_E = 1.4426950408889634
NEG_INF = -1e30
D_RNN = 1280
RNN_BLOCKS = 10
RNN_BLOCK_W = D_RNN // RNN_BLOCKS
CONV_W = 4
LRU_C = 8.0
D_FF = 2816
RMS_EPS = 1e-6

LANES_V7X = 128
SUBLANES_V7X = 8
MXU_DIM_V7X = 256
VMEM_LIMIT_BYTES = 56 * 1024 * 1024
PROMPT_ROW_TILE = MXU_DIM_V7X

OFF_Q, OFF_K, OFF_V = 0, QKV_WIDTH, 2 * QKV_WIDTH
OFF_XB = 3 * QKV_WIDTH
OFF_GB = OFF_XB + D_RNN
OFF_GA = OFF_GB + D_RNN
OFF_GB2 = OFF_GA + D_MODEL


def _mm(a, b):
    return jnp.dot(a, b, preferred_element_type=F32)


def _mm_nt(a, b):
    return lax.dot_general(a, b, (((1,), (1,)), ((), ())), preferred_element_type=F32)


def _rms_norm_rows(x, g):
    return x * lax.rsqrt(jnp.mean(x * x, axis=-1, keepdims=True) + RMS_EPS) * g


def _const_spec(shape):
    nd = len(shape)
    return pl.BlockSpec(shape, lambda *_: (0,) * nd, pipeline_mode=pl.Buffered(1))


def _params(sem):
    return pltpu.CompilerParams(dimension_semantics=sem, vmem_limit_bytes=VMEM_LIMIT_BYTES)


def _inproj_kernel(x_ref, g1_ref, w_ref, qg_ref, kg_ref, *refs, dilations, tail_first_steps,
                   fuse_lru, n_cast):
    if fuse_lru:
        lru_refs, refs = refs[:7], refs[7:]
    cast_in, refs = refs[:n_cast], refs[n_cast:]
    qkv_refs, refs = refs[:3 * N_GROUPS], refs[3 * N_GROUPS:]
    if fuse_lru:
        (hb_ref, conv_ref, hlast_ref, ga_ref, gb2_ref), refs = refs[:5], refs[5:]
    else:
        (xb_ref, gbr_ref, ga_ref, gb2_ref), refs = refs[:4], refs[4:]
    kvt_refs, refs = refs[:N_GROUPS], refs[N_GROUPS:]
    cast_out, refs = refs[:n_cast], refs[n_cast:]
    if fuse_lru:
        hn_ref, hbs_ref, hnp_ref, ctail_ref, hcar_ref = refs
    else:
        (hn_ref,) = refs
    for src_ref, dst_ref in zip(cast_in, cast_out):
        dst_ref[...] = src_ref[...].astype(dst_ref.dtype)
    step = pl.program_id(0)
    tm = x_ref.shape[0]
    hn32 = _rms_norm_rows(x_ref[...], g1_ref[...])
    hn = hn32.astype(BF16)
    n_slabs = D_MODEL // LANES_V7X
    for c in range(n_slabs):
        hn_ref[c] = hn32[:, c * LANES_V7X:(c + 1) * LANES_V7X]

    def strided_rows(start, size, stride):
        return jnp.concatenate([hn_ref[c, pl.ds(start, size, stride=stride), :] for c in range(n_slabs)],
                               axis=1)

    if fuse_lru:
        @pl.when(step == 0)
        def _():
            ctail_ref[...] = jnp.zeros(ctail_ref.shape, F32)
            hcar_ref[...] = jnp.zeros(hcar_ref.shape, F32)

        nj = tm // SUBLANES_V7X
        pitch = nj + SUBLANES_V7X
        for s in range(SUBLANES_V7X):
            for c in range(n_slabs):
                hnp_ref[c, s * pitch:s * pitch + nj, :] = hn32[s * nj:(s + 1) * nj, c * LANES_V7X:(c + 1) * LANES_V7X]
        ht = jnp.concatenate(
            [jnp.concatenate([hnp_ref[c, pl.ds(j, SUBLANES_V7X, stride=pitch), :] for c in range(n_slabs)],
                             axis=1) for j in range(nj)], axis=0).astype(BF16)
        per_vreg = LRU_CHUNK // LANES_V7X
        for n in range(N_LRU_CHUNKS):
            c0 = n * LRU_CHUNK
            hb_rows = _lru_tile(_mm(ht, w_ref[:, OFF_XB + c0:OFF_XB + c0 + LRU_CHUNK]),
                                _mm(ht, w_ref[:, OFF_GB + c0:OFF_GB + c0 + LRU_CHUNK]),
                                n, lru_refs, ctail_ref, hcar_ref)
            for j, rows in enumerate(hb_rows):
                for c in range(per_vreg):
                    hbs_ref[n * per_vreg + c, pl.ds(j, SUBLANES_V7X, stride=pitch), :] = (
                        rows[:, c * LANES_V7X:(c + 1) * LANES_V7X])
        conv_ref[...] = ctail_ref[...]
        hlast_ref[...] = hcar_ref[...]
        hb_ref[...] = jnp.concatenate(
            [jnp.concatenate([hbs_ref[c, s * pitch:s * pitch + nj, :] for s in range(SUBLANES_V7X)], axis=0)
             for c in range(D_RNN // LANES_V7X)], axis=1).astype(hb_ref.dtype)
    else:
        xb_ref[...] = _mm(hn, w_ref[:, OFF_XB:OFF_XB + D_RNN])
        gbr_ref[...] = _mm(hn, w_ref[:, OFF_GB:OFF_GB + D_RNN])

    def head_norm(t, gain):
        first_head = lax.broadcasted_iota(jnp.int32, (1, LANES_V7X), 1) < HEAD_DIM
        cols = []
        for c in range(0, GROUP_W, LANES_V7X):
            x = t[:, c:c + LANES_V7X]
            xx = x * x
            s0 = jnp.sum(jnp.where(first_head, xx, 0.0), axis=-1, keepdims=True)
            s1 = jnp.sum(jnp.where(first_head, 0.0, xx), axis=-1, keepdims=True)
            ms = jnp.where(first_head, s0, s1) * (1.0 / HEAD_DIM)
            cols.append(x * lax.rsqrt(ms + RMS_EPS))
        return jnp.concatenate(cols, axis=1) * gain

    def qkv(h, g):
        c = g * GROUP_W
        qn = head_norm(_mm(h, w_ref[:, OFF_Q + c:OFF_Q + c + GROUP_W]), qg_ref[...])
        kn = head_norm(_mm(h, w_ref[:, OFF_K + c:OFF_K + c + GROUP_W]), kg_ref[...])
        vv = _mm(h, w_ref[:, OFF_V + c:OFF_V + c + GROUP_W])
        return qn, kn, vv

    in_order = {}
    for g, d in enumerate(dilations):
        rows = tm // d
        if d == 1:
            hg = hn
        else:
            hg = jnp.concatenate([strided_rows(r, rows, d) for r in range(d)], axis=0).astype(BF16)
        parts = qkv(hg, g)
        if d == 1:
            in_order[g] = parts
        for t, o_ref in zip(parts, qkv_refs[3 * g:3 * g + 3]):
            for r in range(d):
                o_ref[:, r * GROUP_W:(r + 1) * GROUP_W] = t[r * rows:(r + 1) * rows].astype(o_ref.dtype)

    ga_ref[...] = _mm(hn, w_ref[:, OFF_GA:OFF_GA + D_MODEL])
    gb2_ref[...] = _mm(hn, w_ref[:, OFF_GB2:OFF_GB2 + D_MODEL])

    for g, (kvt_ref, first_step) in enumerate(zip(kvt_refs, tail_first_steps)):
        @pl.when(step >= first_step)
        def _():
            _, kn, vv = in_order[g] if g in in_order else qkv(hn, g)
            keep = kvt_ref.shape[0]
            kvt_ref[:, 0:GROUP_W] = kn[tm - keep:]
            kvt_ref[:, GROUP_W:2 * GROUP_W] = vv[tm - keep:]


def _inproj(x, g1, w_in, qg, kg, lru=None, to_bf16=(), *, tm, tails, qkv_dtype, dilations):
    m = x.shape[0]
    nt = m // tm
    row = lambda w: pl.BlockSpec((tm, w), lambda i: (i, 0))
    fixed = lambda r, w: pl.BlockSpec((r, w), lambda i: (0, 0))
    qkv_specs, qkv_shapes = [], []
    for d in dilations:
        qkv_specs += [pl.BlockSpec((tm // d, d * GROUP_W), lambda i: (i, 0))] * 3
        qkv_shapes += [jax.ShapeDtypeStruct((m // d, d * GROUP_W), qkv_dtype)] * 3
    kvt_specs, kvt_shapes, tail_first_steps = [], [], []
    for rows in tails:
        blk = min(rows, tm)
        first = nt - rows // blk
        kvt_specs.append(pl.BlockSpec((blk, 2 * GROUP_W), lambda i, first=first: (jnp.maximum(i - first, 0), 0)))
        kvt_shapes.append(jax.ShapeDtypeStruct((rows, 2 * GROUP_W), F32))
        tail_first_steps.append(first)
    scratch = [pltpu.VMEM((D_MODEL // LANES_V7X, tm, LANES_V7X), F32)]
    tail_rows = (CONV_W - 1) * SUBLANES_V7X
    if lru is None:
        lru = ()
        rnn_specs = (row(D_RNN), row(D_RNN))
        rnn_shapes = (jax.ShapeDtypeStruct((m, D_RNN), F32),) * 2
    else:
        rnn_specs = (row(D_RNN), fixed(tail_rows, D_RNN), fixed(1, D_RNN))
        rnn_shapes = (jax.ShapeDtypeStruct((m, D_RNN), BF16),
                      jax.ShapeDtypeStruct((tail_rows, D_RNN), F32),
                      jax.ShapeDtypeStruct((1, D_RNN), F32))
        padded = tm + SUBLANES_V7X * SUBLANES_V7X
        scratch += [pltpu.VMEM((D_RNN // LANES_V7X, padded, LANES_V7X), F32),
                    pltpu.VMEM((D_MODEL // LANES_V7X, padded, LANES_V7X), F32),
                    pltpu.VMEM((tail_rows, D_RNN), F32),
                    pltpu.VMEM((1, D_RNN), F32)]
    out_shape = tuple(qkv_shapes) + rnn_shapes + (
        jax.ShapeDtypeStruct((m, D_MODEL), F32),
        jax.ShapeDtypeStruct((m, D_MODEL), F32),
    ) + tuple(kvt_shapes)
    consts = (g1, w_in, qg, kg) + tuple(lru)
    cast_specs, cast_shapes = [], []
    bf16_rows = 2 * SUBLANES_V7X
    for w in to_bf16:
        nblk = nt
        while w.shape[0] % (nblk * bf16_rows):
            nblk //= 2
        cast_specs.append(pl.BlockSpec((w.shape[0] // nblk, w.shape[1]),
                                       lambda i, per=nt // nblk: (i // per, 0)))
        cast_shapes.append(jax.ShapeDtypeStruct(w.shape, BF16))
    return pl.pallas_call(
        functools.partial(_inproj_kernel, dilations=dilations, tail_first_steps=tuple(tail_first_steps),
                          fuse_lru=bool(lru), n_cast=len(to_bf16)),
        grid=(nt,),
        in_specs=[row(D_MODEL)] + [_const_spec(c.shape) for c in consts] + cast_specs,
        out_specs=(tuple(qkv_specs) + rnn_specs + (row(D_MODEL), row(D_MODEL)) + tuple(kvt_specs)
                   + tuple(cast_specs)),
        out_shape=out_shape + tuple(cast_shapes),
        scratch_shapes=scratch,
        compiler_params=_params(("arbitrary",)),
        name="inproj_lru" if lru else "inproj",
    )(x, *consts, *to_bf16)


ATTN_BLOCKS_PER_STEP = 8


def _attn_prompt_kernel(q_ref, kc_ref, kp_ref, vc_ref, vp_ref, *refs, sample_first_row):
    if sample_first_row is None:
        o_ref, lse_ref = refs
    else:
        qs_ref, ks_ref, vs_ref, c0_ref, c1_ref, c2_ref, o_ref, lse_ref, os_ref = refs
        flat_step = pl.program_id(0) * pl.num_programs(1) + pl.program_id(1)
        _attn_sample_row(flat_step, sample_first_row + flat_step, qs_ref, ks_ref, vs_ref,
                         (c0_ref, c1_ref, c2_ref), os_ref)
    step = pl.program_id(1)
    qi = lax.broadcasted_iota(jnp.int32, (BLOCK, 2 * BLOCK), 0)
    kj = lax.broadcasted_iota(jnp.int32, (BLOCK, 2 * BLOCK), 1)
    dist = BLOCK + qi - kj
    band = (dist >= 0) & (dist <= BLOCK)
    first_band = band & ((step > 0) | (kj >= BLOCK))
    band2 = jnp.concatenate([band, band], axis=0)
    first_band2 = jnp.concatenate([first_band, first_band], axis=0)
    first_head = lax.broadcasted_iota(jnp.int32, (1, LANES_V7X), 1) < HEAD_DIM
    zero = jnp.zeros((), BF16)

    for blk in range(q_ref.shape[0] // BLOCK):
        rows = slice(blk * BLOCK, (blk + 1) * BLOCK)
        prev_rows = slice((blk - 1) * BLOCK, blk * BLOCK)
        valid = band2 if blk else first_band2
        lse_ref[rows, :] = jnp.zeros((BLOCK, LANES_V7X), F32)
        for pair in range(HEADS_PER_GROUP // 2):
            cols = slice(pair * LANES_V7X, (pair + 1) * LANES_V7X)
            qp = q_ref[rows, cols]
            k_prev = kc_ref[prev_rows, cols] if blk else kp_ref[:, cols]
            v_prev = vc_ref[prev_rows, cols] if blk else vp_ref[:, cols]
            kk = jnp.concatenate([k_prev, kc_ref[rows, cols]], axis=0)
            vv = jnp.concatenate([v_prev, vc_ref[rows, cols]], axis=0)
            q2 = jnp.concatenate([jnp.where(first_head, qp, zero), jnp.where(first_head, zero, qp)], axis=0)
            s = jnp.where(valid, _mm_nt(q2, kk), NEG_INF)
            mx = jnp.max(s, axis=-1, keepdims=True)
            p = jnp.exp2(s - mx)
            den = jnp.sum(p, axis=-1, keepdims=True)
            pv = _mm(p.astype(BF16), vv)
            o_ref[rows, cols] = jnp.where(first_head, pv[:BLOCK], pv[BLOCK:]).astype(o_ref.dtype)
            for e, (lo, hi) in enumerate(((0, BLOCK), (BLOCK, 2 * BLOCK))):
                head = 2 * pair + e
                lse_ref[rows, head:head + 1] = mx[lo:hi]
                lse_ref[rows, HEADS_PER_GROUP + head:HEADS_PER_GROUP + head + 1] = den[lo:hi]


def _attn_prompt(q, k, v, g, dilation, sample=None):
    m_len = q.shape[0]
    nbs = ATTN_BLOCKS_PER_STEP
    rows = nbs * BLOCK
    n_inner = m_len // rows
    cur = pl.BlockSpec((rows, GROUP_W), lambda r, b: (b, r))
    prev = pl.BlockSpec((BLOCK, GROUP_W), lambda r, b: (jnp.maximum(b * nbs - 1, 0), r))
    in_specs, args = [cur, cur, prev, cur, prev], [q, k, k, v, v]
    out_specs = [pl.BlockSpec((rows, GROUP_W), lambda r, b: (b, r)),
                 pl.BlockSpec((rows, LANES_V7X), lambda r, b: (b, r))]
    out_shape = [jax.ShapeDtypeStruct((m_len, dilation * GROUP_W), BF16),
                 jax.ShapeDtypeStruct((m_len, dilation * LANES_V7X), F32)]
    first_row = None
    if sample is not None:
        qs, ks, vs, caches, first_row = sample
        steps = dilation * n_inner
        full = pl.BlockSpec(qs.shape, lambda r, b: (0, 0, 0))
        in_specs += [full, full, full] + [
            pl.BlockSpec((None,) + c.shape[1:], lambda r, b: (first_row + r * n_inner + b, 0, 0))
            for c in caches]
        args += [qs, ks, vs, *caches]
        out_specs.append(pl.BlockSpec((steps, GROUP_W), lambda r, b: (0, 0)))
        out_shape.append(jax.ShapeDtypeStruct((steps, GROUP_W), F32))
    return pl.pallas_call(
        functools.partial(_attn_prompt_kernel, sample_first_row=first_row),
        grid=(dilation, n_inner),
        in_specs=in_specs,
        out_specs=tuple(out_specs),
        out_shape=tuple(out_shape),
        compiler_params=_params(("arbitrary", "arbitrary")),
        name=f"attn_prompt_g{g}",
    )(*args)


def _attn_sample_row(out_row, b, q_ref, k_ref, v_ref, cache_refs, o_ref):
    head_row = lax.broadcasted_iota(jnp.int32, (HEADS_PER_GROUP, GROUP_W), 0)
    head_lane = lax.broadcasted_iota(jnp.int32, (HEADS_PER_GROUP, GROUP_W), 1) // HEAD_DIM
    own = head_row == head_lane
    parts = []
    for g, c_ref in enumerate(cache_refs):
        window, dilation = GROUPS[g]
        bf = lambda t: t.astype(BF16).astype(F32)
        qmat = jnp.where(own, jnp.broadcast_to(q_ref[g, pl.ds(b, 1), :], (HEADS_PER_GROUP, GROUP_W)), 0.0)
        qmat = qmat.astype(BF16)
        knew = bf(k_ref[g, pl.ds(b, 1), :])
        vnew = bf(v_ref[g, pl.ds(b, 1), :])
        pos = lax.broadcasted_iota(jnp.int32, (HEADS_PER_GROUP, window), 1)
        s = _mm(qmat, c_ref[0:GROUP_W, :].astype(BF16))
        s = jnp.where((pos & (dilation - 1)) == 0, s, NEG_INF)
        s_new = jnp.sum(qmat.astype(F32) * knew, axis=-1, keepdims=True)
        mx = jnp.maximum(jnp.max(s, axis=-1, keepdims=True), s_new)
        p = jnp.exp2(s - mx)
        p_new = jnp.exp2(s_new - mx)
        den = jnp.sum(p, axis=-1, keepdims=True) + p_new
        numer = _mm_nt(p.astype(BF16), c_ref[GROUP_W:2 * GROUP_W, :].astype(BF16)) + bf(p_new) * vnew
        parts.append((jnp.where(own, numer, 0.0), mx, den))
    m_all = functools.reduce(jnp.maximum, [mx for _, mx, _ in parts])
    ws = [jnp.exp2(mx - m_all) for _, mx, _ in parts]
    num = sum(n * w for (n, _, _), w in zip(parts, ws))
    den = sum(d * w for (_, _, d), w in zip(parts, ws))
    o_ref[pl.ds(out_row, 1), :] = jnp.sum(num / den, axis=0, keepdims=True)


def _gelu_tanh(x):
    cdf = 0.5 * (1.0 + jnp.tanh(0.7978845608028654 * (x + 0.044715 * (x * x * x))))
    return x * cdf


def _softplus(x):
    return jnp.maximum(x, 0.0) + jnp.log1p(jnp.exp(-jnp.abs(x)))


LRU_CHUNK = 2 * RNN_BLOCK_W
N_LRU_CHUNKS = D_RNN // LRU_CHUNK


def _lru_lanes(n):
    return slice(n * LRU_CHUNK, (n + 1) * LRU_CHUNK)


def _lru_coeffs(xc, n, wrg_ref, brg_ref, wig_ref, big_ref, lam_ref):
    lanes = _lru_lanes(n)
    sigmoid = lambda v: 0.5 + 0.5 * jnp.tanh(0.5 * v)
    xcb = xc.astype(BF16)
    r = sigmoid(_mm(xcb, wrg_ref[n]) + brg_ref[:, lanes])
    i = sigmoid(_mm(xcb, wig_ref[n]) + big_ref[:, lanes])
    neg_rate = LRU_C * _softplus(-lam_ref[:, lanes])
    a = jnp.exp2(r * (-LOG2_E * neg_rate))
    t = jnp.tanh(r * neg_rate)
    y = 2.0 * t / (1.0 + t)
    root = jnp.where(y > 0.0, y * lax.rsqrt(y), 0.0)
    return a, root * i * xc


def _pair_blocks(w):
    z = jnp.zeros_like(w[0::2])
    top = jnp.concatenate([w[0::2], z], axis=2)
    bot = jnp.concatenate([z, w[1::2]], axis=2)
    return jnp.concatenate([top, bot], axis=1)


def _lru_tile(xb, gb, n, lru_refs, ctail_ref, hcar_ref):
    cw_ref, cb_ref, wrg_ref, brg_ref, wig_ref, big_ref, lam_ref = lru_refs
    lanes = _lru_lanes(n)
    sl = SUBLANES_V7X
    tm = xb.shape[0]
    nj = tm // sl
    sub = lax.broadcasted_iota(jnp.int32, (sl, LRU_CHUNK), 0)
    vrow = lambda t, j: t[j * sl:(j + 1) * sl]

    taps = CONV_W - 1
    wrap = [pltpu.roll(jnp.where(sub == sl - 1, vrow(ctail_ref[:, lanes], i), vrow(xb, nj - taps + i)), 1, 0)
            for i in range(taps)]
    ctail_ref[:, lanes] = xb[tm - taps * sl:]
    xc = cb_ref[:, lanes] + xb * cw_ref[taps:taps + 1, lanes]
    for k in range(1, CONV_W):
        shifted = jnp.concatenate(wrap[taps - k:] + [xb[:tm - k * sl]], axis=0)
        xc = xc + shifted * cw_ref[taps - k:taps - k + 1, lanes]

    a, b = _lru_coeffs(xc, n, wrg_ref, brg_ref, wig_ref, big_ref, lam_ref)
    gate = _gelu_tanh(gb)

    hl, acc = vrow(b, 0), vrow(a, 0)
    hls, accs = [hl], [acc]
    for j in range(1, nj):
        hl = vrow(a, j) * hl + vrow(b, j)
        acc = vrow(a, j) * acc
        hls.append(hl)
        accs.append(acc)

    for s in (1, 2, 4):
        keep = sub >= s
        acc_prev = jnp.where(keep, pltpu.roll(acc, s, 0), 1.0)
        hl_prev = jnp.where(keep, pltpu.roll(hl, s, 0), 0.0)
        hl = acc * hl_prev + hl
        acc = acc * acc_prev
    carry = hcar_ref[:, lanes]
    h_end = hl + acc * carry
    h_in = jnp.where(sub == 0, carry, pltpu.roll(h_end, 1, 0))
    hcar_ref[:, lanes] = h_end[sl - 1:sl]

    return [(hls[j] + accs[j] * h_in) * vrow(gate, j) for j in range(nj)]


def _rglru_sample_kernel(xb_ref, gb_ref, sc_ref, h0_ref, cw_ref, cb_ref, wrg_ref, brg_ref, wig_ref,
                         big_ref, lam_ref, hb_ref, conv_ref, h_ref):
    xb = xb_ref[...]
    taps = [sc_ref[:, j * D_RNN:(j + 1) * D_RNN] for j in range(CONV_W - 1)] + [xb]
    xc = cb_ref[...] + sum(t * cw_ref[j:j + 1, :] for j, t in enumerate(taps))
    for n in range(N_LRU_CHUNKS):
        lanes = _lru_lanes(n)
        a, b = _lru_coeffs(xc[:, lanes], n, wrg_ref, brg_ref, wig_ref, big_ref, lam_ref)
        h = a * h0_ref[:, lanes] + b
        h_ref[:, lanes] = h
        hb_ref[:, lanes] = (h * _gelu_tanh(gb_ref[:, lanes])).astype(hb_ref.dtype)
    for j in range(CONV_W - 1):
        conv_ref[:, j * D_RNN:(j + 1) * D_RNN] = taps[j + 1]


def _rglru_sample(xb, gb, state_conv, h0, cw, cb, wrg, brg, wig, big, lam):
    nb = xb.shape[0]
    args = (xb, gb, state_conv, h0, cw, cb, wrg, brg, wig, big, lam)
    return pl.pallas_call(
        _rglru_sample_kernel,
        grid=(1,),
        in_specs=[_const_spec(a.shape) for a in args],
        out_specs=(_const_spec((nb, D_RNN)), _const_spec((nb, (CONV_W - 1) * D_RNN)),
                   _const_spec((nb, D_RNN))),
        out_shape=(jax.ShapeDtypeStruct((nb, D_RNN), BF16),
                   jax.ShapeDtypeStruct((nb, (CONV_W - 1) * D_RNN), F32),
                   jax.ShapeDtypeStruct((nb, D_RNN), F32)),
        compiler_params=_params(("arbitrary",)),
        name="rglru_sample",
    )(*args)


def _post_kernel(*refs, dilations):
    n_parts = len(dilations)
    attn_refs = refs[:2 * n_parts] if n_parts > 1 else refs[:1]
    rest = refs[len(attn_refs):]
    (hb_ref, ga_ref, gb2_ref, x_ref, expand_ref, bg_ref, woa_ref, wor_ref, wout_ref, g2_ref,
     wfi_ref, wfo_ref, y_ref) = rest[:13]
    tm = x_ref.shape[0]

    if n_parts > 1:
        o_nat_ref, lse_nat_ref = rest[13:]

        def natural(src_ref, dst_ref, width, d):
            if d == 1:
                return src_ref[...].astype(F32)
            n_slabs = width // LANES_V7X
            for r in range(d):
                for c in range(n_slabs):
                    lanes = slice(r * width + c * LANES_V7X, r * width + (c + 1) * LANES_V7X)
                    dst_ref[c, pl.ds(r, tm // d, stride=d), :] = src_ref[:, lanes].astype(F32)
            return jnp.concatenate([dst_ref[c] for c in range(n_slabs)], axis=1)

        o_refs, stat_refs = attn_refs[:n_parts], attn_refs[n_parts:]
        stats = [natural(r, lse_nat_ref, LANES_V7X, d) for r, d in zip(stat_refs, dilations)]
        m_all = functools.reduce(jnp.maximum, stats)
        ws = [jnp.exp2(st - m_all) for st in stats]
        dens = [pltpu.roll(st, LANES_V7X - HEADS_PER_GROUP, 1) for st in stats]
        inv = 1.0 / sum(w * dn for w, dn in zip(ws, dens))
        is_head = lax.broadcasted_iota(jnp.int32, (1, LANES_V7X), 1) < HEADS_PER_GROUP
        attn = 0.0
        for o_ref, w, d in zip(o_refs, ws, dilations):
            cw = jnp.where(is_head, w * inv, 0.0)
            hi = cw.astype(BF16)
            lo = (cw - hi.astype(F32)).astype(BF16)
            wide = _mm(jnp.concatenate([hi, lo], axis=1), expand_ref[...])
            attn = attn + wide * natural(o_ref, o_nat_ref, GROUP_W, d)
    else:
        attn = attn_refs[0][...]

    ya = _mm(attn.astype(BF16), woa_ref[...])
    yb = _mm(hb_ref[...], wor_ref[...])
    merged = (jax.nn.sigmoid(ga_ref[...] + bg_ref[0:1, :]) * ya
              + jax.nn.sigmoid(gb2_ref[...] + bg_ref[1:2, :]) * yb)
    x1 = x_ref[...] + _mm(merged.astype(BF16), wout_ref[...])
    hn2 = _rms_norm_rows(x1, g2_ref[...]).astype(BF16)
    gu = _mm(hn2, wfi_ref[...])
    act = jax.nn.silu(gu[:, :D_FF]) * gu[:, D_FF:]
    y_ref[...] = x1 + _mm(act.astype(BF16), wfo_ref[...])


def _post(attn_parts, dilations, hb, ga, gb2, x, expand, bg, woa, wor, wout, g2, wfi, wfo, *, tm):
    m = x.shape[0]
    row = lambda w: pl.BlockSpec((tm, w), lambda i: (i, 0))
    n_parts = len(attn_parts)
    scratch = []
    if n_parts > 1:
        blocked = lambda w, d: pl.BlockSpec((tm // d, d * w), lambda i: (i, 0))
        attn_args = [o for o, _ in attn_parts] + [l for _, l in attn_parts]
        attn_specs = ([blocked(GROUP_W, d) for d in dilations]
                      + [blocked(LANES_V7X, d) for d in dilations])
        scratch = [pltpu.VMEM((GROUP_W // LANES_V7X, tm, LANES_V7X), F32),
                   pltpu.VMEM((1, tm, LANES_V7X), F32)]
    else:
        attn_args, attn_specs = list(attn_parts), [row(GROUP_W)]
    consts = (expand, bg, woa, wor, wout, g2, wfi, wfo)
    return pl.pallas_call(
        functools.partial(_post_kernel, dilations=dilations),
        grid=(m // tm,),
        in_specs=attn_specs + [row(D_RNN), row(D_MODEL), row(D_MODEL), row(D_MODEL)]
        + [_const_spec(c.shape) for c in consts],
        out_specs=row(D_MODEL),
        out_shape=jax.ShapeDtypeStruct((m, D_MODEL), F32),
        scratch_shapes=scratch,
        compiler_params=_params(("arbitrary",)),
        name="post_prompt" if n_parts > 1 else "post_sample",
    )(*attn_args, hb, ga, gb2, x, *consts)


def kernel(x_prompt, x_sample, cache_kv_w128, cache_kv_w512, cache_kv_w2048, state_conv, state_h,
           norm1_g, w_in, b_gate, q_norm_g, k_norm_g, conv_w, conv_b, w_rg, b_rg, w_ig, b_ig,
           lru_lambda, w_o_attn, w_o_rnn, w_out, norm2_g, w_ffn_in, w_ffn_out):
    assert x_prompt.shape[0] == 1 and norm1_g.shape[0] == 1 and x_sample.shape[1] == 1
    seq = x_prompt.shape[1]
    nb = x_sample.shape[0]
    layer = 0

    expand = (jnp.arange(LANES_V7X)[:, None] == (jnp.arange(GROUP_W) // HEAD_DIM)[None, :]).astype(BF16)
    expand = jnp.concatenate([expand, expand], axis=0)

    row2 = lambda t: t[layer].reshape(1, -1)
    g1, g2 = row2(norm1_g), row2(norm2_g)
    qg = jnp.tile(q_norm_g[layer], HEADS_PER_GROUP).reshape(1, GROUP_W) * (ATTN_SCALE * LOG2_E)
    kg = jnp.tile(k_norm_g[layer], HEADS_PER_GROUP).reshape(1, GROUP_W)
    w_in_b = w_in[layer].astype(BF16)
    wrg, wig = _pair_blocks(w_rg[layer].astype(BF16)), _pair_blocks(w_ig[layer].astype(BF16))
    lru = (conv_w[layer], row2(conv_b), wrg, row2(b_rg), wig, row2(b_ig), row2(lru_lambda))
    post_f32 = tuple(w[layer] for w in (w_o_attn, w_o_rnn, w_out, w_ffn_in, w_ffn_out))

    xs = x_sample[:, 0]
    *qkvs, xbs, gbs, gas, gb2s, kvs0, kvs1, kvs2 = _inproj(
        xs, g1, w_in_b, qg, kg, tm=nb, tails=(nb,) * N_GROUPS, qkv_dtype=F32, dilations=(1,) * N_GROUPS)
    qs, ks, vs = (jnp.stack(qkvs[j::3]) for j in range(3))
    caches = [jnp.transpose(c[layer], (0, 2, 3, 4, 1)).reshape(nb, 2 * GROUP_W, c.shape[2])
              for c in (cache_kv_w128, cache_kv_w512, cache_kv_w2048)]

    xp = x_prompt[0]
    *qkv, hb, conv_rows, h_last, ga, gb2, kvp0, kvp1, kvp2, woa, wor, wout, wfi, wfo = _inproj(
        xp, g1, w_in_b, qg, kg, lru, post_f32, tm=PROMPT_ROW_TILE, tails=tuple(min(w, seq) for w, _ in GROUPS),
        qkv_dtype=BF16, dilations=DILATIONS)
    post_w = (expand, b_gate[layer], woa, wor, wout, g2, wfi, wfo)
    half = nb // 2
    parts, attn_s = [], []
    for g, d in enumerate(DILATIONS):
        sample = (qs, ks, vs, caches, (g - 1) * half) if g else None
        *part, = _attn_prompt(*qkv[3 * g:3 * g + 3], g, d, sample)
        parts.append(part[:2])
        attn_s += part[2:]
    attn_s = jnp.concatenate(attn_s, axis=0)
    assert attn_s.shape == (nb, GROUP_W)
    y_p = _post(parts, DILATIONS, hb, ga, gb2, xp, *post_w, tm=PROMPT_ROW_TILE)
    conv_p = conv_rows[SUBLANES_V7X - 1::SUBLANES_V7X]

    hbs, conv_s, h_s = _rglru_sample(xbs, gbs, state_conv[layer].reshape(nb, -1), state_h[layer], *lru)
    y_s = _post([attn_s], (1,), hbs, gas, gb2s, xs, *post_w, tm=nb)

    kv_prompt = [t.reshape(1, 1, t.shape[0], 2, HEADS_PER_GROUP, HEAD_DIM) for t in (kvp0, kvp1, kvp2)]
    kv_sample = [t.reshape(1, nb, 1, 2, HEADS_PER_GROUP, HEAD_DIM) for t in (kvs0, kvs1, kvs2)]
    return (y_p[None], y_s[:, None],
            kv_prompt[0], kv_prompt[1], kv_prompt[2],
            conv_p[None, None], h_last[None],
            kv_sample[0], kv_sample[1], kv_sample[2],
            conv_s.reshape(1, nb, CONV_W - 1, D_RNN), h_s[None])
```

```python
import functools

import jax
import jax.numpy as jnp
from jax import lax
from jax.experimental import pallas as pl
from jax.experimental.pallas import tpu as pltpu

F32 = jnp.float32
BF16 = jnp.bfloat16

D_MODEL = 1024
HEAD_DIM = 64
HEADS_PER_GROUP = 8
GROUPS = ((128, 1), (512, 4), (2048, 16))
DILATIONS = tuple(d for _, d in GROUPS)
N_GROUPS = len(GROUPS)
GROUP_W = HEADS_PER_GROUP * HEAD_DIM
QKV_WIDTH = N_GROUPS * GROUP_W
BLOCK = 128
ATTN_SCALE = HEAD_DIM ** -0.5
LOG2_E = 1.4426950408889634
NEG_INF = -1e30
D_RNN = 1280
RNN_BLOCKS = 10
RNN_BLOCK_W = D_RNN // RNN_BLOCKS
CONV_W = 4
LRU_C = 8.0
D_FF = 2816
RMS_EPS = 1e-6

LANES_V7X = 128
SUBLANES_V7X = 8
MXU_DIM_V7X = 256
VMEM_LIMIT_BYTES = 56 * 1024 * 1024
PROMPT_ROW_TILE = MXU_DIM_V7X

OFF_Q, OFF_K, OFF_V = 0, QKV_WIDTH, 2 * QKV_WIDTH
OFF_XB = 3 * QKV_WIDTH
OFF_GB = OFF_XB + D_RNN
OFF_GA = OFF_GB + D_RNN
OFF_GB2 = OFF_GA + D_MODEL


def _mm(a, b):
    return jnp.dot(a, b, preferred_element_type=F32)


def _mm_nt(a, b):
    return lax.dot_general(a, b, (((1,), (1,)), ((), ())), preferred_element_type=F32)


def _rms_norm_rows(x, g):
    return x * lax.rsqrt(jnp.mean(x * x, axis=-1, keepdims=True) + RMS_EPS) * g


def _head_rms_norm(t):
    first_head = lax.broadcasted_iota(jnp.int32, (1, LANES_V7X), 1) < HEAD_DIM
    cols = []
    for c in range(0, t.shape[1], LANES_V7X):
        x = t[:, c:c + LANES_V7X]
        xx = x * x
        s0 = jnp.sum(jnp.where(first_head, xx, 0.0), axis=-1, keepdims=True)
        s1 = jnp.sum(jnp.where(first_head, 0.0, xx), axis=-1, keepdims=True)
        ms = jnp.where(first_head, s0, s1) * (1.0 / HEAD_DIM)
        cols.append(x * lax.rsqrt(ms + RMS_EPS))
    return jnp.concatenate(cols, axis=1)


def _const_spec(shape):
    nd = len(shape)
    return pl.BlockSpec(shape, lambda *_: (0,) * nd, pipeline_mode=pl.Buffered(1))


def _params(sem):
    return pltpu.CompilerParams(dimension_semantics=sem, vmem_limit_bytes=VMEM_LIMIT_BYTES)


def _inproj_kernel(x_ref, g1_ref, w_ref, qg_ref, kg_ref, *refs, dilations, tail_first_steps, n_cast):
    lru_refs, refs = refs[:7], refs[7:]
    cast_in, refs = refs[:n_cast], refs[n_cast:]
    qkv_refs, refs = refs[:3 * N_GROUPS], refs[3 * N_GROUPS:]
    (hb_ref, conv_ref, hlast_ref, ga_ref, gb2_ref), refs = refs[:5], refs[5:]
    kvt_refs, refs = refs[:N_GROUPS], refs[N_GROUPS:]
    cast_out, refs = refs[:n_cast], refs[n_cast:]
    hn_ref, hbs_ref, hnp_ref, ctail_ref, hcar_ref = refs
    for src_ref, dst_ref in zip(cast_in, cast_out):
        dst_ref[...] = src_ref[...].astype(dst_ref.dtype)
    step = pl.program_id(0)
    tm = x_ref.shape[0]
    hn32 = _rms_norm_rows(x_ref[...], g1_ref[...])
    hn = hn32.astype(BF16)
    n_slabs = D_MODEL // LANES_V7X
    for c in range(n_slabs):
        hn_ref[c] = hn32[:, c * LANES_V7X:(c + 1) * LANES_V7X]

    def strided_rows(start, size, stride):
        return jnp.concatenate([hn_ref[c, pl.ds(start, size, stride=stride), :] for c in range(n_slabs)],
                               axis=1)

    @pl.when(step == 0)
    def _():
        ctail_ref[...] = jnp.zeros(ctail_ref.shape, F32)
        hcar_ref[...] = jnp.zeros(hcar_ref.shape, F32)

    nj = tm // SUBLANES_V7X
    pitch = nj + SUBLANES_V7X
    for s in range(SUBLANES_V7X):
        for c in range(n_slabs):
            hnp_ref[c, s * pitch:s * pitch + nj, :] = hn32[s * nj:(s + 1) * nj, c * LANES_V7X:(c + 1) * LANES_V7X]
    ht = jnp.concatenate(
        [jnp.concatenate([hnp_ref[c, pl.ds(j, SUBLANES_V7X, stride=pitch), :] for c in range(n_slabs)],
                         axis=1) for j in range(nj)], axis=0).astype(BF16)
    per_vreg = LRU_CHUNK // LANES_V7X
    for n in range(N_LRU_CHUNKS):
        c0 = n * LRU_CHUNK
        hb_rows = _lru_tile(_mm(ht, w_ref[:, OFF_XB + c0:OFF_XB + c0 + LRU_CHUNK]),
                            _mm(ht, w_ref[:, OFF_GB + c0:OFF_GB + c0 + LRU_CHUNK]),
                            n, lru_refs, ctail_ref, hcar_ref)
        for j, rows in enumerate(hb_rows):
            for c in range(per_vreg):
                hbs_ref[n * per_vreg + c, pl.ds(j, SUBLANES_V7X, stride=pitch), :] = (
                    rows[:, c * LANES_V7X:(c + 1) * LANES_V7X])
    conv_ref[...] = ctail_ref[...]
    hlast_ref[...] = hcar_ref[...]
    hb_ref[...] = jnp.concatenate(
        [jnp.concatenate([hbs_ref[c, s * pitch:s * pitch + nj, :] for s in range(SUBLANES_V7X)], axis=0)
         for c in range(D_RNN // LANES_V7X)], axis=1).astype(hb_ref.dtype)

    def head_norm(t, gain):
        return _head_rms_norm(t) * gain

    def qkv(h, g):
        c = g * GROUP_W
        qn = head_norm(_mm(h, w_ref[:, OFF_Q + c:OFF_Q + c + GROUP_W]), qg_ref[...])
        kn = head_norm(_mm(h, w_ref[:, OFF_K + c:OFF_K + c + GROUP_W]), kg_ref[...])
        vv = _mm(h, w_ref[:, OFF_V + c:OFF_V + c + GROUP_W])
        return qn, kn, vv

    in_order = {}
    for g, d in enumerate(dilations):
        rows = tm // d
        if d == 1:
            hg = hn
        else:
            hg = jnp.concatenate([strided_rows(r, rows, d) for r in range(d)], axis=0).astype(BF16)
        parts = qkv(hg, g)
        if d == 1:
            in_order[g] = parts
        for t, o_ref in zip(parts, qkv_refs[3 * g:3 * g + 3]):
            for r in range(d):
                o_ref[:, r * GROUP_W:(r + 1) * GROUP_W] = t[r * rows:(r + 1) * rows].astype(o_ref.dtype)

    ga_ref[...] = _mm(hn, w_ref[:, OFF_GA:OFF_GA + D_MODEL])
    gb2_ref[...] = _mm(hn, w_ref[:, OFF_GB2:OFF_GB2 + D_MODEL])

    for g, (kvt_ref, first_step) in enumerate(zip(kvt_refs, tail_first_steps)):
        @pl.when(step >= first_step)
        def _():
            _, kn, vv = in_order[g] if g in in_order else qkv(hn, g)
            keep = kvt_ref.shape[0]
            kvt_ref[:, 0:GROUP_W] = kn[tm - keep:]
            kvt_ref[:, GROUP_W:2 * GROUP_W] = vv[tm - keep:]


def _inproj(x, g1, w_in, qg, kg, lru, to_bf16, *, tm, tails, dilations):
    m = x.shape[0]
    nt = m // tm
    row = lambda w: pl.BlockSpec((tm, w), lambda i: (i, 0))
    fixed = lambda r, w: pl.BlockSpec((r, w), lambda i: (0, 0))
    qkv_specs, qkv_shapes = [], []
    for d in dilations:
        qkv_specs += [pl.BlockSpec((tm // d, d * GROUP_W), lambda i: (i, 0))] * 3
        qkv_shapes += [jax.ShapeDtypeStruct((m // d, d * GROUP_W), BF16)] * 3
    kvt_specs, kvt_shapes, tail_first_steps = [], [], []
    for rows in tails:
        blk = min(rows, tm)
        first = nt - rows // blk
        kvt_specs.append(pl.BlockSpec((blk, 2 * GROUP_W), lambda i, first=first: (jnp.maximum(i - first, 0), 0)))
        kvt_shapes.append(jax.ShapeDtypeStruct((rows, 2 * GROUP_W), F32))
        tail_first_steps.append(first)
    tail_rows = (CONV_W - 1) * SUBLANES_V7X
    rnn_specs = (row(D_RNN), fixed(tail_rows, D_RNN), fixed(1, D_RNN))
    rnn_shapes = (jax.ShapeDtypeStruct((m, D_RNN), BF16),
                  jax.ShapeDtypeStruct((tail_rows, D_RNN), F32),
                  jax.ShapeDtypeStruct((1, D_RNN), F32))
    padded = tm + SUBLANES_V7X * SUBLANES_V7X
    scratch = [pltpu.VMEM((D_MODEL // LANES_V7X, tm, LANES_V7X), F32),
               pltpu.VMEM((D_RNN // LANES_V7X, padded, LANES_V7X), F32),
               pltpu.VMEM((D_MODEL // LANES_V7X, padded, LANES_V7X), F32),
               pltpu.VMEM((tail_rows, D_RNN), F32),
               pltpu.VMEM((1, D_RNN), F32)]
    out_shape = tuple(qkv_shapes) + rnn_shapes + (
        jax.ShapeDtypeStruct((m, D_MODEL), F32),
        jax.ShapeDtypeStruct((m, D_MODEL), F32),
    ) + tuple(kvt_shapes)
    consts = (g1, w_in, qg, kg) + tuple(lru)
    cast_specs, cast_shapes = [], []
    bf16_rows = 2 * SUBLANES_V7X
    for w in to_bf16:
        nblk = nt
        while w.shape[0] % (nblk * bf16_rows):
            nblk //= 2
        cast_specs.append(pl.BlockSpec((w.shape[0] // nblk, w.shape[1]),
                                       lambda i, per=nt // nblk: (i // per, 0)))
        cast_shapes.append(jax.ShapeDtypeStruct(w.shape, BF16))
    return pl.pallas_call(
        functools.partial(_inproj_kernel, dilations=dilations, tail_first_steps=tuple(tail_first_steps),
                          n_cast=len(to_bf16)),
        grid=(nt,),
        in_specs=[row(D_MODEL)] + [_const_spec(c.shape) for c in consts] + cast_specs,
        out_specs=(tuple(qkv_specs) + rnn_specs + (row(D_MODEL), row(D_MODEL)) + tuple(kvt_specs)
                   + tuple(cast_specs)),
        out_shape=out_shape + tuple(cast_shapes),
        scratch_shapes=scratch,
        compiler_params=_params(("arbitrary",)),
        name="inproj_lru",
    )(x, *consts, *to_bf16)


def _sample_inproj_kernel(x_ref, g1_ref, qg_ref, kg_ref, w_ref, proj_ref, wb_ref):
    block = pl.program_id(0)
    wb = w_ref[...].astype(BF16)
    wb_ref[...] = wb
    t = _mm(_rms_norm_rows(x_ref[...], g1_ref[...]).astype(BF16), wb)
    normed = _head_rms_norm(t)
    proj_ref[...] = jnp.where(block < N_GROUPS, normed * qg_ref[...],
                              jnp.where(block < 2 * N_GROUPS, normed * kg_ref[...], t))


def _sample_inproj(x, g1, w_in, qg, kg):
    nb = x.shape[0]
    n_blocks = w_in.shape[1] // GROUP_W
    consts = (x, g1, qg, kg)
    col = lambda rows: pl.BlockSpec((rows, GROUP_W), lambda j: (0, j))
    return pl.pallas_call(
        _sample_inproj_kernel,
        grid=(n_blocks,),
        in_specs=[_const_spec(c.shape) for c in consts] + [col(D_MODEL)],
        out_specs=(col(nb), col(D_MODEL)),
        out_shape=(jax.ShapeDtypeStruct((nb, w_in.shape[1]), F32),
                   jax.ShapeDtypeStruct(w_in.shape, BF16)),
        compiler_params=_params(("arbitrary",)),
        name="inproj_sample",
    )(*consts, w_in)


ATTN_BLOCKS_PER_STEP = 8


def _attn_prompt_kernel(q_ref, kc_ref, kp_ref, vc_ref, vp_ref, *refs, sample_first_row):
    if sample_first_row is None:
        o_ref, lse_ref = refs
    else:
        qs_ref, ks_ref, vs_ref, c0_ref, c1_ref, c2_ref, o_ref, lse_ref, os_ref = refs
        flat_step = pl.program_id(0) * pl.num_programs(1) + pl.program_id(1)
        _attn_sample_row(flat_step, sample_first_row + flat_step, qs_ref, ks_ref, vs_ref,
                         (c0_ref, c1_ref, c2_ref), os_ref)
    step = pl.program_id(1)
    qi = lax.broadcasted_iota(jnp.int32, (BLOCK, 2 * BLOCK), 0)
    kj = lax.broadcasted_iota(jnp.int32, (BLOCK, 2 * BLOCK), 1)
    dist = BLOCK + qi - kj
    band = (dist >= 0) & (dist <= BLOCK)
    first_band = band & ((step > 0) | (kj >= BLOCK))
    band2 = jnp.concatenate([band, band], axis=0)
    first_band2 = jnp.concatenate([first_band, first_band], axis=0)
    first_head = lax.broadcasted_iota(jnp.int32, (1, LANES_V7X), 1) < HEAD_DIM
    zero = jnp.zeros((), BF16)

    for blk in range(q_ref.shape[0] // BLOCK):
        rows = slice(blk * BLOCK, (blk + 1) * BLOCK)
        prev_rows = slice((blk - 1) * BLOCK, blk * BLOCK)
        valid = band2 if blk else first_band2
        lse_ref[rows, :] = jnp.zeros((BLOCK, LANES_V7X), F32)
        for pair in range(HEADS_PER_GROUP // 2):
            cols = slice(pair * LANES_V7X, (pair + 1) * LANES_V7X)
            qp = q_ref[rows, cols]
            k_prev = kc_ref[prev_rows, cols] if blk else kp_ref[:, cols]
            v_prev = vc_ref[prev_rows, cols] if blk else vp_ref[:, cols]
            kk = jnp.concatenate([k_prev, kc_ref[rows, cols]], axis=0)
            vv = jnp.concatenate([v_prev, vc_ref[rows, cols]], axis=0)
            q2 = jnp.concatenate([jnp.where(first_head, qp, zero), jnp.where(first_head, zero, qp)], axis=0)
            s = jnp.where(valid, _mm_nt(q2, kk), NEG_INF)
            mx = jnp.max(s, axis=-1, keepdims=True)
            p = jnp.exp2(s - mx)
            den = jnp.sum(p, axis=-1, keepdims=True)
            pv = _mm(p.astype(BF16), vv)
            o_ref[rows, cols] = jnp.where(first_head, pv[:BLOCK], pv[BLOCK:]).astype(o_ref.dtype)
            for e, (lo, hi) in enumerate(((0, BLOCK), (BLOCK, 2 * BLOCK))):
                head = 2 * pair + e
                lse_ref[rows, head:head + 1] = mx[lo:hi]
                lse_ref[rows, HEADS_PER_GROUP + head:HEADS_PER_GROUP + head + 1] = den[lo:hi]


def _attn_prompt(q, k, v, g, dilation, sample=None):
    m_len = q.shape[0]
    nbs = ATTN_BLOCKS_PER_STEP
    rows = nbs * BLOCK
    n_inner = m_len // rows
    cur = pl.BlockSpec((rows, GROUP_W), lambda r, b: (b, r))
    prev = pl.BlockSpec((BLOCK, GROUP_W), lambda r, b: (jnp.maximum(b * nbs - 1, 0), r))
    in_specs, args = [cur, cur, prev, cur, prev], [q, k, k, v, v]
    out_specs = [pl.BlockSpec((rows, GROUP_W), lambda r, b: (b, r)),
                 pl.BlockSpec((rows, LANES_V7X), lambda r, b: (b, r))]
    out_shape = [jax.ShapeDtypeStruct((m_len, dilation * GROUP_W), BF16),
                 jax.ShapeDtypeStruct((m_len, dilation * LANES_V7X), F32)]
    first_row = None
    if sample is not None:
        qs, ks, vs, caches, first_row = sample
        steps = dilation * n_inner
        full = pl.BlockSpec(qs.shape, lambda r, b: (0, 0, 0))
        in_specs += [full, full, full] + [
            pl.BlockSpec((None,) + c.shape[1:], lambda r, b: (first_row + r * n_inner + b, 0, 0))
            for c in caches]
        args += [qs, ks, vs, *caches]
        out_specs.append(pl.BlockSpec((steps, GROUP_W), lambda r, b: (0, 0)))
        out_shape.append(jax.ShapeDtypeStruct((steps, GROUP_W), F32))
    return pl.pallas_call(
        functools.partial(_attn_prompt_kernel, sample_first_row=first_row),
        grid=(dilation, n_inner),
        in_specs=in_specs,
        out_specs=tuple(out_specs),
        out_shape=tuple(out_shape),
        compiler_params=_params(("arbitrary", "arbitrary")),
        name=f"attn_prompt_g{g}",
    )(*args)


def _attn_sample_row(out_row, b, q_ref, k_ref, v_ref, cache_refs, o_ref):
    head_row = lax.broadcasted_iota(jnp.int32, (HEADS_PER_GROUP, GROUP_W), 0)
    head_lane = lax.broadcasted_iota(jnp.int32, (HEADS_PER_GROUP, GROUP_W), 1) // HEAD_DIM
    own = head_row == head_lane
    parts = []
    for g, c_ref in enumerate(cache_refs):
        window, dilation = GROUPS[g]
        bf = lambda t: t.astype(BF16).astype(F32)
        qmat = jnp.where(own, jnp.broadcast_to(q_ref[g, pl.ds(b, 1), :], (HEADS_PER_GROUP, GROUP_W)), 0.0)
        qmat = qmat.astype(BF16)
        knew = bf(k_ref[g, pl.ds(b, 1), :])
        vnew = bf(v_ref[g, pl.ds(b, 1), :])
        pos = lax.broadcasted_iota(jnp.int32, (HEADS_PER_GROUP, window), 1)
        s = _mm(qmat, c_ref[0:GROUP_W, :].astype(BF16))
        s = jnp.where((pos & (dilation - 1)) == 0, s, NEG_INF)
        s_new = jnp.sum(qmat.astype(F32) * knew, axis=-1, keepdims=True)
        mx = jnp.maximum(jnp.max(s, axis=-1, keepdims=True), s_new)
        p = jnp.exp2(s - mx)
        p_new = jnp.exp2(s_new - mx)
        den = jnp.sum(p, axis=-1, keepdims=True) + p_new
        numer = _mm_nt(p.astype(BF16), c_ref[GROUP_W:2 * GROUP_W, :].astype(BF16)) + bf(p_new) * vnew
        parts.append((jnp.where(own, numer, 0.0), mx, den))
    m_all = functools.reduce(jnp.maximum, [mx for _, mx, _ in parts])
    ws = [jnp.exp2(mx - m_all) for _, mx, _ in parts]
    num = sum(n * w for (n, _, _), w in zip(parts, ws))
    den = sum(d * w for (_, _, d), w in zip(parts, ws))
    o_ref[pl.ds(out_row, 1), :] = jnp.sum(num / den, axis=0, keepdims=True)


def _gelu_tanh(x):
    cdf = 0.5 * (1.0 + jnp.tanh(0.7978845608028654 * (x + 0.044715 * (x * x * x))))
    return x * cdf


def _softplus(x):
    return jnp.maximum(x, 0.0) + jnp.log1p(jnp.exp(-jnp.abs(x)))


LRU_CHUNK = 2 * RNN_BLOCK_W
N_LRU_CHUNKS = D_RNN // LRU_CHUNK


def _lru_lanes(n):
    return slice(n * LRU_CHUNK, (n + 1) * LRU_CHUNK)


def _lru_coeffs(xc, n, wrg_ref, brg_ref, wig_ref, big_ref, lam_ref):
    lanes = _lru_lanes(n)
    sigmoid = lambda v: 0.5 + 0.5 * jnp.tanh(0.5 * v)
    xcb = xc.astype(BF16)
    r = sigmoid(_mm(xcb, wrg_ref[n]) + brg_ref[:, lanes])
    i = sigmoid(_mm(xcb, wig_ref[n]) + big_ref[:, lanes])
    neg_rate = LRU_C * _softplus(-lam_ref[:, lanes])
    a = jnp.exp2(r * (-LOG2_E * neg_rate))
    t = jnp.tanh(r * neg_rate)
    y = 2.0 * t / (1.0 + t)
    root = jnp.where(y > 0.0, y * lax.rsqrt(y), 0.0)
    return a, root * i * xc


def _pair_blocks(w):
    z = jnp.zeros_like(w[0::2])
    top = jnp.concatenate([w[0::2], z], axis=2)
    bot = jnp.concatenate([z, w[1::2]], axis=2)
    return jnp.concatenate([top, bot], axis=1)


def _lru_tile(xb, gb, n, lru_refs, ctail_ref, hcar_ref):
    cw_ref, cb_ref, wrg_ref, brg_ref, wig_ref, big_ref, lam_ref = lru_refs
    lanes = _lru_lanes(n)
    sl = SUBLANES_V7X
    tm = xb.shape[0]
    nj = tm // sl
    sub = lax.broadcasted_iota(jnp.int32, (sl, LRU_CHUNK), 0)
    vrow = lambda t, j: t[j * sl:(j + 1) * sl]

    taps = CONV_W - 1
    wrap = [pltpu.roll(jnp.where(sub == sl - 1, vrow(ctail_ref[:, lanes], i), vrow(xb, nj - taps + i)), 1, 0)
            for i in range(taps)]
    ctail_ref[:, lanes] = xb[tm - taps * sl:]
    xc = cb_ref[:, lanes] + xb * cw_ref[taps:taps + 1, lanes]
    for k in range(1, CONV_W):
        shifted = jnp.concatenate(wrap[taps - k:] + [xb[:tm - k * sl]], axis=0)
        xc = xc + shifted * cw_ref[taps - k:taps - k + 1, lanes]

    a, b = _lru_coeffs(xc, n, wrg_ref, brg_ref, wig_ref, big_ref, lam_ref)
    gate = _gelu_tanh(gb)

    hl, acc = vrow(b, 0), vrow(a, 0)
    hls, accs = [hl], [acc]
    for j in range(1, nj):
        hl = vrow(a, j) * hl + vrow(b, j)
        acc = vrow(a, j) * acc
        hls.append(hl)
        accs.append(acc)

    for s in (1, 2, 4):
        keep = sub >= s
        acc_prev = jnp.where(keep, pltpu.roll(acc, s, 0), 1.0)
        hl_prev = jnp.where(keep, pltpu.roll(hl, s, 0), 0.0)
        hl = acc * hl_prev + hl
        acc = acc * acc_prev
    carry = hcar_ref[:, lanes]
    h_end = hl + acc * carry
    h_in = jnp.where(sub == 0, carry, pltpu.roll(h_end, 1, 0))
    hcar_ref[:, lanes] = h_end[sl - 1:sl]

    return [(hls[j] + accs[j] * h_in) * vrow(gate, j) for j in range(nj)]


def _rglru_sample_kernel(xb_ref, gb_ref, sc_ref, h0_ref, cw_ref, cb_ref, wrg_ref, brg_ref, wig_ref,
                         big_ref, lam_ref, hb_ref, conv_ref, h_ref):
    xb = xb_ref[...]
    taps = [sc_ref[:, j * D_RNN:(j + 1) * D_RNN] for j in range(CONV_W - 1)] + [xb]
    xc = cb_ref[...] + sum(t * cw_ref[j:j + 1, :] for j, t in enumerate(taps))
    for n in range(N_LRU_CHUNKS):
        lanes = _lru_lanes(n)
        a, b = _lru_coeffs(xc[:, lanes], n, wrg_ref, brg_ref, wig_ref, big_ref, lam_ref)
        h = a * h0_ref[:, lanes] + b
        h_ref[:, lanes] = h
        hb_ref[:, lanes] = (h * _gelu_tanh(gb_ref[:, lanes])).astype(hb_ref.dtype)
    for j in range(CONV_W - 1):
        conv_ref[:, j * D_RNN:(j + 1) * D_RNN] = taps[j + 1]


def _rglru_sample(xb, gb, state_conv, h0, cw, cb, wrg, brg, wig, big, lam):
    nb = xb.shape[0]
    args = (xb, gb, state_conv, h0, cw, cb, wrg, brg, wig, big, lam)
    return pl.pallas_call(
        _rglru_sample_kernel,
        grid=(1,),
        in_specs=[_const_spec(a.shape) for a in args],
        out_specs=(_const_spec((nb, D_RNN)), _const_spec((nb, (CONV_W - 1) * D_RNN)),
                   _const_spec((nb, D_RNN))),
        out_shape=(jax.ShapeDtypeStruct((nb, D_RNN), BF16),
                   jax.ShapeDtypeStruct((nb, (CONV_W - 1) * D_RNN), F32),
                   jax.ShapeDtypeStruct((nb, D_RNN), F32)),
        compiler_params=_params(("arbitrary",)),
        name="rglru_sample",
    )(*args)


def _post_kernel(*refs, dilations):
    n_parts = len(dilations)
    attn_refs = refs[:2 * n_parts] if n_parts > 1 else refs[:1]
    rest = refs[len(attn_refs):]
    (hb_ref, ga_ref, gb2_ref, x_ref, expand_ref, bg_ref, woa_ref, wor_ref, wout_ref, g2_ref,
     wfi_ref, wfo_ref, y_ref) = rest[:13]
    tm = x_ref.shape[0]

    if n_parts > 1:
        o_nat_ref, lse_nat_ref = rest[13:]

        def natural(src_ref, dst_ref, width, d):
            if d == 1:
                return src_ref[...].astype(F32)
            n_slabs = width // LANES_V7X
            for r in range(d):
                for c in range(n_slabs):
                    lanes = slice(r * width + c * LANES_V7X, r * width + (c + 1) * LANES_V7X)
                    dst_ref[c, pl.ds(r, tm // d, stride=d), :] = src_ref[:, lanes].astype(F32)
            return jnp.concatenate([dst_ref[c] for c in range(n_slabs)], axis=1)

        o_refs, stat_refs = attn_refs[:n_parts], attn_refs[n_parts:]
        stats = [natural(r, lse_nat_ref, LANES_V7X, d) for r, d in zip(stat_refs, dilations)]
        m_all = functools.reduce(jnp.maximum, stats)
        ws = [jnp.exp2(st - m_all) for st in stats]
        dens = [pltpu.roll(st, LANES_V7X - HEADS_PER_GROUP, 1) for st in stats]
        inv = 1.0 / sum(w * dn for w, dn in zip(ws, dens))
        is_head = lax.broadcasted_iota(jnp.int32, (1, LANES_V7X), 1) < HEADS_PER_GROUP
        attn = 0.0
        for o_ref, w, d in zip(o_refs, ws, dilations):
            cw = jnp.where(is_head, w * inv, 0.0)
            hi = cw.astype(BF16)
            lo = (cw - hi.astype(F32)).astype(BF16)
            wide = _mm(jnp.concatenate([hi, lo], axis=1), expand_ref[...])
            attn = attn + wide * natural(o_ref, o_nat_ref, GROUP_W, d)
    else:
        attn = attn_refs[0][...]

    ya = _mm(attn.astype(BF16), woa_ref[...])
    yb = _mm(hb_ref[...], wor_ref[...])
    merged = (jax.nn.sigmoid(ga_ref[...] + bg_ref[0:1, :]) * ya
              + jax.nn.sigmoid(gb2_ref[...] + bg_ref[1:2, :]) * yb)
    x1 = x_ref[...] + _mm(merged.astype(BF16), wout_ref[...])
    hn2 = _rms_norm_rows(x1, g2_ref[...]).astype(BF16)
    gu = _mm(hn2, wfi_ref[...])
    act = jax.nn.silu(gu[:, :D_FF]) * gu[:, D_FF:]
    y_ref[...] = x1 + _mm(act.astype(BF16), wfo_ref[...])


def _post(attn_parts, dilations, hb, ga, gb2, x, expand, bg, woa, wor, wout, g2, wfi, wfo, *, tm):
    m = x.shape[0]
    row = lambda w: pl.BlockSpec((tm, w), lambda i: (i, 0))
    n_parts = len(attn_parts)
    scratch = []
    if n_parts > 1:
        blocked = lambda w, d: pl.BlockSpec((tm // d, d * w), lambda i: (i, 0))
        attn_args = [o for o, _ in attn_parts] + [l for _, l in attn_parts]
        attn_specs = ([blocked(GROUP_W, d) for d in dilations]
                      + [blocked(LANES_V7X, d) for d in dilations])
        scratch = [pltpu.VMEM((GROUP_W // LANES_V7X, tm, LANES_V7X), F32),
                   pltpu.VMEM((1, tm, LANES_V7X), F32)]
    else:
        attn_args, attn_specs = list(attn_parts), [row(GROUP_W)]
    consts = (expand, bg, woa, wor, wout, g2, wfi, wfo)
    return pl.pallas_call(
        functools.partial(_post_kernel, dilations=dilations),
        grid=(m // tm,),
        in_specs=attn_specs + [row(D_RNN), row(D_MODEL), row(D_MODEL), row(D_MODEL)]
        + [_const_spec(c.shape) for c in consts],
        out_specs=row(D_MODEL),
        out_shape=jax.ShapeDtypeStruct((m, D_MODEL), F32),
        scratch_shapes=scratch,
        compiler_params=_params(("arbitrary",)),
        name="post_prompt" if n_parts > 1 else "post_sample",
    )(*attn_args, hb, ga, gb2, x, *consts)


def kernel(x_prompt, x_sample, cache_kv_w128, cache_kv_w512, cache_kv_w2048, state_conv, state_h,
           norm1_g, w_in, b_gate, q_norm_g, k_norm_g, conv_w, conv_b, w_rg, b_rg, w_ig, b_ig,
           lru_lambda, w_o_attn, w_o_rnn, w_out, norm2_g, w_ffn_in, w_ffn_out):
    assert x_prompt.shape[0] == 1 and norm1_g.shape[0] == 1 and x_sample.shape[1] == 1
    seq = x_prompt.shape[1]
    nb = x_sample.shape[0]
    layer = 0

    expand = (jnp.arange(LANES_V7X)[:, None] == (jnp.arange(GROUP_W) // HEAD_DIM)[None, :]).astype(BF16)
    expand = jnp.concatenate([expand, expand], axis=0)

    row2 = lambda t: t[layer].reshape(1, -1)
    g1, g2 = row2(norm1_g), row2(norm2_g)
    qg = jnp.tile(q_norm_g[layer], HEADS_PER_GROUP).reshape(1, GROUP_W) * (ATTN_SCALE * LOG2_E)
    kg = jnp.tile(k_norm_g[layer], HEADS_PER_GROUP).reshape(1, GROUP_W)
    wrg, wig = _pair_blocks(w_rg[layer].astype(BF16)), _pair_blocks(w_ig[layer].astype(BF16))
    lru = (conv_w[layer], row2(conv_b), wrg, row2(b_rg), wig, row2(b_ig), row2(lru_lambda))
    post_f32 = tuple(w[layer] for w in (w_o_attn, w_o_rnn, w_out, w_ffn_in, w_ffn_out))

    xs = x_sample[:, 0]
    proj, w_in_b = _sample_inproj(xs, g1, w_in[layer], qg, kg)
    qkvs = proj[:, :OFF_XB].reshape(nb, 3 * N_GROUPS, GROUP_W).transpose(1, 0, 2)
    qs, ks, vs = qkvs[:N_GROUPS], qkvs[N_GROUPS:2 * N_GROUPS], qkvs[2 * N_GROUPS:]
    xbs, gbs, gas, gb2s = (proj[:, a:b] for a, b in
                           ((OFF_XB, OFF_GB), (OFF_GB, OFF_GA), (OFF_GA, OFF_GB2), (OFF_GB2, proj.shape[1])))
    caches = [jnp.transpose(c[layer], (0, 2, 3, 4, 1)).reshape(nb, 2 * GROUP_W, c.shape[2])
              for c in (cache_kv_w128, cache_kv_w512, cache_kv_w2048)]

    xp = x_prompt[0]
    *qkv, hb, conv_rows, h_last, ga, gb2, kvp0, kvp1, kvp2, woa, wor, wout, wfi, wfo = _inproj(
        xp, g1, w_in_b, qg, kg, lru, post_f32, tm=PROMPT_ROW_TILE,
        tails=tuple(min(w, seq) for w, _ in GROUPS), dilations=DILATIONS)
    post_w = (expand, b_gate[layer], woa, wor, wout, g2, wfi, wfo)
    half = nb // 2
    parts, attn_s = [], []
    for g, d in enumerate(DILATIONS):
        sample = (qs, ks, vs, caches, (g - 1) * half) if g else None
        *part, = _attn_prompt(*qkv[3 * g:3 * g + 3], g, d, sample)
        parts.append(part[:2])
        attn_s += part[2:]
    attn_s = jnp.concatenate(attn_s, axis=0)
    assert attn_s.shape == (nb, GROUP_W)
    y_p = _post(parts, DILATIONS, hb, ga, gb2, xp, *post_w, tm=PROMPT_ROW_TILE)
    conv_p = conv_rows[SUBLANES_V7X - 1::SUBLANES_V7X]

    hbs, conv_s, h_s = _rglru_sample(xbs, gbs, state_conv[layer].reshape(nb, -1), state_h[layer], *lru)
    y_s = _post([attn_s], (1,), hbs, gas, gb2s, xs, *post_w, tm=nb)

    kv_prompt = [t.reshape(1, 1, t.shape[0], 2, HEADS_PER_GROUP, HEAD_DIM) for t in (kvp0, kvp1, kvp2)]
    kv_sample = [jnp.stack([ks[g], vs[g]], axis=1).reshape(1, nb, 1, 2, HEADS_PER_GROUP, HEAD_DIM)
                 for g in range(N_GROUPS)]
    return (y_p[None], y_s[:, None],
            kv_prompt[0], kv_prompt[1], kv_prompt[2],
            conv_p[None, None], h_last[None],
            kv_sample[0], kv_sample[1], kv_sample[2],
            conv_s.reshape(1, nb, CONV_W - 1, D_RNN), h_s[None])
```

```python
import functools

import jax
import jax.numpy as jnp
from jax import lax
from jax.experimental import pallas as pl
from jax.experimental.pallas import tpu as pltpu

F32 = jnp.float32
BF16 = jnp.bfloat16

D_MODEL = 1024
HEAD_DIM = 64
HEADS_PER_GROUP = 8
GROUPS = ((128, 1), (512, 4), (2048, 16))
DILATIONS = tuple(d for _, d in GROUPS)
N_GROUPS = len(GROUPS)
GROUP_W = HEADS_PER_GROUP * HEAD_DIM
QKV_WIDTH = N_GROUPS * GROUP_W
BLOCK = 128
ATTN_SCALE = HEAD_DIM ** -0.5
LOG2_E = 1.4426950408889634
NEG_INF = -1e30
D_RNN = 1280
RNN_BLOCKS = 10
RNN_BLOCK_W = D_RNN // RNN_BLOCKS
CONV_W = 4
LRU_C = 8.0
D_FF = 2816
RMS_EPS = 1e-6

LANES_V7X = 128
SUBLANES_V7X = 8
MXU_DIM_V7X = 256
VMEM_LIMIT_BYTES = 56 * 1024 * 1024
PROMPT_ROW_TILE = MXU_DIM_V7X

OFF_Q, OFF_K, OFF_V = 0, QKV_WIDTH, 2 * QKV_WIDTH
OFF_XB = 3 * QKV_WIDTH
OFF_GB = OFF_XB + D_RNN
OFF_GA = OFF_GB + D_RNN
OFF_GB2 = OFF_GA + D_MODEL


def _mm(a, b):
    return jnp.dot(a, b, preferred_element_type=F32)


def _mm_nt(a, b):
    return lax.dot_general(a, b, (((1,), (1,)), ((), ())), preferred_element_type=F32)


def _rms_norm_rows(x, g):
    return x * lax.rsqrt(jnp.mean(x * x, axis=-1, keepdims=True) + RMS_EPS) * g


def _head_rms_norm(t):
    first_head = lax.broadcasted_iota(jnp.int32, (1, LANES_V7X), 1) < HEAD_DIM
    cols = []
    for c in range(0, t.shape[1], LANES_V7X):
        x = t[:, c:c + LANES_V7X]
        xx = x * x
        s0 = jnp.sum(jnp.where(first_head, xx, 0.0), axis=-1, keepdims=True)
        s1 = jnp.sum(jnp.where(first_head, 0.0, xx), axis=-1, keepdims=True)
        ms = jnp.where(first_head, s0, s1) * (1.0 / HEAD_DIM)
        cols.append(x * lax.rsqrt(ms + RMS_EPS))
    return jnp.concatenate(cols, axis=1)


def _const_spec(shape):
    nd = len(shape)
    return pl.BlockSpec(shape, lambda *_: (0,) * nd, pipeline_mode=pl.Buffered(1))


def _params(sem):
    return pltpu.CompilerParams(dimension_semantics=sem, vmem_limit_bytes=VMEM_LIMIT_BYTES)


def _inproj_kernel(x0_ref, xn_ref, g1_ref, w_ref, qg_ref, kg_ref, *refs, dilations, tail_first_steps,
                   n_cast):
    lru_refs, refs = refs[:6], refs[6:]
    cast_in, refs = refs[:n_cast], refs[n_cast:]
    qkv_refs, refs = refs[:3 * N_GROUPS], refs[3 * N_GROUPS:]
    (hb_ref, conv_ref, hlast_ref, ga_ref, gb2_ref), refs = refs[:5], refs[5:]
    kvt_refs, refs = refs[:N_GROUPS], refs[N_GROUPS:]
    cast_out, refs = refs[:n_cast], refs[n_cast:]
    hn_ref, hbs_ref, hnp_ref, ctail_ref, hcar_ref = refs
    for src_ref, dst_ref in zip(cast_in, cast_out):
        dst_ref[...] = src_ref[...].astype(dst_ref.dtype)
    step = pl.program_id(0)
    tm = xn_ref.shape[0]
    n_slabs = D_MODEL // LANES_V7X
    nj = tm // SUBLANES_V7X
    pitch = nj + SUBLANES_V7X

    def stage_normed_rows(x_ref):
        hn32 = _rms_norm_rows(x_ref[...], g1_ref[...])
        for c in range(n_slabs):
            lanes = slice(c * LANES_V7X, (c + 1) * LANES_V7X)
            hn_ref[c] = hn32[:, lanes]
            for s in range(SUBLANES_V7X):
                hnp_ref[c, s * pitch:s * pitch + nj, :] = hn32[s * nj:(s + 1) * nj, lanes]

    @pl.when(step == 0)
    def _():
        ctail_ref[...] = jnp.zeros(ctail_ref.shape, F32)
        hcar_ref[...] = jnp.zeros(hcar_ref.shape, F32)
        stage_normed_rows(x0_ref)

    hn = jnp.concatenate([hn_ref[c] for c in range(n_slabs)], axis=1).astype(BF16)

    def strided_rows(start, size, stride):
        return jnp.concatenate([hn_ref[c, pl.ds(start, size, stride=stride), :] for c in range(n_slabs)],
                               axis=1)

    ht = jnp.concatenate(
        [jnp.concatenate([hnp_ref[c, pl.ds(j, SUBLANES_V7X, stride=pitch), :] for c in range(n_slabs)],
                         axis=1) for j in range(nj)], axis=0).astype(BF16)
    per_vreg = LRU_CHUNK // LANES_V7X
    xg = _mm(ht, w_ref[:, OFF_XB:OFF_GA])
    for n in range(N_LRU_CHUNKS):
        c0 = n * LRU_CHUNK
        hb_rows = _lru_tile(xg[:, c0:c0 + LRU_CHUNK], xg[:, D_RNN + c0:D_RNN + c0 + LRU_CHUNK],
                            n, lru_refs, ctail_ref, hcar_ref)
        for j, rows in enumerate(hb_rows):
            for c in range(per_vreg):
                hbs_ref[n * per_vreg + c, pl.ds(j, SUBLANES_V7X, stride=pitch), :] = (
                    rows[:, c * LANES_V7X:(c + 1) * LANES_V7X])
    conv_ref[...] = ctail_ref[...]
    hlast_ref[...] = hcar_ref[...]
    hb_ref[...] = jnp.concatenate(
        [jnp.concatenate([hbs_ref[c, s * pitch:s * pitch + nj, :] for s in range(SUBLANES_V7X)], axis=0)
         for c in range(D_RNN // LANES_V7X)], axis=1).astype(hb_ref.dtype)

    def head_norm(t, gain):
        return _head_rms_norm(t) * gain

    def qkv(h, g):
        c = g * GROUP_W
        qn = head_norm(_mm(h, w_ref[:, OFF_Q + c:OFF_Q + c + GROUP_W]), qg_ref[...])
        kn = head_norm(_mm(h, w_ref[:, OFF_K + c:OFF_K + c + GROUP_W]), kg_ref[...])
        vv = _mm(h, w_ref[:, OFF_V + c:OFF_V + c + GROUP_W])
        return qn, kn, vv

    in_order = {}
    for g, d in enumerate(dilations):
        rows = tm // d
        if d == 1:
            hg = hn
        else:
            hg = jnp.concatenate([strided_rows(r, rows, d) for r in range(d)], axis=0).astype(BF16)
        parts = qkv(hg, g)
        if d == 1:
            in_order[g] = parts
        for t, o_ref in zip(parts, qkv_refs[3 * g:3 * g + 3]):
            for r in range(d):
                o_ref[:, r * GROUP_W:(r + 1) * GROUP_W] = t[r * rows:(r + 1) * rows].astype(o_ref.dtype)

    gates = _mm(hn, w_ref[:, OFF_GA:OFF_GB2 + D_MODEL])
    ga_ref[...] = gates[:, :D_MODEL]
    gb2_ref[...] = gates[:, D_MODEL:]

    stage_normed_rows(xn_ref)

    for g, (kvt_ref, first_step) in enumerate(zip(kvt_refs, tail_first_steps)):
        @pl.when(step >= first_step)
        def _():
            _, kn, vv = in_order[g] if g in in_order else qkv(hn, g)
            keep = kvt_ref.shape[0]
            kvt_ref[:, 0:GROUP_W] = kn[tm - keep:]
            kvt_ref[:, GROUP_W:2 * GROUP_W] = vv[tm - keep:]


def _inproj(x, g1, w_in, qg, kg, lru, to_bf16, *, tm, tails, dilations):
    m = x.shape[0]
    nt = m // tm
    row = lambda w: pl.BlockSpec((tm, w), lambda i: (i, 0))
    fixed = lambda r, w: pl.BlockSpec((r, w), lambda i: (0, 0))
    qkv_specs, qkv_shapes = [], []
    for d in dilations:
        qkv_specs += [pl.BlockSpec((tm // d, d * GROUP_W), lambda i: (i, 0))] * 3
        qkv_shapes += [jax.ShapeDtypeStruct((m // d, d * GROUP_W), BF16)] * 3
    kvt_specs, kvt_shapes, tail_first_steps = [], [], []
    for rows in tails:
        blk = min(rows, tm)
        first = nt - rows // blk
        kvt_specs.append(pl.BlockSpec((blk, 2 * GROUP_W), lambda i, first=first: (jnp.maximum(i - first, 0), 0)))
        kvt_shapes.append(jax.ShapeDtypeStruct((rows, 2 * GROUP_W), F32))
        tail_first_steps.append(first)
    tail_rows = (CONV_W - 1) * SUBLANES_V7X
    rnn_specs = (row(D_RNN), fixed(tail_rows, D_RNN), fixed(1, D_RNN))
    rnn_shapes = (jax.ShapeDtypeStruct((m, D_RNN), BF16),
                  jax.ShapeDtypeStruct((tail_rows, D_RNN), F32),
                  jax.ShapeDtypeStruct((1, D_RNN), F32))
    padded = tm + SUBLANES_V7X * SUBLANES_V7X
    scratch = [pltpu.VMEM((D_MODEL // LANES_V7X, tm, LANES_V7X), F32),
               pltpu.VMEM((D_RNN // LANES_V7X, padded, LANES_V7X), F32),
               pltpu.VMEM((D_MODEL // LANES_V7X, padded, LANES_V7X), F32),
               pltpu.VMEM((tail_rows, D_RNN), F32),
               pltpu.VMEM((1, D_RNN), F32)]
    out_shape = tuple(qkv_shapes) + rnn_shapes + (
        jax.ShapeDtypeStruct((m, D_MODEL), F32),
        jax.ShapeDtypeStruct((m, D_MODEL), F32),
    ) + tuple(kvt_shapes)
    consts = (g1, w_in, qg, kg) + tuple(lru)
    cast_specs, cast_shapes = [], []
    bf16_rows = 2 * SUBLANES_V7X
    for w in to_bf16:
        nblk = nt
        while w.shape[0] % (nblk * bf16_rows):
            nblk //= 2
        cast_specs.append(pl.BlockSpec((w.shape[0] // nblk, w.shape[1]),
                                       lambda i, per=nt // nblk: (i // per, 0)))
        cast_shapes.append(jax.ShapeDtypeStruct(w.shape, BF16))
    return pl.pallas_call(
        functools.partial(_inproj_kernel, dilations=dilations, tail_first_steps=tuple(tail_first_steps),
                          n_cast=len(to_bf16)),
        grid=(nt,),
        in_specs=[pl.BlockSpec((tm, D_MODEL), lambda i: (0, 0)),
                  pl.BlockSpec((tm, D_MODEL), lambda i: (jnp.minimum(i + 1, nt - 1), 0))]
        + [_const_spec(c.shape) for c in consts] + cast_specs,
        out_specs=(tuple(qkv_specs) + rnn_specs + (row(D_MODEL), row(D_MODEL)) + tuple(kvt_specs)
                   + tuple(cast_specs)),
        out_shape=out_shape + tuple(cast_shapes),
        scratch_shapes=scratch,
        compiler_params=_params(("arbitrary",)),
        name="inproj_lru",
    )(x, x, *consts, *to_bf16)


def _sample_inproj_kernel(x_ref, g1_ref, qg_ref, kg_ref, w_ref, proj_ref, wb_ref):
    block = pl.program_id(0)
    wb = w_ref[...].astype(BF16)
    wb_ref[...] = wb
    t = _mm(_rms_norm_rows(x_ref[...], g1_ref[...]).astype(BF16), wb)
    normed = _head_rms_norm(t)
    proj_ref[...] = jnp.where(block < N_GROUPS, normed * qg_ref[...],
                              jnp.where(block < 2 * N_GROUPS, normed * kg_ref[...], t))


def _sample_inproj(x, g1, w_in, qg, kg):
    nb = x.shape[0]
    n_blocks = w_in.shape[1] // GROUP_W
    consts = (x, g1, qg, kg)
    col = lambda rows: pl.BlockSpec((rows, GROUP_W), lambda j: (0, j))
    return pl.pallas_call(
        _sample_inproj_kernel,
        grid=(n_blocks,),
        in_specs=[_const_spec(c.shape) for c in consts] + [col(D_MODEL)],
        out_specs=(col(nb), col(D_MODEL)),
        out_shape=(jax.ShapeDtypeStruct((nb, w_in.shape[1]), F32),
                   jax.ShapeDtypeStruct(w_in.shape, BF16)),
        compiler_params=_params(("arbitrary",)),
        name="inproj_sample",
    )(*consts, w_in)


ATTN_BLOCKS_PER_STEP = 8


def _attn_prompt_kernel(q_ref, kc_ref, kp_ref, vc_ref, vp_ref, *refs, sample_first_row):
    if sample_first_row is None:
        o_ref, lse_ref = refs
    else:
        qs_ref, ks_ref, vs_ref, c0_ref, c1_ref, c2_ref, o_ref, lse_ref, os_ref = refs
        flat_step = pl.program_id(0) * pl.num_programs(1) + pl.program_id(1)
        _attn_sample_row(flat_step, sample_first_row + flat_step, qs_ref, ks_ref, vs_ref,
                         (c0_ref, c1_ref, c2_ref), os_ref)
    step = pl.program_id(1)
    qi = lax.broadcasted_iota(jnp.int32, (BLOCK, 2 * BLOCK), 0)
    kj = lax.broadcasted_iota(jnp.int32, (BLOCK, 2 * BLOCK), 1)
    dist = BLOCK + qi - kj
    band = (dist >= 0) & (dist <= BLOCK)
    first_band = band & ((step > 0) | (kj >= BLOCK))
    band2 = jnp.concatenate([band, band], axis=0)
    first_band2 = jnp.concatenate([first_band, first_band], axis=0)
    first_head = lax.broadcasted_iota(jnp.int32, (1, LANES_V7X), 1) < HEAD_DIM
    zero = jnp.zeros((), BF16)

    for blk in range(q_ref.shape[0] // BLOCK):
        rows = slice(blk * BLOCK, (blk + 1) * BLOCK)
        prev_rows = slice((blk - 1) * BLOCK, blk * BLOCK)
        valid = band2 if blk else first_band2
        lse_ref[rows, :] = jnp.zeros((BLOCK, LANES_V7X), F32)
        for pair in range(HEADS_PER_GROUP // 2):
            cols = slice(pair * LANES_V7X, (pair + 1) * LANES_V7X)
            qp = q_ref[rows, cols]
            k_prev = kc_ref[prev_rows, cols] if blk else kp_ref[:, cols]
            v_prev = vc_ref[prev_rows, cols] if blk else vp_ref[:, cols]
            kk = jnp.concatenate([k_prev, kc_ref[rows, cols]], axis=0)
            vv = jnp.concatenate([v_prev, vc_ref[rows, cols]], axis=0)
            q2 = jnp.concatenate([jnp.where(first_head, qp, zero), jnp.where(first_head, zero, qp)], axis=0)
            s = jnp.where(valid, _mm_nt(q2, kk), NEG_INF)
            mx = jnp.max(s, axis=-1, keepdims=True)
            p = jnp.exp2(s - mx)
            den = jnp.sum(p, axis=-1, keepdims=True)
            pv = _mm(p.astype(BF16), vv)
            o_ref[rows, cols] = jnp.where(first_head, pv[:BLOCK], pv[BLOCK:]).astype(o_ref.dtype)
            for e, (lo, hi) in enumerate(((0, BLOCK), (BLOCK, 2 * BLOCK))):
                head = 2 * pair + e
                lse_ref[rows, head:head + 1] = mx[lo:hi]
                lse_ref[rows, HEADS_PER_GROUP + head:HEADS_PER_GROUP + head + 1] = den[lo:hi]


def _attn_prompt(q, k, v, g, dilation, sample=None):
    m_len = q.shape[0]
    nbs = ATTN_BLOCKS_PER_STEP
    rows = nbs * BLOCK
    n_inner = m_len // rows
    cur = pl.BlockSpec((rows, GROUP_W), lambda r, b: (b, r))
    prev = pl.BlockSpec((BLOCK, GROUP_W), lambda r, b: (jnp.maximum(b * nbs - 1, 0), r))
    in_specs, args = [cur, cur, prev, cur, prev], [q, k, k, v, v]
    out_specs = [pl.BlockSpec((rows, GROUP_W), lambda r, b: (b, r)),
                 pl.BlockSpec((rows, LANES_V7X), lambda r, b: (b, r))]
    out_shape = [jax.ShapeDtypeStruct((m_len, dilation * GROUP_W), BF16),
                 jax.ShapeDtypeStruct((m_len, dilation * LANES_V7X), F32)]
    first_row = None
    if sample is not None:
        qs, ks, vs, caches, first_row = sample
        steps = dilation * n_inner
        full = pl.BlockSpec(qs.shape, lambda r, b: (0, 0, 0))
        in_specs += [full, full, full] + [
            pl.BlockSpec((None,) + c.shape[1:], lambda r, b: (first_row + r * n_inner + b, 0, 0))
            for c in caches]
        args += [qs, ks, vs, *caches]
        out_specs.append(pl.BlockSpec((steps, GROUP_W), lambda r, b: (0, 0)))
        out_shape.append(jax.ShapeDtypeStruct((steps, GROUP_W), F32))
    return pl.pallas_call(
        functools.partial(_attn_prompt_kernel, sample_first_row=first_row),
        grid=(dilation, n_inner),
        in_specs=in_specs,
        out_specs=tuple(out_specs),
        out_shape=tuple(out_shape),
        compiler_params=_params(("arbitrary", "arbitrary")),
        name=f"attn_prompt_g{g}",
    )(*args)


def _attn_sample_row(out_row, b, q_ref, k_ref, v_ref, cache_refs, o_ref):
    head_row = lax.broadcasted_iota(jnp.int32, (HEADS_PER_GROUP, GROUP_W), 0)
    head_lane = lax.broadcasted_iota(jnp.int32, (HEADS_PER_GROUP, GROUP_W), 1) // HEAD_DIM
    own = head_row == head_lane
    parts = []
    for g, c_ref in enumerate(cache_refs):
        window, dilation = GROUPS[g]
        bf = lambda t: t.astype(BF16).astype(F32)
        qmat = jnp.where(own, jnp.broadcast_to(q_ref[g, pl.ds(b, 1), :], (HEADS_PER_GROUP, GROUP_W)), 0.0)
        qmat = qmat.astype(BF16)
        knew = bf(k_ref[g, pl.ds(b, 1), :])
        vnew = bf(v_ref[g, pl.ds(b, 1), :])
        pos = lax.broadcasted_iota(jnp.int32, (HEADS_PER_GROUP, window), 1)
        s = _mm(qmat, c_ref[0:GROUP_W, :].astype(BF16))
        s = jnp.where((pos & (dilation - 1)) == 0, s, NEG_INF)
        s_new = jnp.sum(qmat.astype(F32) * knew, axis=-1, keepdims=True)
        mx = jnp.maximum(jnp.max(s, axis=-1, keepdims=True), s_new)
        p = jnp.exp2(s - mx)
        p_new = jnp.exp2(s_new - mx)
        den = jnp.sum(p, axis=-1, keepdims=True) + p_new
        numer = _mm_nt(p.astype(BF16), c_ref[GROUP_W:2 * GROUP_W, :].astype(BF16)) + bf(p_new) * vnew
        parts.append((jnp.where(own, numer, 0.0), mx, den))
    m_all = functools.reduce(jnp.maximum, [mx for _, mx, _ in parts])
    ws = [jnp.exp2(mx - m_all) for _, mx, _ in parts]
    num = sum(n * w for (n, _, _), w in zip(parts, ws))
    den = sum(d * w for (_, _, d), w in zip(parts, ws))
    o_ref[pl.ds(out_row, 1), :] = jnp.sum(num / den, axis=0, keepdims=True)


def _gelu_tanh(x):
    cdf = 0.5 * (1.0 + jnp.tanh(0.7978845608028654 * (x + 0.044715 * (x * x * x))))
    return x * cdf


def _softplus(x):
    return jnp.maximum(x, 0.0) + jnp.log1p(jnp.exp(-jnp.abs(x)))


LRU_CHUNK = 2 * RNN_BLOCK_W
N_LRU_CHUNKS = D_RNN // LRU_CHUNK


def _lru_lanes(n):
    return slice(n * LRU_CHUNK, (n + 1) * LRU_CHUNK)


def _lru_coeffs(xc, n, wgate_ref, brg_ref, big_ref, lam_ref):
    lanes = _lru_lanes(n)
    sigmoid = lambda v: 0.5 + 0.5 * jnp.tanh(0.5 * v)
    logits = _mm(xc.astype(BF16), wgate_ref[n])
    r = sigmoid(logits[:, :LRU_CHUNK] + brg_ref[:, lanes])
    i = sigmoid(logits[:, LRU_CHUNK:] + big_ref[:, lanes])
    neg_rate = LRU_C * _softplus(-lam_ref[:, lanes])
    a = jnp.exp2(r * (-LOG2_E * neg_rate))
    t = jnp.tanh(r * neg_rate)
    y = 2.0 * t / (1.0 + t)
    root = jnp.where(y > 0.0, y * lax.rsqrt(y), 0.0)
    return a, root * i * xc


def _gate_weights(w_rg, w_ig):
    def pairs(w):
        z = jnp.zeros_like(w[0::2])
        top = jnp.concatenate([w[0::2], z], axis=2)
        bot = jnp.concatenate([z, w[1::2]], axis=2)
        return jnp.concatenate([top, bot], axis=1)
    return jnp.concatenate([pairs(w_rg), pairs(w_ig)], axis=2)


def _lru_tile(xb, gb, n, lru_refs, ctail_ref, hcar_ref):
    cw_ref, cb_ref, wgate_ref, brg_ref, big_ref, lam_ref = lru_refs
    lanes = _lru_lanes(n)
    sl = SUBLANES_V7X
    tm = xb.shape[0]
    nj = tm // sl
    sub = lax.broadcasted_iota(jnp.int32, (sl, LRU_CHUNK), 0)
    vrow = lambda t, j: t[j * sl:(j + 1) * sl]

    taps = CONV_W - 1
    wrap = [pltpu.roll(jnp.where(sub == sl - 1, vrow(ctail_ref[:, lanes], i), vrow(xb, nj - taps + i)), 1, 0)
            for i in range(taps)]
    ctail_ref[:, lanes] = xb[tm - taps * sl:]
    xc = cb_ref[:, lanes] + xb * cw_ref[taps:taps + 1, lanes]
    for k in range(1, CONV_W):
        shifted = jnp.concatenate(wrap[taps - k:] + [xb[:tm - k * sl]], axis=0)
        xc = xc + shifted * cw_ref[taps - k:taps - k + 1, lanes]

    a, b = _lru_coeffs(xc, n, wgate_ref, brg_ref, big_ref, lam_ref)
    gate = _gelu_tanh(gb)

    hl, acc = vrow(b, 0), vrow(a, 0)
    hls, accs = [hl], [acc]
    for j in range(1, nj):
        hl = vrow(a, j) * hl + vrow(b, j)
        acc = vrow(a, j) * acc
        hls.append(hl)
        accs.append(acc)

    for s in (1, 2, 4):
        keep = sub >= s
        acc_prev = jnp.where(keep, pltpu.roll(acc, s, 0), 1.0)
        hl_prev = jnp.where(keep, pltpu.roll(hl, s, 0), 0.0)
        hl = acc * hl_prev + hl
        acc = acc * acc_prev
    carry = hcar_ref[:, lanes]
    h_end = hl + acc * carry
    h_in = jnp.where(sub == 0, carry, pltpu.roll(h_end, 1, 0))
    hcar_ref[:, lanes] = h_end[sl - 1:sl]

    return [(hls[j] + accs[j] * h_in) * vrow(gate, j) for j in range(nj)]


def _rglru_sample_kernel(xb_ref, gb_ref, sc_ref, h0_ref, cw_ref, cb_ref, wgate_ref, brg_ref, big_ref,
                         lam_ref, hb_ref, conv_ref, h_ref):
    xb = xb_ref[...]
    taps = [sc_ref[:, j * D_RNN:(j + 1) * D_RNN] for j in range(CONV_W - 1)] + [xb]
    xc = cb_ref[...] + sum(t * cw_ref[j:j + 1, :] for j, t in enumerate(taps))
    for n in range(N_LRU_CHUNKS):
        lanes = _lru_lanes(n)
        a, b = _lru_coeffs(xc[:, lanes], n, wgate_ref, brg_ref, big_ref, lam_ref)
        h = a * h0_ref[:, lanes] + b
        h_ref[:, lanes] = h
        hb_ref[:, lanes] = (h * _gelu_tanh(gb_ref[:, lanes])).astype(hb_ref.dtype)
    for j in range(CONV_W - 1):
        conv_ref[:, j * D_RNN:(j + 1) * D_RNN] = taps[j + 1]


def _rglru_sample(xb, gb, state_conv, h0, cw, cb, wgate, brg, big, lam):
    nb = xb.shape[0]
    args = (xb, gb, state_conv, h0, cw, cb, wgate, brg, big, lam)
    return pl.pallas_call(
        _rglru_sample_kernel,
        grid=(1,),
        in_specs=[_const_spec(a.shape) for a in args],
        out_specs=(_const_spec((nb, D_RNN)), _const_spec((nb, (CONV_W - 1) * D_RNN)),
                   _const_spec((nb, D_RNN))),
        out_shape=(jax.ShapeDtypeStruct((nb, D_RNN), BF16),
                   jax.ShapeDtypeStruct((nb, (CONV_W - 1) * D_RNN), F32),
                   jax.ShapeDtypeStruct((nb, D_RNN), F32)),
        compiler_params=_params(("arbitrary",)),
        name="rglru_sample",
    )(*args)


def _post_kernel(*refs, dilations):
    n_parts = len(dilations)
    attn_refs = refs[:2 * n_parts] if n_parts > 1 else refs[:1]
    rest = refs[len(attn_refs):]
    (hb_ref, ga_ref, gb2_ref, x_ref, expand_ref, bg_ref, woa_ref, wor_ref, wout_ref, g2_ref,
     wfi_ref, wfo_ref, y_ref) = rest[:13]
    tm = x_ref.shape[0]

    if n_parts > 1:
        o_nat_ref, lse_nat_ref = rest[13:]

        def natural(src_ref, dst_ref, width, d):
            if d == 1:
                return src_ref[...].astype(F32)
            n_slabs = width // LANES_V7X
            for r in range(d):
                for c in range(n_slabs):
                    lanes = slice(r * width + c * LANES_V7X, r * width + (c + 1) * LANES_V7X)
                    dst_ref[c, pl.ds(r, tm // d, stride=d), :] = src_ref[:, lanes].astype(F32)
            return jnp.concatenate([dst_ref[c] for c in range(n_slabs)], axis=1)

        o_refs, stat_refs = attn_refs[:n_parts], attn_refs[n_parts:]
        stats = [natural(r, lse_nat_ref, LANES_V7X, d) for r, d in zip(stat_refs, dilations)]
        m_all = functools.reduce(jnp.maximum, stats)
        ws = [jnp.exp2(st - m_all) for st in stats]
        dens = [pltpu.roll(st, LANES_V7X - HEADS_PER_GROUP, 1) for st in stats]
        inv = 1.0 / sum(w * dn for w, dn in zip(ws, dens))
        is_head = lax.broadcasted_iota(jnp.int32, (1, LANES_V7X), 1) < HEADS_PER_GROUP
        attn = 0.0
        for o_ref, w, d in zip(o_refs, ws, dilations):
            cw = jnp.where(is_head, w * inv, 0.0)
            hi = cw.astype(BF16)
            lo = (cw - hi.astype(F32)).astype(BF16)
            wide = _mm(jnp.concatenate([hi, lo], axis=1), expand_ref[...])
            attn = attn + wide * natural(o_ref, o_nat_ref, GROUP_W, d)
    else:
        attn = attn_refs[0][...]

    ya = _mm(attn.astype(BF16), woa_ref[...])
    yb = _mm(hb_ref[...], wor_ref[...])
    merged = (jax.nn.sigmoid(ga_ref[...] + bg_ref[0:1, :]) * ya
              + jax.nn.sigmoid(gb2_ref[...] + bg_ref[1:2, :]) * yb)
    x1 = x_ref[...] + _mm(merged.astype(BF16), wout_ref[...])
    hn2 = _rms_norm_rows(x1, g2_ref[...]).astype(BF16)
    gu = _mm(hn2, wfi_ref[...])
    act = jax.nn.silu(gu[:, :D_FF]) * gu[:, D_FF:]
    y_ref[...] = x1 + _mm(act.astype(BF16), wfo_ref[...])


def _post(attn_parts, dilations, hb, ga, gb2, x, expand, bg, woa, wor, wout, g2, wfi, wfo, *, tm):
    m = x.shape[0]
    row = lambda w: pl.BlockSpec((tm, w), lambda i: (i, 0))
    n_parts = len(attn_parts)
    scratch = []
    if n_parts > 1:
        blocked = lambda w, d: pl.BlockSpec((tm // d, d * w), lambda i: (i, 0))
        attn_args = [o for o, _ in attn_parts] + [l for _, l in attn_parts]
        attn_specs = ([blocked(GROUP_W, d) for d in dilations]
                      + [blocked(LANES_V7X, d) for d in dilations])
        scratch = [pltpu.VMEM((GROUP_W // LANES_V7X, tm, LANES_V7X), F32),
                   pltpu.VMEM((1, tm, LANES_V7X), F32)]
    else:
        attn_args, attn_specs = list(attn_parts), [row(GROUP_W)]
    consts = (expand, bg, woa, wor, wout, g2, wfi, wfo)
    return pl.pallas_call(
        functools.partial(_post_kernel, dilations=dilations),
        grid=(m // tm,),
        in_specs=attn_specs + [row(D_RNN), row(D_MODEL), row(D_MODEL), row(D_MODEL)]
        + [_const_spec(c.shape) for c in consts],
        out_specs=row(D_MODEL),
        out_shape=jax.ShapeDtypeStruct((m, D_MODEL), F32),
        scratch_shapes=scratch,
        compiler_params=_params(("arbitrary",)),
        name="post_prompt" if n_parts > 1 else "post_sample",
    )(*attn_args, hb, ga, gb2, x, *consts)


def kernel(x_prompt, x_sample, cache_kv_w128, cache_kv_w512, cache_kv_w2048, state_conv, state_h,
           norm1_g, w_in, b_gate, q_norm_g, k_norm_g, conv_w, conv_b, w_rg, b_rg, w_ig, b_ig,
           lru_lambda, w_o_attn, w_o_rnn, w_out, norm2_g, w_ffn_in, w_ffn_out):
    assert x_prompt.shape[0] == 1 and norm1_g.shape[0] == 1 and x_sample.shape[1] == 1
    seq = x_prompt.shape[1]
    nb = x_sample.shape[0]
    layer = 0

    expand = (jnp.arange(LANES_V7X)[:, None] == (jnp.arange(GROUP_W) // HEAD_DIM)[None, :]).astype(BF16)
    expand = jnp.concatenate([expand, expand], axis=0)

    row2 = lambda t: t[layer].reshape(1, -1)
    g1, g2 = row2(norm1_g), row2(norm2_g)
    qg = jnp.tile(q_norm_g[layer], HEADS_PER_GROUP).reshape(1, GROUP_W) * (ATTN_SCALE * LOG2_E)
    kg = jnp.tile(k_norm_g[layer], HEADS_PER_GROUP).reshape(1, GROUP_W)
    wgate = _gate_weights(w_rg[layer].astype(BF16), w_ig[layer].astype(BF16))
    lru = (conv_w[layer], row2(conv_b), wgate, row2(b_rg), row2(b_ig), row2(lru_lambda))
    post_f32 = tuple(w[layer] for w in (w_o_attn, w_o_rnn, w_out, w_ffn_in, w_ffn_out))

    xs = x_sample[:, 0]
    proj, w_in_b = _sample_inproj(xs, g1, w_in[layer], qg, kg)
    qkvs = proj[:, :OFF_XB].reshape(nb, 3 * N_GROUPS, GROUP_W).transpose(1, 0, 2)
    qs, ks, vs = qkvs[:N_GROUPS], qkvs[N_GROUPS:2 * N_GROUPS], qkvs[2 * N_GROUPS:]
    xbs, gbs, gas, gb2s = (proj[:, a:b] for a, b in
                           ((OFF_XB, OFF_GB), (OFF_GB, OFF_GA), (OFF_GA, OFF_GB2), (OFF_GB2, proj.shape[1])))
    caches = [jnp.transpose(c[layer], (0, 2, 3, 4, 1)).reshape(nb, 2 * GROUP_W, c.shape[2])
              for c in (cache_kv_w128, cache_kv_w512, cache_kv_w2048)]

    xp = x_prompt[0]
    *qkv, hb, conv_rows, h_last, ga, gb2, kvp0, kvp1, kvp2, woa, wor, wout, wfi, wfo = _inproj(
        xp, g1, w_in_b, qg, kg, lru, post_f32, tm=PROMPT_ROW_TILE,
        tails=tuple(min(w, seq) for w, _ in GROUPS), dilations=DILATIONS)
    post_w = (expand, b_gate[layer], woa, wor, wout, g2, wfi, wfo)
    half = nb // 2
    parts, attn_s = [], []
    for g, d in enumerate(DILATIONS):
        sample = (qs, ks, vs, caches, (g - 1) * half) if g else None
        *part, = _attn_prompt(*qkv[3 * g:3 * g + 3], g, d, sample)
        parts.append(part[:2])
        attn_s += part[2:]
    attn_s = jnp.concatenate(attn_s, axis=0)
    assert attn_s.shape == (nb, GROUP_W)
    y_p = _post(parts, DILATIONS, hb, ga, gb2, xp, *post_w, tm=PROMPT_ROW_TILE)
    conv_p = conv_rows[SUBLANES_V7X - 1::SUBLANES_V7X]

    hbs, conv_s, h_s = _rglru_sample(xbs, gbs, state_conv[layer].reshape(nb, -1), state_h[layer], *lru)
    y_s = _post([attn_s], (1,), hbs, gas, gb2s, xs, *post_w, tm=nb)

    kv_prompt = [t.reshape(1, 1, t.shape[0], 2, HEADS_PER_GROUP, HEAD_DIM) for t in (kvp0, kvp1, kvp2)]
    kv_sample = [jnp.stack([ks[g], vs[g]], axis=1).reshape(1, nb, 1, 2, HEADS_PER_GROUP, HEAD_DIM)
                 for g in range(N_GROUPS)]
    return (y_p[None], y_s[:, None],
            kv_prompt[0], kv_prompt[1], kv_prompt[2],
            conv_p[None, None], h_last[None],
            kv_sample[0], kv_sample[1], kv_sample[2],
            conv_s.reshape(1, nb, CONV_W - 1, D_RNN), h_s[None])
```

```python
import functools

import jax
import jax.numpy as jnp
from jax import lax
from jax.experimental import pallas as pl
from jax.experimental.pallas import tpu as pltpu

F32 = jnp.float32
BF16 = jnp.bfloat16

D_MODEL = 1024
HEAD_DIM = 64
HEADS_PER_GROUP = 8
GROUPS = ((128, 1), (512, 4), (2048, 16))
DILATIONS = tuple(d for _, d in GROUPS)
N_GROUPS = len(GROUPS)
GROUP_W = HEADS_PER_GROUP * HEAD_DIM
QKV_WIDTH = N_GROUPS * GROUP_W
BLOCK = 128
ATTN_SCALE = HEAD_DIM ** -0.5
LOG2_E = 1.4426950408889634
NEG_INF = -1e30
D_RNN = 1280
RNN_BLOCKS = 10
RNN_BLOCK_W = D_RNN // RNN_BLOCKS
CONV_W = 4
LRU_C = 8.0
D_FF = 2816
RMS_EPS = 1e-6

LANES_V7X = 128
SUBLANES_V7X = 8
MXU_DIM_V7X = 256
VMEM_LIMIT_BYTES = 56 * 1024 * 1024
PROMPT_ROW_TILE = MXU_DIM_V7X

OFF_Q, OFF_K, OFF_V = 0, QKV_WIDTH, 2 * QKV_WIDTH
OFF_XB = 3 * QKV_WIDTH
OFF_GB = OFF_XB + D_RNN
OFF_GA = OFF_GB + D_RNN
OFF_GB2 = OFF_GA + D_MODEL


def _mm(a, b):
    return jnp.dot(a, b, preferred_element_type=F32)


def _mm_nt(a, b):
    return lax.dot_general(a, b, (((1,), (1,)), ((), ())), preferred_element_type=F32)


def _rms_norm_rows(x, g):
    return x * lax.rsqrt(jnp.mean(x * x, axis=-1, keepdims=True) + RMS_EPS) * g


HEAD_NORM_GAIN = HEAD_DIM ** 0.5


def _head_rms_norm(t):
    first_head = lax.broadcasted_iota(jnp.int32, (1, LANES_V7X), 1) < HEAD_DIM
    cols = []
    for c in range(0, t.shape[1], LANES_V7X):
        x = t[:, c:c + LANES_V7X]
        xx = x * x
        s0 = jnp.sum(jnp.where(first_head, xx, 0.0), axis=-1, keepdims=True)
        s1 = jnp.sum(jnp.where(first_head, 0.0, xx), axis=-1, keepdims=True)
        cols.append(x * lax.rsqrt(jnp.where(first_head, s0, s1) + HEAD_DIM * RMS_EPS))
    return jnp.concatenate(cols, axis=1)


def _const_spec(shape):
    nd = len(shape)
    return pl.BlockSpec(shape, lambda *_: (0,) * nd, pipeline_mode=pl.Buffered(1))


def _params(sem):
    return pltpu.CompilerParams(dimension_semantics=sem, vmem_limit_bytes=VMEM_LIMIT_BYTES)


def _inproj_kernel(x0_ref, xn_ref, g1_ref, w_ref, qg_ref, kg_ref, *refs, dilations, tail_first_steps,
                   n_cast):
    lru_refs, refs = refs[:6], refs[6:]
    cast_in, refs = refs[:n_cast], refs[n_cast:]
    qkv_refs, refs = refs[:3 * N_GROUPS], refs[3 * N_GROUPS:]
    (hb_ref, conv_ref, hlast_ref, ga_ref, gb2_ref), refs = refs[:5], refs[5:]
    kvt_refs, refs = refs[:N_GROUPS], refs[N_GROUPS:]
    cast_out, refs = refs[:n_cast], refs[n_cast:]
    hn_ref, hbs_ref, hnp_ref, ctail_ref, hcar_ref = refs
    for src_ref, dst_ref in zip(cast_in, cast_out):
        dst_ref[...] = src_ref[...].astype(dst_ref.dtype)
    step = pl.program_id(0)
    tm = xn_ref.shape[0]
    n_slabs = D_MODEL // LANES_V7X
    nj = tm // SUBLANES_V7X
    pitch = nj + SUBLANES_V7X

    def stage_normed_rows(x_ref):
        hn32 = _rms_norm_rows(x_ref[...], g1_ref[...])
        for c in range(n_slabs):
            lanes = slice(c * LANES_V7X, (c + 1) * LANES_V7X)
            hn_ref[c] = hn32[:, lanes]
            for s in range(SUBLANES_V7X):
                hnp_ref[c, s * pitch:s * pitch + nj, :] = hn32[s * nj:(s + 1) * nj, lanes]

    @pl.when(step == 0)
    def _():
        ctail_ref[...] = jnp.zeros(ctail_ref.shape, F32)
        hcar_ref[...] = jnp.zeros(hcar_ref.shape, F32)
        stage_normed_rows(x0_ref)

    hn = jnp.concatenate([hn_ref[c] for c in range(n_slabs)], axis=1).astype(BF16)

    def strided_rows(start, size, stride):
        return jnp.concatenate([hn_ref[c, pl.ds(start, size, stride=stride), :] for c in range(n_slabs)],
                               axis=1)

    ht = jnp.concatenate(
        [jnp.concatenate([hnp_ref[c, pl.ds(j, SUBLANES_V7X, stride=pitch), :] for c in range(n_slabs)],
                         axis=1) for j in range(nj)], axis=0).astype(BF16)
    per_vreg = LRU_CHUNK // LANES_V7X
    xg = _mm(ht, w_ref[:, OFF_XB:OFF_GA])
    for n in range(N_LRU_CHUNKS):
        c0 = n * LRU_CHUNK
        hb_rows = _lru_tile(xg[:, c0:c0 + LRU_CHUNK], xg[:, D_RNN + c0:D_RNN + c0 + LRU_CHUNK],
                            n, lru_refs, ctail_ref, hcar_ref)
        for j, rows in enumerate(hb_rows):
            for c in range(per_vreg):
                hbs_ref[n * per_vreg + c, pl.ds(j, SUBLANES_V7X, stride=pitch), :] = (
                    rows[:, c * LANES_V7X:(c + 1) * LANES_V7X])
    conv_ref[...] = ctail_ref[...]
    hlast_ref[...] = hcar_ref[...]
    hb_ref[...] = jnp.concatenate(
        [jnp.concatenate([hbs_ref[c, s * pitch:s * pitch + nj, :] for s in range(SUBLANES_V7X)], axis=0)
         for c in range(D_RNN // LANES_V7X)], axis=1).astype(hb_ref.dtype)

    def head_norm(t, gain):
        return _head_rms_norm(t) * gain

    def qkv(h, g):
        c = g * GROUP_W
        qn = head_norm(_mm(h, w_ref[:, OFF_Q + c:OFF_Q + c + GROUP_W]), qg_ref[...])
        kn = head_norm(_mm(h, w_ref[:, OFF_K + c:OFF_K + c + GROUP_W]), kg_ref[...])
        vv = _mm(h, w_ref[:, OFF_V + c:OFF_V + c + GROUP_W])
        return qn, kn, vv

    in_order = {}
    for g, d in enumerate(dilations):
        rows = tm // d
        if d == 1:
            hg = hn
        else:
            hg = jnp.concatenate([strided_rows(r, rows, d) for r in range(d)], axis=0).astype(BF16)
        parts = qkv(hg, g)
        if d == 1:
            in_order[g] = parts
        for t, o_ref in zip(parts, qkv_refs[3 * g:3 * g + 3]):
            for r in range(d):
                o_ref[:, r * GROUP_W:(r + 1) * GROUP_W] = t[r * rows:(r + 1) * rows].astype(o_ref.dtype)

    gates = _mm(hn, w_ref[:, OFF_GA:OFF_GB2 + D_MODEL])
    ga_ref[...] = gates[:, :D_MODEL]
    gb2_ref[...] = gates[:, D_MODEL:]

    stage_normed_rows(xn_ref)

    for g, (kvt_ref, first_step) in enumerate(zip(kvt_refs, tail_first_steps)):
        @pl.when(step >= first_step)
        def _():
            _, kn, vv = in_order[g] if g in in_order else qkv(hn, g)
            keep = kvt_ref.shape[0]
            kvt_ref[:, 0:GROUP_W] = kn[tm - keep:]
            kvt_ref[:, GROUP_W:2 * GROUP_W] = vv[tm - keep:]


def _inproj(x, g1, w_in, qg, kg, lru, to_bf16, *, tm, tails, dilations):
    m = x.shape[0]
    nt = m // tm
    row = lambda w: pl.BlockSpec((tm, w), lambda i: (i, 0))
    fixed = lambda r, w: pl.BlockSpec((r, w), lambda i: (0, 0))
    qkv_specs, qkv_shapes = [], []
    for d in dilations:
        qkv_specs += [pl.BlockSpec((tm // d, d * GROUP_W), lambda i: (i, 0))] * 3
        qkv_shapes += [jax.ShapeDtypeStruct((m // d, d * GROUP_W), BF16)] * 3
    kvt_specs, kvt_shapes, tail_first_steps = [], [], []
    for rows in tails:
        blk = min(rows, tm)
        first = nt - rows // blk
        kvt_specs.append(pl.BlockSpec((blk, 2 * GROUP_W), lambda i, first=first: (jnp.maximum(i - first, 0), 0)))
        kvt_shapes.append(jax.ShapeDtypeStruct((rows, 2 * GROUP_W), F32))
        tail_first_steps.append(first)
    tail_rows = (CONV_W - 1) * SUBLANES_V7X
    rnn_specs = (row(D_RNN), fixed(tail_rows, D_RNN), fixed(1, D_RNN))
    rnn_shapes = (jax.ShapeDtypeStruct((m, D_RNN), BF16),
                  jax.ShapeDtypeStruct((tail_rows, D_RNN), F32),
                  jax.ShapeDtypeStruct((1, D_RNN), F32))
    padded = tm + SUBLANES_V7X * SUBLANES_V7X
    scratch = [pltpu.VMEM((D_MODEL // LANES_V7X, tm, LANES_V7X), F32),
               pltpu.VMEM((D_RNN // LANES_V7X, padded, LANES_V7X), F32),
               pltpu.VMEM((D_MODEL // LANES_V7X, padded, LANES_V7X), F32),
               pltpu.VMEM((tail_rows, D_RNN), F32),
               pltpu.VMEM((1, D_RNN), F32)]
    out_shape = tuple(qkv_shapes) + rnn_shapes + (
        jax.ShapeDtypeStruct((m, D_MODEL), F32),
        jax.ShapeDtypeStruct((m, D_MODEL), F32),
    ) + tuple(kvt_shapes)
    consts = (g1, w_in, qg, kg) + tuple(lru)
    cast_specs, cast_shapes = [], []
    bf16_rows = 2 * SUBLANES_V7X
    for w in to_bf16:
        nblk = nt
        while w.shape[0] % (nblk * bf16_rows):
            nblk //= 2
        cast_specs.append(pl.BlockSpec((w.shape[0] // nblk, w.shape[1]),
                                       lambda i, per=nt // nblk: (i // per, 0)))
        cast_shapes.append(jax.ShapeDtypeStruct(w.shape, BF16))
    return pl.pallas_call(
        functools.partial(_inproj_kernel, dilations=dilations, tail_first_steps=tuple(tail_first_steps),
                          n_cast=len(to_bf16)),
        grid=(nt,),
        in_specs=[pl.BlockSpec((tm, D_MODEL), lambda i: (0, 0)),
                  pl.BlockSpec((tm, D_MODEL), lambda i: (jnp.minimum(i + 1, nt - 1), 0))]
        + [_const_spec(c.shape) for c in consts] + cast_specs,
        out_specs=(tuple(qkv_specs) + rnn_specs + (row(D_MODEL), row(D_MODEL)) + tuple(kvt_specs)
                   + tuple(cast_specs)),
        out_shape=out_shape + tuple(cast_shapes),
        scratch_shapes=scratch,
        compiler_params=_params(("arbitrary",)),
        name="inproj_lru",
    )(x, x, *consts, *to_bf16)


def _sample_inproj_kernel(x_ref, g1_ref, qg_ref, kg_ref, w_ref, proj_ref, wb_ref):
    wb = w_ref[...].astype(BF16)
    wb_ref[...] = wb
    t = _mm(_rms_norm_rows(x_ref[...], g1_ref[...]).astype(BF16), wb)
    for part in range(w_ref.shape[1] // GROUP_W):
        block = pl.program_id(0) * (w_ref.shape[1] // GROUP_W) + part
        cols = slice(part * GROUP_W, (part + 1) * GROUP_W)
        normed = _head_rms_norm(t[:, cols])
        proj_ref[:, cols] = jnp.where(block < N_GROUPS, normed * qg_ref[...],
                                      jnp.where(block < 2 * N_GROUPS, normed * kg_ref[...], t[:, cols]))


def _sample_inproj(x, g1, w_in, qg, kg):
    nb = x.shape[0]
    width = 2 * GROUP_W
    consts = (x, g1, qg, kg)
    col = lambda rows: pl.BlockSpec((rows, width), lambda j: (0, j))
    return pl.pallas_call(
        _sample_inproj_kernel,
        grid=(w_in.shape[1] // width,),
        in_specs=[_const_spec(c.shape) for c in consts] + [col(D_MODEL)],
        out_specs=(col(nb), col(D_MODEL)),
        out_shape=(jax.ShapeDtypeStruct((nb, w_in.shape[1]), F32),
                   jax.ShapeDtypeStruct(w_in.shape, BF16)),
        compiler_params=_params(("arbitrary",)),
        name="inproj_sample",
    )(*consts, w_in)


ATTN_BLOCKS_PER_STEP = 8


def _attn_prompt_kernel(q_ref, kc_ref, kp_ref, vc_ref, vp_ref, *refs, sample_first_row):
    if sample_first_row is None:
        o_ref, lse_ref = refs
    else:
        qs_ref, ks_ref, vs_ref, c0_ref, c1_ref, c2_ref, o_ref, lse_ref, os_ref = refs
        flat_step = pl.program_id(0) * pl.num_programs(1) + pl.program_id(1)
        _attn_sample_row(flat_step, sample_first_row + flat_step, qs_ref, ks_ref, vs_ref,
                         (c0_ref, c1_ref, c2_ref), os_ref)
    step = pl.program_id(1)
    qi = lax.broadcasted_iota(jnp.int32, (BLOCK, 2 * BLOCK), 0)
    kj = lax.broadcasted_iota(jnp.int32, (BLOCK, 2 * BLOCK), 1)
    dist = BLOCK + qi - kj
    band = (dist >= 0) & (dist <= BLOCK)
    first_band = band & ((step > 0) | (kj >= BLOCK))
    band2 = jnp.concatenate([band, band], axis=0)
    first_band2 = jnp.concatenate([first_band, first_band], axis=0)
    first_head = lax.broadcasted_iota(jnp.int32, (1, LANES_V7X), 1) < HEAD_DIM
    zero = jnp.zeros((), BF16)

    for blk in range(q_ref.shape[0] // BLOCK):
        rows = slice(blk * BLOCK, (blk + 1) * BLOCK)
        prev_rows = slice((blk - 1) * BLOCK, blk * BLOCK)
        valid = band2 if blk else first_band2
        lse_ref[rows, :] = jnp.zeros((BLOCK, LANES_V7X), F32)
        for pair in range(HEADS_PER_GROUP // 2):
            cols = slice(pair * LANES_V7X, (pair + 1) * LANES_V7X)
            qp = q_ref[rows, cols]
            k_prev = kc_ref[prev_rows, cols] if blk else kp_ref[:, cols]
            v_prev = vc_ref[prev_rows, cols] if blk else vp_ref[:, cols]
            kk = jnp.concatenate([k_prev, kc_ref[rows, cols]], axis=0)
            vv = jnp.concatenate([v_prev, vc_ref[rows, cols]], axis=0)
            q2 = jnp.concatenate([jnp.where(first_head, qp, zero), jnp.where(first_head, zero, qp)], axis=0)
            s = jnp.where(valid, _mm_nt(q2, kk), NEG_INF)
            mx = jnp.max(s, axis=-1, keepdims=True)
            p = jnp.exp2(s - mx)
            den = jnp.sum(p, axis=-1, keepdims=True)
            pv = _mm(p.astype(BF16), vv)
            o_ref[rows, cols] = jnp.where(first_head, pv[:BLOCK], pv[BLOCK:]).astype(o_ref.dtype)
            for e, (lo, hi) in enumerate(((0, BLOCK), (BLOCK, 2 * BLOCK))):
                head = 2 * pair + e
                lse_ref[rows, head:head + 1] = mx[lo:hi]
                lse_ref[rows, HEADS_PER_GROUP + head:HEADS_PER_GROUP + head + 1] = den[lo:hi]


def _attn_prompt(q, k, v, g, dilation, sample=None):
    m_len = q.shape[0]
    nbs = ATTN_BLOCKS_PER_STEP
    rows = nbs * BLOCK
    n_inner = m_len // rows
    cur = pl.BlockSpec((rows, GROUP_W), lambda r, b: (b, r))
    prev = pl.BlockSpec((BLOCK, GROUP_W), lambda r, b: (jnp.maximum(b * nbs - 1, 0), r))
    in_specs, args = [cur, cur, prev, cur, prev], [q, k, k, v, v]
    out_specs = [pl.BlockSpec((rows, GROUP_W), lambda r, b: (b, r)),
                 pl.BlockSpec((rows, LANES_V7X), lambda r, b: (b, r))]
    out_shape = [jax.ShapeDtypeStruct((m_len, dilation * GROUP_W), BF16),
                 jax.ShapeDtypeStruct((m_len, dilation * LANES_V7X), F32)]
    first_row = None
    if sample is not None:
        qs, ks, vs, caches, first_row = sample
        steps = dilation * n_inner
        full = pl.BlockSpec(qs.shape, lambda r, b: (0, 0, 0))
        in_specs += [full, full, full] + [
            pl.BlockSpec((None,) + c.shape[1:], lambda r, b: (first_row + r * n_inner + b, 0, 0))
            for c in caches]
        args += [qs, ks, vs, *caches]
        out_specs.append(pl.BlockSpec((steps, GROUP_W), lambda r, b: (0, 0)))
        out_shape.append(jax.ShapeDtypeStruct((steps, GROUP_W), F32))
    return pl.pallas_call(
        functools.partial(_attn_prompt_kernel, sample_first_row=first_row),
        grid=(dilation, n_inner),
        in_specs=in_specs,
        out_specs=tuple(out_specs),
        out_shape=tuple(out_shape),
        compiler_params=_params(("arbitrary", "arbitrary")),
        name=f"attn_prompt_g{g}",
    )(*args)


def _attn_sample_row(out_row, b, q_ref, k_ref, v_ref, cache_refs, o_ref):
    head_row = lax.broadcasted_iota(jnp.int32, (HEADS_PER_GROUP, GROUP_W), 0)
    head_lane = lax.broadcasted_iota(jnp.int32, (HEADS_PER_GROUP, GROUP_W), 1) // HEAD_DIM
    own = head_row == head_lane
    parts = []
    for g, c_ref in enumerate(cache_refs):
        window, dilation = GROUPS[g]
        bf = lambda t: t.astype(BF16).astype(F32)
        qmat = jnp.where(own, jnp.broadcast_to(q_ref[g, pl.ds(b, 1), :], (HEADS_PER_GROUP, GROUP_W)), 0.0)
        qmat = qmat.astype(BF16)
        knew = bf(k_ref[g, pl.ds(b, 1), :])
        vnew = bf(v_ref[g, pl.ds(b, 1), :])
        pos = lax.broadcasted_iota(jnp.int32, (HEADS_PER_GROUP, window), 1)
        s = _mm(qmat, c_ref[0:GROUP_W, :].astype(BF16))
        s = jnp.where((pos & (dilation - 1)) == 0, s, NEG_INF)
        s_new = jnp.sum(qmat.astype(F32) * knew, axis=-1, keepdims=True)
        mx = jnp.maximum(jnp.max(s, axis=-1, keepdims=True), s_new)
        p = jnp.exp2(s - mx)
        p_new = jnp.exp2(s_new - mx)
        den = jnp.sum(p, axis=-1, keepdims=True) + p_new
        numer = _mm_nt(p.astype(BF16), c_ref[GROUP_W:2 * GROUP_W, :].astype(BF16)) + bf(p_new) * vnew
        parts.append((jnp.where(own, numer, 0.0), mx, den))
    m_all = functools.reduce(jnp.maximum, [mx for _, mx, _ in parts])
    ws = [jnp.exp2(mx - m_all) for _, mx, _ in parts]
    num = sum(n * w for (n, _, _), w in zip(parts, ws))
    den = sum(d * w for (_, _, d), w in zip(parts, ws))
    o_ref[pl.ds(out_row, 1), :] = jnp.sum(num / den, axis=0, keepdims=True)


def _gelu_tanh(x):
    cdf = 0.5 * (1.0 + jnp.tanh(0.7978845608028654 * (x + 0.044715 * (x * x * x))))
    return x * cdf


def _softplus(x):
    return jnp.maximum(x, 0.0) + jnp.log1p(jnp.exp(-jnp.abs(x)))


LRU_CHUNK = 2 * RNN_BLOCK_W
N_LRU_CHUNKS = D_RNN // LRU_CHUNK


def _lru_lanes(n):
    return slice(n * LRU_CHUNK, (n + 1) * LRU_CHUNK)


def _lru_coeffs(xc, n, wgate_ref, brg_ref, big_ref, lam_ref):
    lanes = _lru_lanes(n)
    sigmoid = lambda v: 0.5 + 0.5 * jnp.tanh(0.5 * v)
    logits = _mm(xc.astype(BF16), wgate_ref[n])
    r = sigmoid(logits[:, :LRU_CHUNK] + brg_ref[:, lanes])
    i = sigmoid(logits[:, LRU_CHUNK:] + big_ref[:, lanes])
    neg_rate = LRU_C * _softplus(-lam_ref[:, lanes])
    a = jnp.exp2(r * (-LOG2_E * neg_rate))
    t = jnp.tanh(r * neg_rate)
    y = 2.0 * t / (1.0 + t)
    root = jnp.where(y > 0.0, y * lax.rsqrt(y), 0.0)
    return a, root * i * xc


def _gate_weights(w_rg, w_ig):
    def pairs(w):
        z = jnp.zeros_like(w[0::2])
        top = jnp.concatenate([w[0::2], z], axis=2)
        bot = jnp.concatenate([z, w[1::2]], axis=2)
        return jnp.concatenate([top, bot], axis=1)
    return jnp.concatenate([pairs(w_rg), pairs(w_ig)], axis=2)


def _lru_tile(xb, gb, n, lru_refs, ctail_ref, hcar_ref):
    cw_ref, cb_ref, wgate_ref, brg_ref, big_ref, lam_ref = lru_refs
    lanes = _lru_lanes(n)
    sl = SUBLANES_V7X
    tm = xb.shape[0]
    nj = tm // sl
    sub = lax.broadcasted_iota(jnp.int32, (sl, LRU_CHUNK), 0)
    vrow = lambda t, j: t[j * sl:(j + 1) * sl]

    taps = CONV_W - 1
    wrap = [pltpu.roll(jnp.where(sub == sl - 1, vrow(ctail_ref[:, lanes], i), vrow(xb, nj - taps + i)), 1, 0)
            for i in range(taps)]
    ctail_ref[:, lanes] = xb[tm - taps * sl:]
    xc = cb_ref[:, lanes] + xb * cw_ref[taps:taps + 1, lanes]
    for k in range(1, CONV_W):
        shifted = jnp.concatenate(wrap[taps - k:] + [xb[:tm - k * sl]], axis=0)
        xc = xc + shifted * cw_ref[taps - k:taps - k + 1, lanes]

    a, b = _lru_coeffs(xc, n, wgate_ref, brg_ref, big_ref, lam_ref)
    gate = _gelu_tanh(gb)

    hl, acc = vrow(b, 0), vrow(a, 0)
    hls, accs = [hl], [acc]
    for j in range(1, nj):
        hl = vrow(a, j) * hl + vrow(b, j)
        acc = vrow(a, j) * acc
        hls.append(hl)
        accs.append(acc)

    for s in (1, 2, 4):
        keep = sub >= s
        acc_prev = jnp.where(keep, pltpu.roll(acc, s, 0), 1.0)
        hl_prev = jnp.where(keep, pltpu.roll(hl, s, 0), 0.0)
        hl = acc * hl_prev + hl
        acc = acc * acc_prev
    carry = hcar_ref[:, lanes]
    h_end = hl + acc * carry
    h_in = jnp.where(sub == 0, carry, pltpu.roll(h_end, 1, 0))
    hcar_ref[:, lanes] = h_end[sl - 1:sl]

    return [(hls[j] + accs[j] * h_in) * vrow(gate, j) for j in range(nj)]


def _rglru_sample_kernel(xb_ref, gb_ref, sc_ref, h0_ref, cw_ref, cb_ref, wgate_ref, brg_ref, big_ref,
                         lam_ref, hb_ref, conv_ref, h_ref):
    xb = xb_ref[...]
    taps = [sc_ref[:, j * D_RNN:(j + 1) * D_RNN] for j in range(CONV_W - 1)] + [xb]
    xc = cb_ref[...] + sum(t * cw_ref[j:j + 1, :] for j, t in enumerate(taps))
    for n in range(N_LRU_CHUNKS):
        lanes = _lru_lanes(n)
        a, b = _lru_coeffs(xc[:, lanes], n, wgate_ref, brg_ref, big_ref, lam_ref)
        h = a * h0_ref[:, lanes] + b
        h_ref[:, lanes] = h
        hb_ref[:, lanes] = (h * _gelu_tanh(gb_ref[:, lanes])).astype(hb_ref.dtype)
    for j in range(CONV_W - 1):
        conv_ref[:, j * D_RNN:(j + 1) * D_RNN] = taps[j + 1]


def _rglru_sample(xb, gb, state_conv, h0, cw, cb, wgate, brg, big, lam):
    nb = xb.shape[0]
    args = (xb, gb, state_conv, h0, cw, cb, wgate, brg, big, lam)
    return pl.pallas_call(
        _rglru_sample_kernel,
        grid=(1,),
        in_specs=[_const_spec(a.shape) for a in args],
        out_specs=(_const_spec((nb, D_RNN)), _const_spec((nb, (CONV_W - 1) * D_RNN)),
                   _const_spec((nb, D_RNN))),
        out_shape=(jax.ShapeDtypeStruct((nb, D_RNN), BF16),
                   jax.ShapeDtypeStruct((nb, (CONV_W - 1) * D_RNN), F32),
                   jax.ShapeDtypeStruct((nb, D_RNN), F32)),
        compiler_params=_params(("arbitrary",)),
        name="rglru_sample",
    )(*args)


def _post_kernel(*refs, dilations):
    n_parts = len(dilations)
    attn_refs = refs[:2 * n_parts] if n_parts > 1 else refs[:1]
    rest = refs[len(attn_refs):]
    (hb_ref, ga_ref, gb2_ref, x_ref, expand_ref, bg_ref, woa_ref, wor_ref, wout_ref, g2_ref,
     wfi_ref, wfo_ref, y_ref) = rest[:13]
    tm = x_ref.shape[0]

    if n_parts > 1:
        o_nat_ref, lse_nat_ref = rest[13:]

        def natural(src_ref, dst_ref, width, d):
            if d == 1:
                return src_ref[...].astype(F32)
            n_slabs = width // LANES_V7X
            for r in range(d):
                for c in range(n_slabs):
                    lanes = slice(r * width + c * LANES_V7X, r * width + (c + 1) * LANES_V7X)
                    dst_ref[c, pl.ds(r, tm // d, stride=d), :] = src_ref[:, lanes].astype(F32)
            return jnp.concatenate([dst_ref[c] for c in range(n_slabs)], axis=1)

        o_refs, stat_refs = attn_refs[:n_parts], attn_refs[n_parts:]
        stats = [natural(r, lse_nat_ref, LANES_V7X, d) for r, d in zip(stat_refs, dilations)]
        m_all = functools.reduce(jnp.maximum, stats)
        ws = [jnp.exp2(st - m_all) for st in stats]
        dens = [pltpu.roll(st, LANES_V7X - HEADS_PER_GROUP, 1) for st in stats]
        inv = 1.0 / sum(w * dn for w, dn in zip(ws, dens))
        is_head = lax.broadcasted_iota(jnp.int32, (1, LANES_V7X), 1) < HEADS_PER_GROUP
        attn = 0.0
        for o_ref, w, d in zip(o_refs, ws, dilations):
            cw = jnp.where(is_head, w * inv, 0.0)
            hi = cw.astype(BF16)
            lo = (cw - hi.astype(F32)).astype(BF16)
            wide = _mm(jnp.concatenate([hi, lo], axis=1), expand_ref[...])
            attn = attn + wide * natural(o_ref, o_nat_ref, GROUP_W, d)
    else:
        attn = attn_refs[0][...]

    ya = _mm(attn.astype(BF16), woa_ref[...])
    yb = _mm(hb_ref[...], wor_ref[...])
    merged = (jax.nn.sigmoid(ga_ref[...] + bg_ref[0:1, :]) * ya
              + jax.nn.sigmoid(gb2_ref[...] + bg_ref[1:2, :]) * yb)
    x1 = x_ref[...] + _mm(merged.astype(BF16), wout_ref[...])
    hn2 = _rms_norm_rows(x1, g2_ref[...]).astype(BF16)
    gu = _mm(hn2, wfi_ref[...])
    act = jax.nn.silu(gu[:, :D_FF]) * gu[:, D_FF:]
    y_ref[...] = x1 + _mm(act.astype(BF16), wfo_ref[...])


def _post(attn_parts, dilations, hb, ga, gb2, x, expand, bg, woa, wor, wout, g2, wfi, wfo, *, tm):
    m = x.shape[0]
    row = lambda w: pl.BlockSpec((tm, w), lambda i: (i, 0))
    n_parts = len(attn_parts)
    scratch = []
    if n_parts > 1:
        blocked = lambda w, d: pl.BlockSpec((tm // d, d * w), lambda i: (i, 0))
        attn_args = [o for o, _ in attn_parts] + [l for _, l in attn_parts]
        attn_specs = ([blocked(GROUP_W, d) for d in dilations]
                      + [blocked(LANES_V7X, d) for d in dilations])
        scratch = [pltpu.VMEM((GROUP_W // LANES_V7X, tm, LANES_V7X), F32),
                   pltpu.VMEM((1, tm, LANES_V7X), F32)]
    else:
        attn_args, attn_specs = list(attn_parts), [row(GROUP_W)]
    consts = (expand, bg, woa, wor, wout, g2, wfi, wfo)
    return pl.pallas_call(
        functools.partial(_post_kernel, dilations=dilations),
        grid=(m // tm,),
        in_specs=attn_specs + [row(D_RNN), row(D_MODEL), row(D_MODEL), row(D_MODEL)]
        + [_const_spec(c.shape) for c in consts],
        out_specs=row(D_MODEL),
        out_shape=jax.ShapeDtypeStruct((m, D_MODEL), F32),
        scratch_shapes=scratch,
        compiler_params=_params(("arbitrary",)),
        name="post_prompt" if n_parts > 1 else "post_sample",
    )(*attn_args, hb, ga, gb2, x, *consts)


def kernel(x_prompt, x_sample, cache_kv_w128, cache_kv_w512, cache_kv_w2048, state_conv, state_h,
           norm1_g, w_in, b_gate, q_norm_g, k_norm_g, conv_w, conv_b, w_rg, b_rg, w_ig, b_ig,
           lru_lambda, w_o_attn, w_o_rnn, w_out, norm2_g, w_ffn_in, w_ffn_out):
    assert x_prompt.shape[0] == 1 and norm1_g.shape[0] == 1 and x_sample.shape[1] == 1
    seq = x_prompt.shape[1]
    nb = x_sample.shape[0]
    layer = 0

    expand = (jnp.arange(LANES_V7X)[:, None] == (jnp.arange(GROUP_W) // HEAD_DIM)[None, :]).astype(BF16)
    expand = jnp.concatenate([expand, expand], axis=0)

    row2 = lambda t: t[layer].reshape(1, -1)
    g1, g2 = row2(norm1_g), row2(norm2_g)
    qg = jnp.tile(q_norm_g[layer], HEADS_PER_GROUP).reshape(1, GROUP_W) * (
        HEAD_NORM_GAIN * ATTN_SCALE * LOG2_E)
    kg = jnp.tile(k_norm_g[layer], HEADS_PER_GROUP).reshape(1, GROUP_W) * HEAD_NORM_GAIN
    wgate = _gate_weights(w_rg[layer].astype(BF16), w_ig[layer].astype(BF16))
    lru = (conv_w[layer], row2(conv_b), wgate, row2(b_rg), row2(b_ig), row2(lru_lambda))
    post_f32 = tuple(w[layer] for w in (w_o_attn, w_o_rnn, w_out, w_ffn_in, w_ffn_out))

    xs = x_sample[:, 0]
    proj, w_in_b = _sample_inproj(xs, g1, w_in[layer], qg, kg)
    qkvs = proj[:, :OFF_XB].reshape(nb, 3 * N_GROUPS, GROUP_W).transpose(1, 0, 2)
    qs, ks, vs = qkvs[:N_GROUPS], qkvs[N_GROUPS:2 * N_GROUPS], qkvs[2 * N_GROUPS:]
    xbs, gbs, gas, gb2s = (proj[:, a:b] for a, b in
                           ((OFF_XB, OFF_GB), (OFF_GB, OFF_GA), (OFF_GA, OFF_GB2), (OFF_GB2, proj.shape[1])))
    caches = [jnp.transpose(c[layer], (0, 2, 3, 4, 1)).reshape(nb, 2 * GROUP_W, c.shape[2])
              for c in (cache_kv_w128, cache_kv_w512, cache_kv_w2048)]

    xp = x_prompt[0]
    *qkv, hb, conv_rows, h_last, ga, gb2, kvp0, kvp1, kvp2, woa, wor, wout, wfi, wfo = _inproj(
        xp, g1, w_in_b, qg, kg, lru, post_f32, tm=PROMPT_ROW_TILE,
        tails=tuple(min(w, seq) for w, _ in GROUPS), dilations=DILATIONS)
    post_w = (expand, b_gate[layer], woa, wor, wout, g2, wfi, wfo)
    half = nb // 2
    parts, attn_s = [], []
    for g, d in enumerate(DILATIONS):
        sample = (qs, ks, vs, caches, (g - 1) * half) if g else None
        *part, = _attn_prompt(*qkv[3 * g:3 * g + 3], g, d, sample)
        parts.append(part[:2])
        attn_s += part[2:]
    attn_s = jnp.concatenate(attn_s, axis=0)
    assert attn_s.shape == (nb, GROUP_W)
    y_p = _post(parts, DILATIONS, hb, ga, gb2, xp, *post_w, tm=PROMPT_ROW_TILE)
    conv_p = conv_rows[SUBLANES_V7X - 1::SUBLANES_V7X]

    hbs, conv_s, h_s = _rglru_sample(xbs, gbs, state_conv[layer].reshape(nb, -1), state_h[layer], *lru)
    y_s = _post([attn_s], (1,), hbs, gas, gb2s, xs, *post_w, tm=nb)

    kv_prompt = [t.reshape(1, 1, t.shape[0], 2, HEADS_PER_GROUP, HEAD_DIM) for t in (kvp0, kvp1, kvp2)]
    kv_sample = [jnp.stack([ks[g], vs[g]], axis=1).reshape(1, nb, 1, 2, HEADS_PER_GROUP, HEAD_DIM)
                 for g in range(N_GROUPS)]
    return (y_p[None], y_s[:, None],
            kv_prompt[0], kv_prompt[1], kv_prompt[2],
            conv_p[None, None], h_last[None],
            kv_sample[0], kv_sample[1], kv_sample[2],
            conv_s.reshape(1, nb, CONV_W - 1, D_RNN), h_s[None])
```

```python
import functools

import jax
import jax.numpy as jnp
from jax import lax
from jax.experimental import pallas as pl
from jax.experimental.pallas import tpu as pltpu

F32 = jnp.float32
BF16 = jnp.bfloat16

D_MODEL = 1024
HEAD_DIM = 64
HEADS_PER_GROUP = 8
GROUPS = ((128, 1), (512, 4), (2048, 16))
DILATIONS = tuple(d for _, d in GROUPS)
N_GROUPS = len(GROUPS)
GROUP_W = HEADS_PER_GROUP * HEAD_DIM
QKV_WIDTH = N_GROUPS * GROUP_W
BLOCK = 128
ATTN_SCALE = HEAD_DIM ** -0.5
LOG2_E = 1.4426950408889634
NEG_INF = -1e30
D_RNN = 1280
RNN_BLOCKS = 10
RNN_BLOCK_W = D_RNN // RNN_BLOCKS
CONV_W = 4
LRU_C = 8.0
D_FF = 2816
RMS_EPS = 1e-6

LANES_V7X = 128
SUBLANES_V7X = 8
MXU_DIM_V7X = 256
VMEM_LIMIT_BYTES = 56 * 1024 * 1024
PROMPT_ROW_TILE = MXU_DIM_V7X

OFF_Q, OFF_K, OFF_V = 0, QKV_WIDTH, 2 * QKV_WIDTH
OFF_XB = 3 * QKV_WIDTH
OFF_GB = OFF_XB + D_RNN
OFF_GA = OFF_GB + D_RNN
OFF_GB2 = OFF_GA + D_MODEL


def _mm(a, b):
    return jnp.dot(a, b, preferred_element_type=F32)


def _mm_nt(a, b):
    return lax.dot_general(a, b, (((1,), (1,)), ((), ())), preferred_element_type=F32)


def _rms_norm_rows(x, g):
    return x * lax.rsqrt(jnp.mean(x * x, axis=-1, keepdims=True) + RMS_EPS) * g


HEAD_NORM_GAIN = HEAD_DIM ** 0.5


def _head_rms_norm(t):
    first_head = lax.broadcasted_iota(jnp.int32, (1, LANES_V7X), 1) < HEAD_DIM
    cols = []
    for c in range(0, t.shape[1], LANES_V7X):
        x = t[:, c:c + LANES_V7X]
        xx = x * x
        s0 = jnp.sum(jnp.where(first_head, xx, 0.0), axis=-1, keepdims=True)
        s1 = jnp.sum(jnp.where(first_head, 0.0, xx), axis=-1, keepdims=True)
        cols.append(x * lax.rsqrt(jnp.where(first_head, s0, s1) + HEAD_DIM * RMS_EPS))
    return jnp.concatenate(cols, axis=1)


def _const_spec(shape):
    nd = len(shape)
    return pl.BlockSpec(shape, lambda *_: (0,) * nd, pipeline_mode=pl.Buffered(1))


def _params(sem):
    return pltpu.CompilerParams(dimension_semantics=sem, vmem_limit_bytes=VMEM_LIMIT_BYTES)


def _inproj_kernel(x0_ref, xn_ref, g1_ref, w_ref, qg_ref, kg_ref, *refs, dilations, tail_first_steps,
                   n_cast):
    lru_refs, refs = refs[:6], refs[6:]
    cast_in, refs = refs[:n_cast], refs[n_cast:]
    qkv_refs, refs = refs[:3 * N_GROUPS], refs[3 * N_GROUPS:]
    (hb_ref, conv_ref, hlast_ref, ga_ref, gb2_ref), refs = refs[:5], refs[5:]
    kvt_refs, refs = refs[:N_GROUPS], refs[N_GROUPS:]
    cast_out, refs = refs[:n_cast], refs[n_cast:]
    hn_ref, hbs_ref, hnp_ref, ctail_ref, hcar_ref = refs
    for src_ref, dst_ref in zip(cast_in, cast_out):
        dst_ref[...] = src_ref[...].astype(dst_ref.dtype)
    step = pl.program_id(0)
    tm = xn_ref.shape[0]
    n_slabs = D_MODEL // LANES_V7X
    nj = tm // SUBLANES_V7X
    pitch = nj + SUBLANES_V7X

    def stage_normed_rows(x_ref):
        hn32 = _rms_norm_rows(x_ref[...], g1_ref[...])
        for c in range(n_slabs):
            lanes = slice(c * LANES_V7X, (c + 1) * LANES_V7X)
            hn_ref[c] = hn32[:, lanes]
            for s in range(SUBLANES_V7X):
                hnp_ref[c, s * pitch:s * pitch + nj, :] = hn32[s * nj:(s + 1) * nj, lanes]

    @pl.when(step == 0)
    def _():
        ctail_ref[...] = jnp.zeros(ctail_ref.shape, F32)
        hcar_ref[...] = jnp.zeros(hcar_ref.shape, F32)
        stage_normed_rows(x0_ref)

    hn = jnp.concatenate([hn_ref[c] for c in range(n_slabs)], axis=1).astype(BF16)

    def strided_rows(start, size, stride):
        return jnp.concatenate([hn_ref[c, pl.ds(start, size, stride=stride), :] for c in range(n_slabs)],
                               axis=1)

    ht = jnp.concatenate(
        [jnp.concatenate([hnp_ref[c, pl.ds(j, SUBLANES_V7X, stride=pitch), :] for c in range(n_slabs)],
                         axis=1) for j in range(nj)], axis=0).astype(BF16)
    per_vreg = LRU_CHUNK // LANES_V7X
    xg = _mm(ht, w_ref[:, OFF_XB:OFF_GA])
    for n in range(N_LRU_CHUNKS):
        c0 = n * LRU_CHUNK
        hb_rows = _lru_tile(xg[:, c0:c0 + LRU_CHUNK], xg[:, D_RNN + c0:D_RNN + c0 + LRU_CHUNK],
                            n, lru_refs, ctail_ref, hcar_ref)
        for j, rows in enumerate(hb_rows):
            for c in range(per_vreg):
                hbs_ref[n * per_vreg + c, pl.ds(j, SUBLANES_V7X, stride=pitch), :] = (
                    rows[:, c * LANES_V7X:(c + 1) * LANES_V7X])
    conv_ref[...] = ctail_ref[...]
    hlast_ref[...] = hcar_ref[...]
    hb_ref[...] = jnp.concatenate(
        [jnp.concatenate([hbs_ref[c, s * pitch:s * pitch + nj, :] for s in range(SUBLANES_V7X)], axis=0)
         for c in range(D_RNN // LANES_V7X)], axis=1).astype(hb_ref.dtype)

    def head_norm(t, gain):
        return _head_rms_norm(t) * gain

    def qkv(h, g):
        c = g * GROUP_W
        qn = head_norm(_mm(h, w_ref[:, OFF_Q + c:OFF_Q + c + GROUP_W]), qg_ref[...])
        kn = head_norm(_mm(h, w_ref[:, OFF_K + c:OFF_K + c + GROUP_W]), kg_ref[...])
        vv = _mm(h, w_ref[:, OFF_V + c:OFF_V + c + GROUP_W])
        return qn, kn, vv

    in_order = {}
    for g, d in enumerate(dilations):
        rows = tm // d
        if d == 1:
            hg = hn
        else:
            hg = jnp.concatenate([strided_rows(r, rows, d) for r in range(d)], axis=0).astype(BF16)
        parts = qkv(hg, g)
        if d == 1:
            in_order[g] = parts
        for t, o_ref in zip(parts, qkv_refs[3 * g:3 * g + 3]):
            for r in range(d):
                o_ref[:, r * GROUP_W:(r + 1) * GROUP_W] = t[r * rows:(r + 1) * rows].astype(o_ref.dtype)

    gates = _mm(hn, w_ref[:, OFF_GA:OFF_GB2 + D_MODEL])
    ga_ref[...] = gates[:, :D_MODEL]
    gb2_ref[...] = gates[:, D_MODEL:]

    stage_normed_rows(xn_ref)

    for g, (kvt_ref, first_step) in enumerate(zip(kvt_refs, tail_first_steps)):
        @pl.when(step >= first_step)
        def _():
            _, kn, vv = in_order[g] if g in in_order else qkv(hn, g)
            keep = kvt_ref.shape[0]
            kvt_ref[:, 0:GROUP_W] = kn[tm - keep:]
            kvt_ref[:, GROUP_W:2 * GROUP_W] = vv[tm - keep:]


def _inproj(x, g1, w_in, qg, kg, lru, to_bf16, *, tm, tails, dilations):
    m = x.shape[0]
    nt = m // tm
    row = lambda w: pl.BlockSpec((tm, w), lambda i: (i, 0))
    fixed = lambda r, w: pl.BlockSpec((r, w), lambda i: (0, 0))
    qkv_specs, qkv_shapes = [], []
    for d in dilations:
        qkv_specs += [pl.BlockSpec((tm // d, d * GROUP_W), lambda i: (i, 0))] * 3
        qkv_shapes += [jax.ShapeDtypeStruct((m // d, d * GROUP_W), BF16)] * 3
    kvt_specs, kvt_shapes, tail_first_steps = [], [], []
    for rows in tails:
        blk = min(rows, tm)
        first = nt - rows // blk
        kvt_specs.append(pl.BlockSpec((blk, 2 * GROUP_W), lambda i, first=first: (jnp.maximum(i - first, 0), 0)))
        kvt_shapes.append(jax.ShapeDtypeStruct((rows, 2 * GROUP_W), F32))
        tail_first_steps.append(first)
    tail_rows = (CONV_W - 1) * SUBLANES_V7X
    rnn_specs = (row(D_RNN), fixed(tail_rows, D_RNN), fixed(1, D_RNN))
    rnn_shapes = (jax.ShapeDtypeStruct((m, D_RNN), BF16),
                  jax.ShapeDtypeStruct((tail_rows, D_RNN), F32),
                  jax.ShapeDtypeStruct((1, D_RNN), F32))
    padded = tm + SUBLANES_V7X * SUBLANES_V7X
    scratch = [pltpu.VMEM((D_MODEL // LANES_V7X, tm, LANES_V7X), F32),
               pltpu.VMEM((D_RNN // LANES_V7X, padded, LANES_V7X), F32),
               pltpu.VMEM((D_MODEL // LANES_V7X, padded, LANES_V7X), F32),
               pltpu.VMEM((tail_rows, D_RNN), F32),
               pltpu.VMEM((1, D_RNN), F32)]
    out_shape = tuple(qkv_shapes) + rnn_shapes + (
        jax.ShapeDtypeStruct((m, D_MODEL), F32),
        jax.ShapeDtypeStruct((m, D_MODEL), F32),
    ) + tuple(kvt_shapes)
    consts = (g1, w_in, qg, kg) + tuple(lru)
    cast_specs, cast_shapes = [], []
    bf16_rows = 2 * SUBLANES_V7X
    for w in to_bf16:
        nblk = nt
        while w.shape[0] % (nblk * bf16_rows):
            nblk //= 2
        cast_specs.append(pl.BlockSpec((w.shape[0] // nblk, w.shape[1]),
                                       lambda i, per=nt // nblk: (i // per, 0)))
        cast_shapes.append(jax.ShapeDtypeStruct(w.shape, BF16))
    return pl.pallas_call(
        functools.partial(_inproj_kernel, dilations=dilations, tail_first_steps=tuple(tail_first_steps),
                          n_cast=len(to_bf16)),
        grid=(nt,),
        in_specs=[pl.BlockSpec((tm, D_MODEL), lambda i: (0, 0)),
                  pl.BlockSpec((tm, D_MODEL), lambda i: (jnp.minimum(i + 1, nt - 1), 0))]
        + [_const_spec(c.shape) for c in consts] + cast_specs,
        out_specs=(tuple(qkv_specs) + rnn_specs + (row(D_MODEL), row(D_MODEL)) + tuple(kvt_specs)
                   + tuple(cast_specs)),
        out_shape=out_shape + tuple(cast_shapes),
        scratch_shapes=scratch,
        compiler_params=_params(("arbitrary",)),
        name="inproj_lru",
    )(x, x, *consts, *to_bf16)


def _sample_inproj_kernel(x_ref, g1_ref, qg_ref, kg_ref, w_ref, proj_ref, wb_ref):
    wb = w_ref[...].astype(BF16)
    wb_ref[...] = wb
    t = _mm(_rms_norm_rows(x_ref[...], g1_ref[...]).astype(BF16), wb)
    for part in range(w_ref.shape[1] // GROUP_W):
        block = pl.program_id(0) * (w_ref.shape[1] // GROUP_W) + part
        cols = slice(part * GROUP_W, (part + 1) * GROUP_W)
        normed = _head_rms_norm(t[:, cols])
        proj_ref[:, cols] = jnp.where(block < N_GROUPS, normed * qg_ref[...],
                                      jnp.where(block < 2 * N_GROUPS, normed * kg_ref[...], t[:, cols]))


def _sample_inproj(x, g1, w_in, qg, kg):
    nb = x.shape[0]
    width = 2 * GROUP_W
    consts = (x, g1, qg, kg)
    col = lambda rows: pl.BlockSpec((rows, width), lambda j: (0, j))
    return pl.pallas_call(
        _sample_inproj_kernel,
        grid=(w_in.shape[1] // width,),
        in_specs=[_const_spec(c.shape) for c in consts] + [col(D_MODEL)],
        out_specs=(col(nb), col(D_MODEL)),
        out_shape=(jax.ShapeDtypeStruct((nb, w_in.shape[1]), F32),
                   jax.ShapeDtypeStruct(w_in.shape, BF16)),
        compiler_params=_params(("arbitrary",)),
        name="inproj_sample",
    )(*consts, w_in)


ATTN_BLOCKS_PER_STEP = 8


def _attn_prompt_kernel(q_ref, kc_ref, kp_ref, vc_ref, vp_ref, *refs, sample_first_row):
    if sample_first_row is None:
        o_ref, lse_ref = refs
    else:
        qs_ref, ks_ref, vs_ref, c0_ref, c1_ref, c2_ref, o_ref, lse_ref, os_ref = refs
        flat_step = pl.program_id(0) * pl.num_programs(1) + pl.program_id(1)
        _attn_sample_row(flat_step, sample_first_row + flat_step, qs_ref, ks_ref, vs_ref,
                         (c0_ref, c1_ref, c2_ref), os_ref)
    step = pl.program_id(1)
    qi = lax.broadcasted_iota(jnp.int32, (BLOCK, 2 * BLOCK), 0)
    kj = lax.broadcasted_iota(jnp.int32, (BLOCK, 2 * BLOCK), 1)
    dist = BLOCK + qi - kj
    band = (dist >= 0) & (dist <= BLOCK)
    first_band = band & ((step > 0) | (kj >= BLOCK))
    band2 = jnp.concatenate([band, band], axis=0)
    first_band2 = jnp.concatenate([first_band, first_band], axis=0)
    first_head = lax.broadcasted_iota(jnp.int32, (1, LANES_V7X), 1) < HEAD_DIM
    zero = jnp.zeros((), BF16)

    for blk in range(q_ref.shape[0] // BLOCK):
        rows = slice(blk * BLOCK, (blk + 1) * BLOCK)
        prev_rows = slice((blk - 1) * BLOCK, blk * BLOCK)
        valid = band2 if blk else first_band2
        lse_ref[rows, :] = jnp.zeros((BLOCK, LANES_V7X), F32)
        for pair in range(HEADS_PER_GROUP // 2):
            cols = slice(pair * LANES_V7X, (pair + 1) * LANES_V7X)
            qp = q_ref[rows, cols]
            k_prev = kc_ref[prev_rows, cols] if blk else kp_ref[:, cols]
            v_prev = vc_ref[prev_rows, cols] if blk else vp_ref[:, cols]
            kk = jnp.concatenate([k_prev, kc_ref[rows, cols]], axis=0)
            vv = jnp.concatenate([v_prev, vc_ref[rows, cols]], axis=0)
            q2 = jnp.concatenate([jnp.where(first_head, qp, zero), jnp.where(first_head, zero, qp)], axis=0)
            s = jnp.where(valid, _mm_nt(q2, kk), NEG_INF)
            mx = jnp.max(s, axis=-1, keepdims=True)
            p = jnp.exp2(s - mx)
            den = jnp.sum(p, axis=-1, keepdims=True)
            pv = _mm(p.astype(BF16), vv)
            o_ref[rows, cols] = jnp.where(first_head, pv[:BLOCK], pv[BLOCK:]).astype(o_ref.dtype)
            for e, (lo, hi) in enumerate(((0, BLOCK), (BLOCK, 2 * BLOCK))):
                head = 2 * pair + e
                lse_ref[rows, head:head + 1] = mx[lo:hi]
                lse_ref[rows, HEADS_PER_GROUP + head:HEADS_PER_GROUP + head + 1] = den[lo:hi]


def _attn_prompt(q, k, v, g, dilation, sample=None):
    m_len = q.shape[0]
    nbs = ATTN_BLOCKS_PER_STEP
    rows = nbs * BLOCK
    n_inner = m_len // rows
    cur = pl.BlockSpec((rows, GROUP_W), lambda r, b: (b, r))
    prev = pl.BlockSpec((BLOCK, GROUP_W), lambda r, b: (jnp.maximum(b * nbs - 1, 0), r))
    in_specs, args = [cur, cur, prev, cur, prev], [q, k, k, v, v]
    out_specs = [pl.BlockSpec((rows, GROUP_W), lambda r, b: (b, r)),
                 pl.BlockSpec((rows, LANES_V7X), lambda r, b: (b, r))]
    out_shape = [jax.ShapeDtypeStruct((m_len, dilation * GROUP_W), BF16),
                 jax.ShapeDtypeStruct((m_len, dilation * LANES_V7X), F32)]
    first_row = None
    if sample is not None:
        qs, ks, vs, caches, first_row = sample
        steps = dilation * n_inner
        full = pl.BlockSpec(qs.shape, lambda r, b: (0, 0, 0))
        in_specs += [full, full, full] + [
            pl.BlockSpec((None,) + c.shape[1:], lambda r, b: (first_row + r * n_inner + b, 0, 0))
            for c in caches]
        args += [qs, ks, vs, *caches]
        out_specs.append(pl.BlockSpec((steps, GROUP_W), lambda r, b: (0, 0)))
        out_shape.append(jax.ShapeDtypeStruct((steps, GROUP_W), F32))
    return pl.pallas_call(
        functools.partial(_attn_prompt_kernel, sample_first_row=first_row),
        grid=(dilation, n_inner),
        in_specs=in_specs,
        out_specs=tuple(out_specs),
        out_shape=tuple(out_shape),
        compiler_params=_params(("arbitrary", "arbitrary")),
        name=f"attn_prompt_g{g}",
    )(*args)


def _attn_sample_row(out_row, b, q_ref, k_ref, v_ref, cache_refs, o_ref):
    head_row = lax.broadcasted_iota(jnp.int32, (HEADS_PER_GROUP, GROUP_W), 0)
    head_lane = lax.broadcasted_iota(jnp.int32, (HEADS_PER_GROUP, GROUP_W), 1) // HEAD_DIM
    own = head_row == head_lane
    parts = []
    for g, c_ref in enumerate(cache_refs):
        window, dilation = GROUPS[g]
        bf = lambda t: t.astype(BF16).astype(F32)
        qmat = jnp.where(own, jnp.broadcast_to(q_ref[g, pl.ds(b, 1), :], (HEADS_PER_GROUP, GROUP_W)), 0.0)
        qmat = qmat.astype(BF16)
        knew = bf(k_ref[g, pl.ds(b, 1), :])
        vnew = bf(v_ref[g, pl.ds(b, 1), :])
        pos = lax.broadcasted_iota(jnp.int32, (HEADS_PER_GROUP, window), 1)
        s = _mm(qmat, c_ref[0:GROUP_W, :].astype(BF16))
        s = jnp.where((pos & (dilation - 1)) == 0, s, NEG_INF)
        s_new = jnp.sum(qmat.astype(F32) * knew, axis=-1, keepdims=True)
        mx = jnp.maximum(jnp.max(s, axis=-1, keepdims=True), s_new)
        p = jnp.exp2(s - mx)
        p_new = jnp.exp2(s_new - mx)
        den = jnp.sum(p, axis=-1, keepdims=True) + p_new
        numer = _mm_nt(p.astype(BF16), c_ref[GROUP_W:2 * GROUP_W, :].astype(BF16)) + bf(p_new) * vnew
        parts.append((jnp.where(own, numer, 0.0), mx, den))
    m_all = functools.reduce(jnp.maximum, [mx for _, mx, _ in parts])
    ws = [jnp.exp2(mx - m_all) for _, mx, _ in parts]
    num = sum(n * w for (n, _, _), w in zip(parts, ws))
    den = sum(d * w for (_, _, d), w in zip(parts, ws))
    o_ref[pl.ds(out_row, 1), :] = jnp.sum(num / den, axis=0, keepdims=True)


def _gelu_tanh(x):
    cdf = 0.5 * (1.0 + jnp.tanh(0.7978845608028654 * (x + 0.044715 * (x * x * x))))
    return x * cdf


def _softplus(x):
    return jnp.maximum(x, 0.0) + jnp.log1p(jnp.exp(-jnp.abs(x)))


LRU_CHUNK = 2 * RNN_BLOCK_W
N_LRU_CHUNKS = D_RNN // LRU_CHUNK


def _lru_lanes(n):
    return slice(n * LRU_CHUNK, (n + 1) * LRU_CHUNK)


def _lru_coeffs(xc, n, wgate_ref, brg_ref, big_ref, lam_ref):
    lanes = _lru_lanes(n)
    sigmoid = lambda v: 0.5 + 0.5 * jnp.tanh(0.5 * v)
    logits = _mm(xc.astype(BF16), wgate_ref[n])
    r = sigmoid(logits[:, :LRU_CHUNK] + brg_ref[:, lanes])
    i = sigmoid(logits[:, LRU_CHUNK:] + big_ref[:, lanes])
    neg_rate = LRU_C * _softplus(-lam_ref[:, lanes])
    a = jnp.exp2(r * (-LOG2_E * neg_rate))
    t = jnp.tanh(r * neg_rate)
    y = 2.0 * t / (1.0 + t)
    root = jnp.where(y > 0.0, y * lax.rsqrt(y), 0.0)
    return a, root * i * xc


def _gate_weights(w_rg, w_ig):
    def pairs(w):
        z = jnp.zeros_like(w[0::2])
        top = jnp.concatenate([w[0::2], z], axis=2)
        bot = jnp.concatenate([z, w[1::2]], axis=2)
        return jnp.concatenate([top, bot], axis=1)
    return jnp.concatenate([pairs(w_rg), pairs(w_ig)], axis=2)


def _lru_tile(xb, gb, n, lru_refs, ctail_ref, hcar_ref):
    cw_ref, cb_ref, wgate_ref, brg_ref, big_ref, lam_ref = lru_refs
    lanes = _lru_lanes(n)
    sl = SUBLANES_V7X
    tm = xb.shape[0]
    nj = tm // sl
    sub = lax.broadcasted_iota(jnp.int32, (sl, LRU_CHUNK), 0)
    vrow = lambda t, j: t[j * sl:(j + 1) * sl]

    taps = CONV_W - 1
    wrap = [pltpu.roll(jnp.where(sub == sl - 1, vrow(ctail_ref[:, lanes], i), vrow(xb, nj - taps + i)), 1, 0)
            for i in range(taps)]
    ctail_ref[:, lanes] = xb[tm - taps * sl:]
    xc = cb_ref[:, lanes] + xb * cw_ref[taps:taps + 1, lanes]
    for k in range(1, CONV_W):
        shifted = jnp.concatenate(wrap[taps - k:] + [xb[:tm - k * sl]], axis=0)
        xc = xc + shifted * cw_ref[taps - k:taps - k + 1, lanes]

    a, b = _lru_coeffs(xc, n, wgate_ref, brg_ref, big_ref, lam_ref)
    gate = _gelu_tanh(gb)

    hl, acc = vrow(b, 0), vrow(a, 0)
    hls, accs = [hl], [acc]
    for j in range(1, nj):
        hl = vrow(a, j) * hl + vrow(b, j)
        acc = vrow(a, j) * acc
        hls.append(hl)
        accs.append(acc)

    for s in (1, 2, 4):
        keep = sub >= s
        acc_prev = jnp.where(keep, pltpu.roll(acc, s, 0), 1.0)
        hl_prev = jnp.where(keep, pltpu.roll(hl, s, 0), 0.0)
        hl = acc * hl_prev + hl
        acc = acc * acc_prev
    carry = hcar_ref[:, lanes]
    h_end = hl + acc * carry
    h_in = jnp.where(sub == 0, carry, pltpu.roll(h_end, 1, 0))
    hcar_ref[:, lanes] = h_end[sl - 1:sl]

    return [(hls[j] + accs[j] * h_in) * vrow(gate, j) for j in range(nj)]


def _rglru_sample_kernel(xb_ref, gb_ref, sc_ref, h0_ref, cw_ref, cb_ref, wgate_ref, brg_ref, big_ref,
                         lam_ref, hb_ref, conv_ref, h_ref):
    xb = xb_ref[...]
    taps = [sc_ref[:, j * D_RNN:(j + 1) * D_RNN] for j in range(CONV_W - 1)] + [xb]
    xc = cb_ref[...] + sum(t * cw_ref[j:j + 1, :] for j, t in enumerate(taps))
    for n in range(N_LRU_CHUNKS):
        lanes = _lru_lanes(n)
        a, b = _lru_coeffs(xc[:, lanes], n, wgate_ref, brg_ref, big_ref, lam_ref)
        h = a * h0_ref[:, lanes] + b
        h_ref[:, lanes] = h
        hb_ref[:, lanes] = (h * _gelu_tanh(gb_ref[:, lanes])).astype(hb_ref.dtype)
    for j in range(CONV_W - 1):
        conv_ref[:, j * D_RNN:(j + 1) * D_RNN] = taps[j + 1]


def _rglru_sample(xb, gb, state_conv, h0, cw, cb, wgate, brg, big, lam):
    nb = xb.shape[0]
    args = (xb, gb, state_conv, h0, cw, cb, wgate, brg, big, lam)
    return pl.pallas_call(
        _rglru_sample_kernel,
        grid=(1,),
        in_specs=[_const_spec(a.shape) for a in args],
        out_specs=(_const_spec((nb, D_RNN)), _const_spec((nb, (CONV_W - 1) * D_RNN)),
                   _const_spec((nb, D_RNN))),
        out_shape=(jax.ShapeDtypeStruct((nb, D_RNN), BF16),
                   jax.ShapeDtypeStruct((nb, (CONV_W - 1) * D_RNN), F32),
                   jax.ShapeDtypeStruct((nb, D_RNN), F32)),
        compiler_params=_params(("arbitrary",)),
        name="rglru_sample",
    )(*args)


def _post_kernel(*refs, dilations):
    n_parts = len(dilations)
    attn_refs = refs[:4 * n_parts] if n_parts > 1 else refs[:1]
    rest = refs[len(attn_refs):]
    (hb_ref, ga_ref, gb2_ref, x_ref, expand_ref, bg_ref, woa_ref, wor_ref, wout_ref, g2_ref,
     wfi_ref, wfo_ref, y_ref) = rest[:13]
    tm = x_ref.shape[0]

    if n_parts > 1:
        o_nat_ref, lse_nat_ref, attn_ref = rest[13:]

        def natural(src_ref, dst_ref, width, d):
            if d == 1:
                return src_ref[...].astype(F32)
            n_slabs = width // LANES_V7X
            for r in range(d):
                for c in range(n_slabs):
                    lanes = slice(r * width + c * LANES_V7X, r * width + (c + 1) * LANES_V7X)
                    dst_ref[c, pl.ds(r, tm // d, stride=d), :] = src_ref[:, lanes].astype(F32)
            return jnp.concatenate([dst_ref[c] for c in range(n_slabs)], axis=1)

        def stage_mixture(o_refs, stat_refs):
            stats = [natural(r, lse_nat_ref, LANES_V7X, d) for r, d in zip(stat_refs, dilations)]
            m_all = functools.reduce(jnp.maximum, stats)
            ws = [jnp.exp2(st - m_all) for st in stats]
            dens = [pltpu.roll(st, LANES_V7X - HEADS_PER_GROUP, 1) for st in stats]
            inv = 1.0 / sum(w * dn for w, dn in zip(ws, dens))
            is_head = lax.broadcasted_iota(jnp.int32, (1, LANES_V7X), 1) < HEADS_PER_GROUP
            attn = 0.0
            for o_ref, w, d in zip(o_refs, ws, dilations):
                cw = jnp.where(is_head, w * inv, 0.0)
                hi = cw.astype(BF16)
                lo = (cw - hi.astype(F32)).astype(BF16)
                wide = _mm(jnp.concatenate([hi, lo], axis=1), expand_ref[...])
                attn = attn + wide * natural(o_ref, o_nat_ref, GROUP_W, d)
            attn_ref[...] = attn.astype(BF16)

        first, nxt = attn_refs[:2 * n_parts], attn_refs[2 * n_parts:]

        @pl.when(pl.program_id(0) == 0)
        def _():
            stage_mixture(first[:n_parts], first[n_parts:])

        attn = attn_ref[...]
    else:
        attn = attn_refs[0][...].astype(BF16)

    ya = _mm(attn, woa_ref[...])
    yb = _mm(hb_ref[...], wor_ref[...])
    merged = (jax.nn.sigmoid(ga_ref[...] + bg_ref[0:1, :]) * ya
              + jax.nn.sigmoid(gb2_ref[...] + bg_ref[1:2, :]) * yb)
    x1 = x_ref[...] + _mm(merged.astype(BF16), wout_ref[...])
    hn2 = _rms_norm_rows(x1, g2_ref[...]).astype(BF16)
    gu = _mm(hn2, wfi_ref[...])
    act = jax.nn.silu(gu[:, :D_FF]) * gu[:, D_FF:]
    y_ref[...] = x1 + _mm(act.astype(BF16), wfo_ref[...])
    if n_parts > 1:
        stage_mixture(nxt[:n_parts], nxt[n_parts:])


def _post(attn_parts, dilations, hb, ga, gb2, x, expand, bg, woa, wor, wout, g2, wfi, wfo, *, tm):
    m = x.shape[0]
    row = lambda w: pl.BlockSpec((tm, w), lambda i: (i, 0))
    n_parts = len(attn_parts)
    scratch = []
    if n_parts > 1:
        nt = m // tm
        first = lambda w, d: pl.BlockSpec((tm // d, d * w), lambda i: (0, 0))
        nxt = lambda w, d: pl.BlockSpec((tm // d, d * w), lambda i: (jnp.minimum(i + 1, nt - 1), 0))
        attn_args = 2 * ([o for o, _ in attn_parts] + [l for _, l in attn_parts])
        attn_specs = [spec(w, d) for spec in (first, nxt) for w in (GROUP_W, LANES_V7X) for d in dilations]
        scratch = [pltpu.VMEM((GROUP_W // LANES_V7X, tm, LANES_V7X), F32),
                   pltpu.VMEM((1, tm, LANES_V7X), F32),
                   pltpu.VMEM((tm, GROUP_W), BF16)]
    else:
        attn_args, attn_specs = list(attn_parts), [row(GROUP_W)]
    consts = (expand, bg, woa, wor, wout, g2, wfi, wfo)
    return pl.pallas_call(
        functools.partial(_post_kernel, dilations=dilations),
        grid=(m // tm,),
        in_specs=attn_specs + [row(D_RNN), row(D_MODEL), row(D_MODEL), row(D_MODEL)]
        + [_const_spec(c.shape) for c in consts],
        out_specs=row(D_MODEL),
        out_shape=jax.ShapeDtypeStruct((m, D_MODEL), F32),
        scratch_shapes=scratch,
        compiler_params=_params(("arbitrary",)),
        name="post_prompt" if n_parts > 1 else "post_sample",
    )(*attn_args, hb, ga, gb2, x, *consts)


def kernel(x_prompt, x_sample, cache_kv_w128, cache_kv_w512, cache_kv_w2048, state_conv, state_h,
           norm1_g, w_in, b_gate, q_norm_g, k_norm_g, conv_w, conv_b, w_rg, b_rg, w_ig, b_ig,
           lru_lambda, w_o_attn, w_o_rnn, w_out, norm2_g, w_ffn_in, w_ffn_out):
    assert x_prompt.shape[0] == 1 and norm1_g.shape[0] == 1 and x_sample.shape[1] == 1
    seq = x_prompt.shape[1]
    nb = x_sample.shape[0]
    layer = 0

    expand = (jnp.arange(LANES_V7X)[:, None] == (jnp.arange(GROUP_W) // HEAD_DIM)[None, :]).astype(BF16)
    expand = jnp.concatenate([expand, expand], axis=0)

    row2 = lambda t: t[layer].reshape(1, -1)
    g1, g2 = row2(norm1_g), row2(norm2_g)
    qg = jnp.tile(q_norm_g[layer], HEADS_PER_GROUP).reshape(1, GROUP_W) * (
        HEAD_NORM_GAIN * ATTN_SCALE * LOG2_E)
    kg = jnp.tile(k_norm_g[layer], HEADS_PER_GROUP).reshape(1, GROUP_W) * HEAD_NORM_GAIN
    wgate = _gate_weights(w_rg[layer].astype(BF16), w_ig[layer].astype(BF16))
    lru = (conv_w[layer], row2(conv_b), wgate, row2(b_rg), row2(b_ig), row2(lru_lambda))
    post_f32 = tuple(w[layer] for w in (w_o_attn, w_o_rnn, w_out, w_ffn_in, w_ffn_out))

    xs = x_sample[:, 0]
    proj, w_in_b = _sample_inproj(xs, g1, w_in[layer], qg, kg)
    qkvs = proj[:, :OFF_XB].reshape(nb, 3 * N_GROUPS, GROUP_W).transpose(1, 0, 2)
    qs, ks, vs = qkvs[:N_GROUPS], qkvs[N_GROUPS:2 * N_GROUPS], qkvs[2 * N_GROUPS:]
    xbs, gbs, gas, gb2s = (proj[:, a:b] for a, b in
                           ((OFF_XB, OFF_GB), (OFF_GB, OFF_GA), (OFF_GA, OFF_GB2), (OFF_GB2, proj.shape[1])))
    caches = [jnp.transpose(c[layer], (0, 2, 3, 4, 1)).reshape(nb, 2 * GROUP_W, c.shape[2])
              for c in (cache_kv_w128, cache_kv_w512, cache_kv_w2048)]

    xp = x_prompt[0]
    *qkv, hb, conv_rows, h_last, ga, gb2, kvp0, kvp1, kvp2, woa, wor, wout, wfi, wfo = _inproj(
        xp, g1, w_in_b, qg, kg, lru, post_f32, tm=PROMPT_ROW_TILE,
        tails=tuple(min(w, seq) for w, _ in GROUPS), dilations=DILATIONS)
    post_w = (expand, b_gate[layer], woa, wor, wout, g2, wfi, wfo)
    half = nb // 2
    parts, attn_s = [], []
    for g, d in enumerate(DILATIONS):
        sample = (qs, ks, vs, caches, (g - 1) * half) if g else None
        *part, = _attn_prompt(*qkv[3 * g:3 * g + 3], g, d, sample)
        parts.append(part[:2])
        attn_s += part[2:]
    attn_s = jnp.concatenate(attn_s, axis=0)
    assert attn_s.shape == (nb, GROUP_W)
    y_p = _post(parts, DILATIONS, hb, ga, gb2, xp, *post_w, tm=PROMPT_ROW_TILE)
    conv_p = conv_rows[SUBLANES_V7X - 1::SUBLANES_V7X]

    hbs, conv_s, h_s = _rglru_sample(xbs, gbs, state_conv[layer].reshape(nb, -1), state_h[layer], *lru)
    y_s = _post([attn_s], (1,), hbs, gas, gb2s, xs, *post_w, tm=nb)

    kv_prompt = [t.reshape(1, 1, t.shape[0], 2, HEADS_PER_GROUP, HEAD_DIM) for t in (kvp0, kvp1, kvp2)]
    kv_sample = [jnp.stack([ks[g], vs[g]], axis=1).reshape(1, nb, 1, 2, HEADS_PER_GROUP, HEAD_DIM)
                 for g in range(N_GROUPS)]
    return (y_p[None], y_s[:, None],
            kv_prompt[0], kv_prompt[1], kv_prompt[2],
            conv_p[None, None], h_last[None],
            kv_sample[0], kv_sample[1], kv_sample[2],
            conv_s.reshape(1, nb, CONV_W - 1, D_RNN), h_s[None])
```

```python
import functools

import jax
import jax.numpy as jnp
from jax import lax
from jax.experimental import pallas as pl
from jax.experimental.pallas import tpu as pltpu

F32 = jnp.float32
BF16 = jnp.bfloat16

D_MODEL = 1024
HEAD_DIM = 64
HEADS_PER_GROUP = 8
GROUPS = ((128, 1), (512, 4), (2048, 16))
DILATIONS = tuple(d for _, d in GROUPS)
N_GROUPS = len(GROUPS)
GROUP_W = HEADS_PER_GROUP * HEAD_DIM
QKV_WIDTH = N_GROUPS * GROUP_W
BLOCK = 128
ATTN_SCALE = HEAD_DIM ** -0.5
LOG2_E = 1.4426950408889634
NEG_INF = -1e30
D_RNN = 1280
RNN_BLOCKS = 10
RNN_BLOCK_W = D_RNN // RNN_BLOCKS
CONV_W = 4
LRU_C = 8.0
D_FF = 2816
RMS_EPS = 1e-6

LANES_V7X = 128
SUBLANES_V7X = 8
MXU_DIM_V7X = 256
VMEM_LIMIT_BYTES = 56 * 1024 * 1024
PROMPT_ROW_TILE = MXU_DIM_V7X

OFF_Q, OFF_K, OFF_V = 0, QKV_WIDTH, 2 * QKV_WIDTH
OFF_XB = 3 * QKV_WIDTH
OFF_GB = OFF_XB + D_RNN
OFF_GA = OFF_GB + D_RNN
OFF_GB2 = OFF_GA + D_MODEL


def _mm(a, b):
    return jnp.dot(a, b, preferred_element_type=F32)


def _mm_nt(a, b):
    return lax.dot_general(a, b, (((1,), (1,)), ((), ())), preferred_element_type=F32)


def _rms_norm_rows(x, g):
    return x * lax.rsqrt(jnp.mean(x * x, axis=-1, keepdims=True) + RMS_EPS) * g


HEAD_NORM_GAIN = HEAD_DIM ** 0.5


def _head_rms_norm(t):
    first_head = lax.broadcasted_iota(jnp.int32, (1, LANES_V7X), 1) < HEAD_DIM
    cols = []
    for c in range(0, t.shape[1], LANES_V7X):
        x = t[:, c:c + LANES_V7X]
        xx = x * x
        s0 = jnp.sum(jnp.where(first_head, xx, 0.0), axis=-1, keepdims=True)
        s1 = jnp.sum(jnp.where(first_head, 0.0, xx), axis=-1, keepdims=True)
        cols.append(x * lax.rsqrt(jnp.where(first_head, s0, s1) + HEAD_DIM * RMS_EPS))
    return jnp.concatenate(cols, axis=1)


def _const_spec(shape):
    nd = len(shape)
    return pl.BlockSpec(shape, lambda *_: (0,) * nd, pipeline_mode=pl.Buffered(1))


def _params(sem):
    return pltpu.CompilerParams(dimension_semantics=sem, vmem_limit_bytes=VMEM_LIMIT_BYTES)


def _inproj_kernel(x0_ref, xn_ref, g1_ref, w_ref, qg_ref, kg_ref, *refs, dilations, tail_first_steps,
                   n_cast):
    lru_refs, refs = refs[:6], refs[6:]
    cast_in, refs = refs[:n_cast], refs[n_cast:]
    qkv_refs, refs = refs[:3 * N_GROUPS], refs[3 * N_GROUPS:]
    (hb_ref, conv_ref, hlast_ref, ga_ref, gb2_ref), refs = refs[:5], refs[5:]
    kvt_refs, refs = refs[:N_GROUPS], refs[N_GROUPS:]
    cast_out, refs = refs[:n_cast], refs[n_cast:]
    hn_ref, hbs_ref, hnp_ref, ctail_ref, hcar_ref = refs
    for src_ref, dst_ref in zip(cast_in, cast_out):
        dst_ref[...] = src_ref[...].astype(dst_ref.dtype)
    step = pl.program_id(0)
    tm = xn_ref.shape[0]
    n_slabs = D_MODEL // LANES_V7X
    nj = tm // SUBLANES_V7X
    pitch = nj + SUBLANES_V7X

    def stage_normed_rows(x_ref):
        hn32 = _rms_norm_rows(x_ref[...], g1_ref[...])
        for c in range(n_slabs):
            lanes = slice(c * LANES_V7X, (c + 1) * LANES_V7X)
            hn_ref[c] = hn32[:, lanes]
            for s in range(SUBLANES_V7X):
                hnp_ref[c, s * pitch:s * pitch + nj, :] = hn32[s * nj:(s + 1) * nj, lanes]

    @pl.when(step == 0)
    def _():
        ctail_ref[...] = jnp.zeros(ctail_ref.shape, F32)
        hcar_ref[...] = jnp.zeros(hcar_ref.shape, F32)
        stage_normed_rows(x0_ref)

    hn = jnp.concatenate([hn_ref[c] for c in range(n_slabs)], axis=1).astype(BF16)

    def strided_rows(start, size, stride):
        return jnp.concatenate([hn_ref[c, pl.ds(start, size, stride=stride), :] for c in range(n_slabs)],
                               axis=1)

    ht = jnp.concatenate(
        [jnp.concatenate([hnp_ref[c, pl.ds(j, SUBLANES_V7X, stride=pitch), :] for c in range(n_slabs)],
                         axis=1) for j in range(nj)], axis=0).astype(BF16)
    per_vreg = LRU_CHUNK // LANES_V7X
    xg = _mm(ht, w_ref[:, OFF_XB:OFF_GA])
    for n in range(N_LRU_CHUNKS):
        c0 = n * LRU_CHUNK
        hb_rows = _lru_tile(xg[:, c0:c0 + LRU_CHUNK], xg[:, D_RNN + c0:D_RNN + c0 + LRU_CHUNK],
                            n, lru_refs, ctail_ref, hcar_ref)
        for j, rows in enumerate(hb_rows):
            for c in range(per_vreg):
                hbs_ref[n * per_vreg + c, pl.ds(j, SUBLANES_V7X, stride=pitch), :] = (
                    rows[:, c * LANES_V7X:(c + 1) * LANES_V7X])
    conv_ref[...] = ctail_ref[...]
    hlast_ref[...] = hcar_ref[...]
    hb_ref[...] = jnp.concatenate(
        [jnp.concatenate([hbs_ref[c, s * pitch:s * pitch + nj, :] for s in range(SUBLANES_V7X)], axis=0)
         for c in range(D_RNN // LANES_V7X)], axis=1).astype(hb_ref.dtype)

    def head_norm(t, gain):
        return _head_rms_norm(t) * gain

    def qkv(h, g):
        c = g * GROUP_W
        qn = head_norm(_mm(h, w_ref[:, OFF_Q + c:OFF_Q + c + GROUP_W]), qg_ref[...])
        kn = head_norm(_mm(h, w_ref[:, OFF_K + c:OFF_K + c + GROUP_W]), kg_ref[...])
        vv = _mm(h, w_ref[:, OFF_V + c:OFF_V + c + GROUP_W])
        return qn, kn, vv

    in_order = {}
    for g, d in enumerate(dilations):
        rows = tm // d
        if d == 1:
            hg = hn
        else:
            hg = jnp.concatenate([strided_rows(r, rows, d) for r in range(d)], axis=0).astype(BF16)
        parts = qkv(hg, g)
        if d == 1:
            in_order[g] = parts
        for t, o_ref in zip(parts, qkv_refs[3 * g:3 * g + 3]):
            for r in range(d):
                o_ref[:, r * GROUP_W:(r + 1) * GROUP_W] = t[r * rows:(r + 1) * rows].astype(o_ref.dtype)

    gates = _mm(hn, w_ref[:, OFF_GA:OFF_GB2 + D_MODEL])
    ga_ref[...] = gates[:, :D_MODEL]
    gb2_ref[...] = gates[:, D_MODEL:]

    stage_normed_rows(xn_ref)

    for g, (kvt_ref, first_step) in enumerate(zip(kvt_refs, tail_first_steps)):
        @pl.when(step >= first_step)
        def _():
            _, kn, vv = in_order[g] if g in in_order else qkv(hn, g)
            keep = kvt_ref.shape[0]
            kvt_ref[:, 0:GROUP_W] = kn[tm - keep:]
            kvt_ref[:, GROUP_W:2 * GROUP_W] = vv[tm - keep:]


def _inproj(x, g1, w_in, qg, kg, lru, to_bf16, *, tm, tails, dilations):
    m = x.shape[0]
    nt = m // tm
    row = lambda w: pl.BlockSpec((tm, w), lambda i: (i, 0))
    fixed = lambda r, w: pl.BlockSpec((r, w), lambda i: (0, 0))
    qkv_specs, qkv_shapes = [], []
    for d in dilations:
        qkv_specs += [pl.BlockSpec((tm // d, d * GROUP_W), lambda i: (i, 0))] * 3
        qkv_shapes += [jax.ShapeDtypeStruct((m // d, d * GROUP_W), BF16)] * 3
    kvt_specs, kvt_shapes, tail_first_steps = [], [], []
    for rows in tails:
        blk = min(rows, tm)
        first = nt - rows // blk
        kvt_specs.append(pl.BlockSpec((blk, 2 * GROUP_W), lambda i, first=first: (jnp.maximum(i - first, 0), 0)))
        kvt_shapes.append(jax.ShapeDtypeStruct((rows, 2 * GROUP_W), F32))
        tail_first_steps.append(first)
    tail_rows = (CONV_W - 1) * SUBLANES_V7X
    rnn_specs = (row(D_RNN), fixed(tail_rows, D_RNN), fixed(1, D_RNN))
    rnn_shapes = (jax.ShapeDtypeStruct((m, D_RNN), BF16),
                  jax.ShapeDtypeStruct((tail_rows, D_RNN), F32),
                  jax.ShapeDtypeStruct((1, D_RNN), F32))
    padded = tm + SUBLANES_V7X * SUBLANES_V7X
    scratch = [pltpu.VMEM((D_MODEL // LANES_V7X, tm, LANES_V7X), F32),
               pltpu.VMEM((D_RNN // LANES_V7X, padded, LANES_V7X), F32),
               pltpu.VMEM((D_MODEL // LANES_V7X, padded, LANES_V7X), F32),
               pltpu.VMEM((tail_rows, D_RNN), F32),
               pltpu.VMEM((1, D_RNN), F32)]
    out_shape = tuple(qkv_shapes) + rnn_shapes + (
        jax.ShapeDtypeStruct((m, D_MODEL), F32),
        jax.ShapeDtypeStruct((m, D_MODEL), F32),
    ) + tuple(kvt_shapes)
    consts = (g1, w_in, qg, kg) + tuple(lru)
    cast_specs, cast_shapes = [], []
    bf16_rows = 2 * SUBLANES_V7X
    for w in to_bf16:
        nblk = nt
        while w.shape[0] % (nblk * bf16_rows):
            nblk //= 2
        cast_specs.append(pl.BlockSpec((w.shape[0] // nblk, w.shape[1]),
                                       lambda i, per=nt // nblk: (i // per, 0)))
        cast_shapes.append(jax.ShapeDtypeStruct(w.shape, BF16))
    return pl.pallas_call(
        functools.partial(_inproj_kernel, dilations=dilations, tail_first_steps=tuple(tail_first_steps),
                          n_cast=len(to_bf16)),
        grid=(nt,),
        in_specs=[pl.BlockSpec((tm, D_MODEL), lambda i: (0, 0)),
                  pl.BlockSpec((tm, D_MODEL), lambda i: (jnp.minimum(i + 1, nt - 1), 0))]
        + [_const_spec(c.shape) for c in consts] + cast_specs,
        out_specs=(tuple(qkv_specs) + rnn_specs + (row(D_MODEL), row(D_MODEL)) + tuple(kvt_specs)
                   + tuple(cast_specs)),
        out_shape=out_shape + tuple(cast_shapes),
        scratch_shapes=scratch,
        compiler_params=_params(("arbitrary",)),
        name="inproj_lru",
    )(x, x, *consts, *to_bf16)


def _sample_inproj_kernel(x_ref, g1_ref, qg_ref, kg_ref, w_ref, proj_ref, wb_ref):
    wb = w_ref[...].astype(BF16)
    wb_ref[...] = wb
    t = _mm(_rms_norm_rows(x_ref[...], g1_ref[...]).astype(BF16), wb)
    for part in range(w_ref.shape[1] // GROUP_W):
        block = pl.program_id(0) * (w_ref.shape[1] // GROUP_W) + part
        cols = slice(part * GROUP_W, (part + 1) * GROUP_W)
        normed = _head_rms_norm(t[:, cols])
        proj_ref[:, cols] = jnp.where(block < N_GROUPS, normed * qg_ref[...],
                                      jnp.where(block < 2 * N_GROUPS, normed * kg_ref[...], t[:, cols]))


def _sample_inproj(x, g1, w_in, qg, kg):
    nb = x.shape[0]
    width = 2 * GROUP_W
    consts = (x, g1, qg, kg)
    col = lambda rows: pl.BlockSpec((rows, width), lambda j: (0, j))
    return pl.pallas_call(
        _sample_inproj_kernel,
        grid=(w_in.shape[1] // width,),
        in_specs=[_const_spec(c.shape) for c in consts] + [col(D_MODEL)],
        out_specs=(col(nb), col(D_MODEL)),
        out_shape=(jax.ShapeDtypeStruct((nb, w_in.shape[1]), F32),
                   jax.ShapeDtypeStruct(w_in.shape, BF16)),
        compiler_params=_params(("arbitrary",)),
        name="inproj_sample",
    )(*consts, w_in)


ATTN_BLOCKS_PER_STEP = 8


def _attn_prompt_kernel(q_ref, kc_ref, kp_ref, vc_ref, vp_ref, *refs, sample_first_row):
    sample_stages = {}
    if sample_first_row is None:
        o_ref, lse_ref = refs
    else:
        qs_ref, ks_ref, vs_ref, c0_ref, c1_ref, c2_ref, o_ref, lse_ref, os_ref = refs
        flat_step = pl.program_id(0) * pl.num_programs(1) + pl.program_id(1)
        stages = _attn_sample_stages(flat_step, sample_first_row + flat_step, qs_ref, ks_ref, vs_ref,
                                     (c0_ref, c1_ref, c2_ref), os_ref)
        sample_stages = dict(zip((0, 2, 3, 4, 6), stages))
    step = pl.program_id(1)
    qi = lax.broadcasted_iota(jnp.int32, (BLOCK, 2 * BLOCK), 0)
    kj = lax.broadcasted_iota(jnp.int32, (BLOCK, 2 * BLOCK), 1)
    dist = BLOCK + qi - kj
    band = (dist >= 0) & (dist <= BLOCK)
    first_band = band & ((step > 0) | (kj >= BLOCK))
    band2 = jnp.concatenate([band, band], axis=0)
    first_band2 = jnp.concatenate([first_band, first_band], axis=0)
    first_head = lax.broadcasted_iota(jnp.int32, (1, LANES_V7X), 1) < HEAD_DIM
    zero = jnp.zeros((), BF16)

    for blk in range(q_ref.shape[0] // BLOCK):
        if blk in sample_stages:
            sample_stages[blk]()
        rows = slice(blk * BLOCK, (blk + 1) * BLOCK)
        prev_rows = slice((blk - 1) * BLOCK, blk * BLOCK)
        valid = band2 if blk else first_band2
        lse_ref[rows, :] = jnp.zeros((BLOCK, LANES_V7X), F32)
        for pair in range(HEADS_PER_GROUP // 2):
            cols = slice(pair * LANES_V7X, (pair + 1) * LANES_V7X)
            qp = q_ref[rows, cols]
            k_prev = kc_ref[prev_rows, cols] if blk else kp_ref[:, cols]
            v_prev = vc_ref[prev_rows, cols] if blk else vp_ref[:, cols]
            kk = jnp.concatenate([k_prev, kc_ref[rows, cols]], axis=0)
            vv = jnp.concatenate([v_prev, vc_ref[rows, cols]], axis=0)
            q2 = jnp.concatenate([jnp.where(first_head, qp, zero), jnp.where(first_head, zero, qp)], axis=0)
            s = jnp.where(valid, _mm_nt(q2, kk), NEG_INF)
            mx = jnp.max(s, axis=-1, keepdims=True)
            p = jnp.exp2(s - mx)
            den = jnp.sum(p, axis=-1, keepdims=True)
            pv = _mm(p.astype(BF16), vv)
            o_ref[rows, cols] = jnp.where(first_head, pv[:BLOCK], pv[BLOCK:]).astype(o_ref.dtype)
            for e, (lo, hi) in enumerate(((0, BLOCK), (BLOCK, 2 * BLOCK))):
                head = 2 * pair + e
                lse_ref[rows, head:head + 1] = mx[lo:hi]
                lse_ref[rows, HEADS_PER_GROUP + head:HEADS_PER_GROUP + head + 1] = den[lo:hi]


def _attn_prompt(q, k, v, g, dilation, sample=None):
    m_len = q.shape[0]
    nbs = ATTN_BLOCKS_PER_STEP if sample is not None else 2 * ATTN_BLOCKS_PER_STEP
    rows = nbs * BLOCK
    n_inner = m_len // rows
    cur = pl.BlockSpec((rows, GROUP_W), lambda r, b: (b, r))
    prev = pl.BlockSpec((BLOCK, GROUP_W), lambda r, b: (jnp.maximum(b * nbs - 1, 0), r))
    in_specs, args = [cur, cur, prev, cur, prev], [q, k, k, v, v]
    out_specs = [pl.BlockSpec((rows, GROUP_W), lambda r, b: (b, r)),
                 pl.BlockSpec((rows, LANES_V7X), lambda r, b: (b, r))]
    out_shape = [jax.ShapeDtypeStruct((m_len, dilation * GROUP_W), BF16),
                 jax.ShapeDtypeStruct((m_len, dilation * LANES_V7X), F32)]
    first_row = None
    if sample is not None:
        qs, ks, vs, caches, first_row = sample
        steps = dilation * n_inner
        full = pl.BlockSpec(qs.shape, lambda r, b: (0, 0, 0))
        in_specs += [full, full, full] + [
            pl.BlockSpec((None,) + c.shape[1:], lambda r, b: (first_row + r * n_inner + b, 0, 0))
            for c in caches]
        args += [qs, ks, vs, *caches]
        out_specs.append(pl.BlockSpec((steps, GROUP_W), lambda r, b: (0, 0)))
        out_shape.append(jax.ShapeDtypeStruct((steps, GROUP_W), F32))
    return pl.pallas_call(
        functools.partial(_attn_prompt_kernel, sample_first_row=first_row),
        grid=(dilation, n_inner),
        in_specs=in_specs,
        out_specs=tuple(out_specs),
        out_shape=tuple(out_shape),
        compiler_params=_params(("arbitrary", "arbitrary")),
        name=f"attn_prompt_g{g}",
    )(*args)


def _attn_sample_stages(out_row, b, q_ref, k_ref, v_ref, cache_refs, o_ref):
    head_row = lax.broadcasted_iota(jnp.int32, (HEADS_PER_GROUP, GROUP_W), 0)
    head_lane = lax.broadcasted_iota(jnp.int32, (HEADS_PER_GROUP, GROUP_W), 1) // HEAD_DIM
    own = head_row == head_lane
    bf = lambda t: t.astype(BF16).astype(F32)
    st = [dict(c_ref=c_ref) for c_ref in cache_refs]

    def scores():
        for g, d in enumerate(st):
            window, dilation = GROUPS[g]
            qmat = jnp.where(own, jnp.broadcast_to(q_ref[g, pl.ds(b, 1), :], (HEADS_PER_GROUP, GROUP_W)), 0.0)
            qmat = qmat.astype(BF16)
            knew = bf(k_ref[g, pl.ds(b, 1), :])
            d["vnew"] = bf(v_ref[g, pl.ds(b, 1), :])
            pos = lax.broadcasted_iota(jnp.int32, (HEADS_PER_GROUP, window), 1)
            s = _mm(qmat, d["c_ref"][0:GROUP_W, :].astype(BF16))
            d["s"] = jnp.where((pos & (dilation - 1)) == 0, s, NEG_INF)
            d["s_new"] = jnp.sum(qmat.astype(F32) * knew, axis=-1, keepdims=True)

    def maxima():
        for d in st:
            d["mx"] = jnp.maximum(jnp.max(d["s"], axis=-1, keepdims=True), d["s_new"])

    def probabilities():
        for d in st:
            d["p"] = jnp.exp2(d["s"] - d["mx"])
            d["p_new"] = jnp.exp2(d["s_new"] - d["mx"])
            d["den"] = jnp.sum(d["p"], axis=-1, keepdims=True) + d["p_new"]

    def values():
        for d in st:
            numer = (_mm_nt(d["p"].astype(BF16), d["c_ref"][GROUP_W:2 * GROUP_W, :].astype(BF16))
                     + bf(d["p_new"]) * d["vnew"])
            d["numer"] = jnp.where(own, numer, 0.0)

    def mixture():
        m_all = functools.reduce(jnp.maximum, [d["mx"] for d in st])
        ws = [jnp.exp2(d["mx"] - m_all) for d in st]
        num = sum(d["numer"] * w for d, w in zip(st, ws))
        den = sum(d["den"] * w for d, w in zip(st, ws))
        o_ref[pl.ds(out_row, 1), :] = jnp.sum(num / den, axis=0, keepdims=True)

    return [scores, maxima, probabilities, values, mixture]


def _gelu_tanh(x):
    cdf = 0.5 * (1.0 + jnp.tanh(0.7978845608028654 * (x + 0.044715 * (x * x * x))))
    return x * cdf


def _softplus(x):
    return jnp.maximum(x, 0.0) + jnp.log1p(jnp.exp(-jnp.abs(x)))


LRU_CHUNK = 2 * RNN_BLOCK_W
N_LRU_CHUNKS = D_RNN // LRU_CHUNK


def _lru_lanes(n):
    return slice(n * LRU_CHUNK, (n + 1) * LRU_CHUNK)


def _lru_coeffs(xc, n, wgate_ref, brg_ref, big_ref, lam_ref):
    lanes = _lru_lanes(n)
    sigmoid = lambda v: 0.5 + 0.5 * jnp.tanh(0.5 * v)
    logits = _mm(xc.astype(BF16), wgate_ref[n])
    r = sigmoid(logits[:, :LRU_CHUNK] + brg_ref[:, lanes])
    i = sigmoid(logits[:, LRU_CHUNK:] + big_ref[:, lanes])
    neg_rate = LRU_C * _softplus(-lam_ref[:, lanes])
    a = jnp.exp2(r * (-LOG2_E * neg_rate))
    t = jnp.tanh(r * neg_rate)
    y = 2.0 * t / (1.0 + t)
    root = jnp.where(y > 0.0, y * lax.rsqrt(y), 0.0)
    return a, root * i * xc


def _gate_weights(w_rg, w_ig):
    def pairs(w):
        z = jnp.zeros_like(w[0::2])
        top = jnp.concatenate([w[0::2], z], axis=2)
        bot = jnp.concatenate([z, w[1::2]], axis=2)
        return jnp.concatenate([top, bot], axis=1)
    return jnp.concatenate([pairs(w_rg), pairs(w_ig)], axis=2)


def _lru_tile(xb, gb, n, lru_refs, ctail_ref, hcar_ref):
    cw_ref, cb_ref, wgate_ref, brg_ref, big_ref, lam_ref = lru_refs
    lanes = _lru_lanes(n)
    sl = SUBLANES_V7X
    tm = xb.shape[0]
    nj = tm // sl
    sub = lax.broadcasted_iota(jnp.int32, (sl, LRU_CHUNK), 0)
    vrow = lambda t, j: t[j * sl:(j + 1) * sl]

    taps = CONV_W - 1
    wrap = [pltpu.roll(jnp.where(sub == sl - 1, vrow(ctail_ref[:, lanes], i), vrow(xb, nj - taps + i)), 1, 0)
            for i in range(taps)]
    ctail_ref[:, lanes] = xb[tm - taps * sl:]
    xc = cb_ref[:, lanes] + xb * cw_ref[taps:taps + 1, lanes]
    for k in range(1, CONV_W):
        shifted = jnp.concatenate(wrap[taps - k:] + [xb[:tm - k * sl]], axis=0)
        xc = xc + shifted * cw_ref[taps - k:taps - k + 1, lanes]

    a, b = _lru_coeffs(xc, n, wgate_ref, brg_ref, big_ref, lam_ref)
    gate = _gelu_tanh(gb)

    hl, acc = vrow(b, 0), vrow(a, 0)
    hls, accs = [hl], [acc]
    for j in range(1, nj):
        hl = vrow(a, j) * hl + vrow(b, j)
        acc = vrow(a, j) * acc
        hls.append(hl)
        accs.append(acc)

    for s in (1, 2, 4):
        keep = sub >= s
        acc_prev = jnp.where(keep, pltpu.roll(acc, s, 0), 1.0)
        hl_prev = jnp.where(keep, pltpu.roll(hl, s, 0), 0.0)
        hl = acc * hl_prev + hl
        acc = acc * acc_prev
    carry = hcar_ref[:, lanes]
    h_end = hl + acc * carry
    h_in = jnp.where(sub == 0, carry, pltpu.roll(h_end, 1, 0))
    hcar_ref[:, lanes] = h_end[sl - 1:sl]

    return [(hls[j] + accs[j] * h_in) * vrow(gate, j) for j in range(nj)]


def _rglru_sample_kernel(xb_ref, gb_ref, sc_ref, h0_ref, cw_ref, cb_ref, wgate_ref, brg_ref, big_ref,
                         lam_ref, hb_ref, conv_ref, h_ref):
    xb = xb_ref[...]
    taps = [sc_ref[:, j * D_RNN:(j + 1) * D_RNN] for j in range(CONV_W - 1)] + [xb]
    xc = cb_ref[...] + sum(t * cw_ref[j:j + 1, :] for j, t in enumerate(taps))
    for n in range(N_LRU_CHUNKS):
        lanes = _lru_lanes(n)
        a, b = _lru_coeffs(xc[:, lanes], n, wgate_ref, brg_ref, big_ref, lam_ref)
        h = a * h0_ref[:, lanes] + b
        h_ref[:, lanes] = h
        hb_ref[:, lanes] = (h * _gelu_tanh(gb_ref[:, lanes])).astype(hb_ref.dtype)
    for j in range(CONV_W - 1):
        conv_ref[:, j * D_RNN:(j + 1) * D_RNN] = taps[j + 1]


def _rglru_sample(xb, gb, state_conv, h0, cw, cb, wgate, brg, big, lam):
    nb = xb.shape[0]
    args = (xb, gb, state_conv, h0, cw, cb, wgate, brg, big, lam)
    return pl.pallas_call(
        _rglru_sample_kernel,
        grid=(1,),
        in_specs=[_const_spec(a.shape) for a in args],
        out_specs=(_const_spec((nb, D_RNN)), _const_spec((nb, (CONV_W - 1) * D_RNN)),
                   _const_spec((nb, D_RNN))),
        out_shape=(jax.ShapeDtypeStruct((nb, D_RNN), BF16),
                   jax.ShapeDtypeStruct((nb, (CONV_W - 1) * D_RNN), F32),
                   jax.ShapeDtypeStruct((nb, D_RNN), F32)),
        compiler_params=_params(("arbitrary",)),
        name="rglru_sample",
    )(*args)


def _post_kernel(*refs, dilations):
    n_parts = len(dilations)
    attn_refs = refs[:4 * n_parts] if n_parts > 1 else refs[:1]
    rest = refs[len(attn_refs):]
    (hb_ref, ga_ref, gb2_ref, x_ref, expand_ref, bg_ref, woa_ref, wor_ref, wout_ref, g2_ref,
     wfi_ref, wfo_ref, y_ref) = rest[:13]
    tm = x_ref.shape[0]

    if n_parts > 1:
        o_nat_ref, lse_nat_ref, attn_ref = rest[13:]

        def natural(src_ref, dst_ref, width, d):
            if d == 1:
                return src_ref[...].astype(F32)
            n_slabs = width // LANES_V7X
            for r in range(d):
                for c in range(n_slabs):
                    lanes = slice(r * width + c * LANES_V7X, r * width + (c + 1) * LANES_V7X)
                    dst_ref[c, pl.ds(r, tm // d, stride=d), :] = src_ref[:, lanes].astype(F32)
            return jnp.concatenate([dst_ref[c] for c in range(n_slabs)], axis=1)

        def stage_mixture(o_refs, stat_refs):
            stats = [natural(r, lse_nat_ref, LANES_V7X, d) for r, d in zip(stat_refs, dilations)]
            m_all = functools.reduce(jnp.maximum, stats)
            ws = [jnp.exp2(st - m_all) for st in stats]
            dens = [pltpu.roll(st, LANES_V7X - HEADS_PER_GROUP, 1) for st in stats]
            inv = 1.0 / sum(w * dn for w, dn in zip(ws, dens))
            is_head = lax.broadcasted_iota(jnp.int32, (1, LANES_V7X), 1) < HEADS_PER_GROUP
            attn = 0.0
            for o_ref, w, d in zip(o_refs, ws, dilations):
                cw = jnp.where(is_head, w * inv, 0.0)
                hi = cw.astype(BF16)
                lo = (cw - hi.astype(F32)).astype(BF16)
                wide = _mm(jnp.concatenate([hi, lo], axis=1), expand_ref[...])
                attn = attn + wide * natural(o_ref, o_nat_ref, GROUP_W, d)
            attn_ref[...] = attn.astype(BF16)

        first, nxt = attn_refs[:2 * n_parts], attn_refs[2 * n_parts:]

        @pl.when(pl.program_id(0) == 0)
        def _():
            stage_mixture(first[:n_parts], first[n_parts:])

        attn = attn_ref[...]
    else:
        attn = attn_refs[0][...].astype(BF16)

    ya = _mm(attn, woa_ref[...])
    yb = _mm(hb_ref[...], wor_ref[...])
    merged = (jax.nn.sigmoid(ga_ref[...] + bg_ref[0:1, :]) * ya
              + jax.nn.sigmoid(gb2_ref[...] + bg_ref[1:2, :]) * yb)
    x1 = x_ref[...] + _mm(merged.astype(BF16), wout_ref[...])
    hn2 = _rms_norm_rows(x1, g2_ref[...]).astype(BF16)
    gu = _mm(hn2, wfi_ref[...])
    act = jax.nn.silu(gu[:, :D_FF]) * gu[:, D_FF:]
    y_ref[...] = x1 + _mm(act.astype(BF16), wfo_ref[...])
    if n_parts > 1:
        stage_mixture(nxt[:n_parts], nxt[n_parts:])


def _post(attn_parts, dilations, hb, ga, gb2, x, expand, bg, woa, wor, wout, g2, wfi, wfo, *, tm):
    m = x.shape[0]
    row = lambda w: pl.BlockSpec((tm, w), lambda i: (i, 0))
    n_parts = len(attn_parts)
    scratch = []
    if n_parts > 1:
        nt = m // tm
        first = lambda w, d: pl.BlockSpec((tm // d, d * w), lambda i: (0, 0))
        nxt = lambda w, d: pl.BlockSpec((tm // d, d * w), lambda i: (jnp.minimum(i + 1, nt - 1), 0))
        attn_args = 2 * ([o for o, _ in attn_parts] + [l for _, l in attn_parts])
        attn_specs = [spec(w, d) for spec in (first, nxt) for w in (GROUP_W, LANES_V7X) for d in dilations]
        scratch = [pltpu.VMEM((GROUP_W // LANES_V7X, tm, LANES_V7X), F32),
                   pltpu.VMEM((1, tm, LANES_V7X), F32),
                   pltpu.VMEM((tm, GROUP_W), BF16)]
    else:
        attn_args, attn_specs = list(attn_parts), [row(GROUP_W)]
    consts = (expand, bg, woa, wor, wout, g2, wfi, wfo)
    return pl.pallas_call(
        functools.partial(_post_kernel, dilations=dilations),
        grid=(m // tm,),
        in_specs=attn_specs + [row(D_RNN), row(D_MODEL), row(D_MODEL), row(D_MODEL)]
        + [_const_spec(c.shape) for c in consts],
        out_specs=row(D_MODEL),
        out_shape=jax.ShapeDtypeStruct((m, D_MODEL), F32),
        scratch_shapes=scratch,
        compiler_params=_params(("arbitrary",)),
        name="post_prompt" if n_parts > 1 else "post_sample",
    )(*attn_args, hb, ga, gb2, x, *consts)


def kernel(x_prompt, x_sample, cache_kv_w128, cache_kv_w512, cache_kv_w2048, state_conv, state_h,
           norm1_g, w_in, b_gate, q_norm_g, k_norm_g, conv_w, conv_b, w_rg, b_rg, w_ig, b_ig,
           lru_lambda, w_o_attn, w_o_rnn, w_out, norm2_g, w_ffn_in, w_ffn_out):
    assert x_prompt.shape[0] == 1 and norm1_g.shape[0] == 1 and x_sample.shape[1] == 1
    seq = x_prompt.shape[1]
    nb = x_sample.shape[0]
    layer = 0

    expand = (jnp.arange(LANES_V7X)[:, None] == (jnp.arange(GROUP_W) // HEAD_DIM)[None, :]).astype(BF16)
    expand = jnp.concatenate([expand, expand], axis=0)

    row2 = lambda t: t[layer].reshape(1, -1)
    g1, g2 = row2(norm1_g), row2(norm2_g)
    qg = jnp.tile(q_norm_g[layer], HEADS_PER_GROUP).reshape(1, GROUP_W) * (
        HEAD_NORM_GAIN * ATTN_SCALE * LOG2_E)
    kg = jnp.tile(k_norm_g[layer], HEADS_PER_GROUP).reshape(1, GROUP_W) * HEAD_NORM_GAIN
    wgate = _gate_weights(w_rg[layer].astype(BF16), w_ig[layer].astype(BF16))
    lru = (conv_w[layer], row2(conv_b), wgate, row2(b_rg), row2(b_ig), row2(lru_lambda))
    post_f32 = tuple(w[layer] for w in (w_o_attn, w_o_rnn, w_out, w_ffn_in, w_ffn_out))

    xs = x_sample[:, 0]
    proj, w_in_b = _sample_inproj(xs, g1, w_in[layer], qg, kg)
    qkvs = proj[:, :OFF_XB].reshape(nb, 3 * N_GROUPS, GROUP_W).transpose(1, 0, 2)
    qs, ks, vs = qkvs[:N_GROUPS], qkvs[N_GROUPS:2 * N_GROUPS], qkvs[2 * N_GROUPS:]
    xbs, gbs, gas, gb2s = (proj[:, a:b] for a, b in
                           ((OFF_XB, OFF_GB), (OFF_GB, OFF_GA), (OFF_GA, OFF_GB2), (OFF_GB2, proj.shape[1])))
    caches = [jnp.transpose(c[layer], (0, 2, 3, 4, 1)).reshape(nb, 2 * GROUP_W, c.shape[2])
              for c in (cache_kv_w128, cache_kv_w512, cache_kv_w2048)]

    xp = x_prompt[0]
    *qkv, hb, conv_rows, h_last, ga, gb2, kvp0, kvp1, kvp2, woa, wor, wout, wfi, wfo = _inproj(
        xp, g1, w_in_b, qg, kg, lru, post_f32, tm=PROMPT_ROW_TILE,
        tails=tuple(min(w, seq) for w, _ in GROUPS), dilations=DILATIONS)
    post_w = (expand, b_gate[layer], woa, wor, wout, g2, wfi, wfo)
    half = nb // 2
    parts, attn_s = [], []
    for g, d in enumerate(DILATIONS):
        sample = (qs, ks, vs, caches, (g - 1) * half) if g else None
        *part, = _attn_prompt(*qkv[3 * g:3 * g + 3], g, d, sample)
        parts.append(part[:2])
        attn_s += part[2:]
    attn_s = jnp.concatenate(attn_s, axis=0)
    assert attn_s.shape == (nb, GROUP_W)
    y_p = _post(parts, DILATIONS, hb, ga, gb2, xp, *post_w, tm=PROMPT_ROW_TILE)
    conv_p = conv_rows[SUBLANES_V7X - 1::SUBLANES_V7X]

    hbs, conv_s, h_s = _rglru_sample(xbs, gbs, state_conv[layer].reshape(nb, -1), state_h[layer], *lru)
    y_s = _post([attn_s], (1,), hbs, gas, gb2s, xs, *post_w, tm=nb)

    kv_prompt = [t.reshape(1, 1, t.shape[0], 2, HEADS_PER_GROUP, HEAD_DIM) for t in (kvp0, kvp1, kvp2)]
    kv_sample = [jnp.stack([ks[g], vs[g]], axis=1).reshape(1, nb, 1, 2, HEADS_PER_GROUP, HEAD_DIM)
                 for g in range(N_GROUPS)]
    return (y_p[None], y_s[:, None],
            kv_prompt[0], kv_prompt[1], kv_prompt[2],
            conv_p[None, None], h_last[None],
            kv_sample[0], kv_sample[1], kv_sample[2],
            conv_s.reshape(1, nb, CONV_W - 1, D_RNN), h_s[None])
```

```python
import functools

import jax
import jax.numpy as jnp
from jax import lax
from jax.experimental import pallas as pl
from jax.experimental.pallas import tpu as pltpu

F32 = jnp.float32
BF16 = jnp.bfloat16

D_MODEL = 1024
HEAD_DIM = 64
HEADS_PER_GROUP = 8
GROUPS = ((128, 1), (512, 4), (2048, 16))
DILATIONS = tuple(d for _, d in GROUPS)
N_GROUPS = len(GROUPS)
GROUP_W = HEADS_PER_GROUP * HEAD_DIM
QKV_WIDTH = N_GROUPS * GROUP_W
BLOCK = 128
ATTN_SCALE = HEAD_DIM ** -0.5
LOG2_E = 1.4426950408889634
NEG_INF = -1e30
D_RNN = 1280
RNN_BLOCKS = 10
RNN_BLOCK_W = D_RNN // RNN_BLOCKS
CONV_W = 4
LRU_C = 8.0
D_FF = 2816
RMS_EPS = 1e-6

LANES_V7X = 128
SUBLANES_V7X = 8
MXU_DIM_V7X = 256
VMEM_LIMIT_BYTES = 56 * 1024 * 1024
PROMPT_ROW_TILE = MXU_DIM_V7X

OFF_Q, OFF_K, OFF_V = 0, QKV_WIDTH, 2 * QKV_WIDTH
OFF_XB = 3 * QKV_WIDTH
OFF_GB = OFF_XB + D_RNN
OFF_GA = OFF_GB + D_RNN
OFF_GB2 = OFF_GA + D_MODEL


def _mm(a, b):
    return jnp.dot(a, b, preferred_element_type=F32)


def _mm_nt(a, b):
    return lax.dot_general(a, b, (((1,), (1,)), ((), ())), preferred_element_type=F32)


def _rms_norm_rows(x, g):
    return x * lax.rsqrt(jnp.mean(x * x, axis=-1, keepdims=True) + RMS_EPS) * g


HEAD_NORM_GAIN = HEAD_DIM ** 0.5


def _head_rms_norm(t):
    first_head = lax.broadcasted_iota(jnp.int32, (1, LANES_V7X), 1) < HEAD_DIM
    cols = []
    for c in range(0, t.shape[1], LANES_V7X):
        x = t[:, c:c + LANES_V7X]
        xx = x * x
        s0 = jnp.sum(jnp.where(first_head, xx, 0.0), axis=-1, keepdims=True)
        s1 = jnp.sum(jnp.where(first_head, 0.0, xx), axis=-1, keepdims=True)
        cols.append(x * lax.rsqrt(jnp.where(first_head, s0, s1) + HEAD_DIM * RMS_EPS))
    return jnp.concatenate(cols, axis=1)


def _const_spec(shape):
    nd = len(shape)
    return pl.BlockSpec(shape, lambda *_: (0,) * nd, pipeline_mode=pl.Buffered(1))


def _params(sem):
    return pltpu.CompilerParams(dimension_semantics=sem, vmem_limit_bytes=VMEM_LIMIT_BYTES)


def _inproj_kernel(x0_ref, xn_ref, g1_ref, w_ref, qg_ref, kg_ref, *refs, dilations, tail_first_steps,
                   n_cast):
    lru_refs, refs = refs[:6], refs[6:]
    cast_in, refs = refs[:n_cast], refs[n_cast:]
    qkv_refs, refs = refs[:3 * N_GROUPS], refs[3 * N_GROUPS:]
    (hb_ref, conv_ref, hlast_ref, ga_ref, gb2_ref), refs = refs[:5], refs[5:]
    kvt_refs, refs = refs[:N_GROUPS], refs[N_GROUPS:]
    cast_out, refs = refs[:n_cast], refs[n_cast:]
    hn_ref, hbs_ref, hnp_ref, ctail_ref, hcar_ref = refs
    for src_ref, dst_ref in zip(cast_in, cast_out):
        dst_ref[...] = src_ref[...].astype(dst_ref.dtype)
    step = pl.program_id(0)
    tm = xn_ref.shape[0]
    n_slabs = D_MODEL // LANES_V7X
    nj = tm // SUBLANES_V7X
    pitch = nj + SUBLANES_V7X

    def stage_normed_rows(x_ref):
        hn32 = _rms_norm_rows(x_ref[...], g1_ref[...])
        for c in range(n_slabs):
            lanes = slice(c * LANES_V7X, (c + 1) * LANES_V7X)
            hn_ref[c] = hn32[:, lanes]
            for s in range(SUBLANES_V7X):
                hnp_ref[c, s * pitch:s * pitch + nj, :] = hn32[s * nj:(s + 1) * nj, lanes]

    @pl.when(step == 0)
    def _():
        ctail_ref[...] = jnp.zeros(ctail_ref.shape, F32)
        hcar_ref[...] = jnp.zeros(hcar_ref.shape, F32)
        stage_normed_rows(x0_ref)

    hn = jnp.concatenate([hn_ref[c] for c in range(n_slabs)], axis=1).astype(BF16)

    def strided_rows(start, size, stride):
        return jnp.concatenate([hn_ref[c, pl.ds(start, size, stride=stride), :] for c in range(n_slabs)],
                               axis=1)

    ht = jnp.concatenate(
        [jnp.concatenate([hnp_ref[c, pl.ds(j, SUBLANES_V7X, stride=pitch), :] for c in range(n_slabs)],
                         axis=1) for j in range(nj)], axis=0).astype(BF16)
    per_vreg = LRU_CHUNK // LANES_V7X
    xg = _mm(ht, w_ref[:, OFF_XB:OFF_GA])
    for n in range(N_LRU_CHUNKS):
        c0 = n * LRU_CHUNK
        hb_rows = _lru_tile(xg[:, c0:c0 + LRU_CHUNK], xg[:, D_RNN + c0:D_RNN + c0 + LRU_CHUNK],
                            n, lru_refs, ctail_ref, hcar_ref)
        for j, rows in enumerate(hb_rows):
            for c in range(per_vreg):
                hbs_ref[n * per_vreg + c, pl.ds(j, SUBLANES_V7X, stride=pitch), :] = (
                    rows[:, c * LANES_V7X:(c + 1) * LANES_V7X])
    conv_ref[...] = ctail_ref[...]
    hlast_ref[...] = hcar_ref[...]
    hb_ref[...] = jnp.concatenate(
        [jnp.concatenate([hbs_ref[c, s * pitch:s * pitch + nj, :] for s in range(SUBLANES_V7X)], axis=0)
         for c in range(D_RNN // LANES_V7X)], axis=1).astype(hb_ref.dtype)

    def head_norm(t, gain):
        return _head_rms_norm(t) * gain

    def qkv(h, g):
        c = g * GROUP_W
        qn = head_norm(_mm(h, w_ref[:, OFF_Q + c:OFF_Q + c + GROUP_W]), qg_ref[...])
        kn = head_norm(_mm(h, w_ref[:, OFF_K + c:OFF_K + c + GROUP_W]), kg_ref[...])
        vv = _mm(h, w_ref[:, OFF_V + c:OFF_V + c + GROUP_W])
        return qn, kn, vv

    in_order = {}
    for g, d in enumerate(dilations):
        rows = tm // d
        if d == 1:
            hg = hn
        else:
            hg = jnp.concatenate([strided_rows(r, rows, d) for r in range(d)], axis=0).astype(BF16)
        parts = qkv(hg, g)
        if d == 1:
            in_order[g] = parts
        for t, o_ref in zip(parts, qkv_refs[3 * g:3 * g + 3]):
            for r in range(d):
                o_ref[:, r * GROUP_W:(r + 1) * GROUP_W] = t[r * rows:(r + 1) * rows].astype(o_ref.dtype)

    gates = _mm(hn, w_ref[:, OFF_GA:OFF_GB2 + D_MODEL])
    ga_ref[...] = gates[:, :D_MODEL]
    gb2_ref[...] = gates[:, D_MODEL:]

    stage_normed_rows(xn_ref)

    for g, (kvt_ref, first_step) in enumerate(zip(kvt_refs, tail_first_steps)):
        @pl.when(step >= first_step)
        def _():
            _, kn, vv = in_order[g] if g in in_order else qkv(hn, g)
            keep = kvt_ref.shape[0]
            kvt_ref[:, 0:GROUP_W] = kn[tm - keep:]
            kvt_ref[:, GROUP_W:2 * GROUP_W] = vv[tm - keep:]


def _inproj(x, g1, w_in, qg, kg, lru, to_bf16, *, tm, tails, dilations):
    m = x.shape[0]
    nt = m // tm
    row = lambda w: pl.BlockSpec((tm, w), lambda i: (i, 0))
    fixed = lambda r, w: pl.BlockSpec((r, w), lambda i: (0, 0))
    qkv_specs, qkv_shapes = [], []
    for d in dilations:
        qkv_specs += [pl.BlockSpec((tm // d, d * GROUP_W), lambda i: (i, 0))] * 3
        qkv_shapes += [jax.ShapeDtypeStruct((m // d, d * GROUP_W), BF16)] * 3
    kvt_specs, kvt_shapes, tail_first_steps = [], [], []
    for rows in tails:
        blk = min(rows, tm)
        first = nt - rows // blk
        kvt_specs.append(pl.BlockSpec((blk, 2 * GROUP_W), lambda i, first=first: (jnp.maximum(i - first, 0), 0)))
        kvt_shapes.append(jax.ShapeDtypeStruct((rows, 2 * GROUP_W), F32))
        tail_first_steps.append(first)
    tail_rows = (CONV_W - 1) * SUBLANES_V7X
    rnn_specs = (row(D_RNN), fixed(tail_rows, D_RNN), fixed(1, D_RNN))
    rnn_shapes = (jax.ShapeDtypeStruct((m, D_RNN), BF16),
                  jax.ShapeDtypeStruct((tail_rows, D_RNN), F32),
                  jax.ShapeDtypeStruct((1, D_RNN), F32))
    padded = tm + SUBLANES_V7X * SUBLANES_V7X
    scratch = [pltpu.VMEM((D_MODEL // LANES_V7X, tm, LANES_V7X), F32),
               pltpu.VMEM((D_RNN // LANES_V7X, padded, LANES_V7X), F32),
               pltpu.VMEM((D_MODEL // LANES_V7X, padded, LANES_V7X), F32),
               pltpu.VMEM((tail_rows, D_RNN), F32),
               pltpu.VMEM((1, D_RNN), F32)]
    out_shape = tuple(qkv_shapes) + rnn_shapes + (
        jax.ShapeDtypeStruct((m, D_MODEL), F32),
        jax.ShapeDtypeStruct((m, D_MODEL), F32),
    ) + tuple(kvt_shapes)
    consts = (g1, w_in, qg, kg) + tuple(lru)
    cast_specs, cast_shapes = [], []
    bf16_rows = 2 * SUBLANES_V7X
    for w in to_bf16:
        nblk = nt
        while w.shape[0] % (nblk * bf16_rows):
            nblk //= 2
        cast_specs.append(pl.BlockSpec((w.shape[0] // nblk, w.shape[1]),
                                       lambda i, per=nt // nblk: (i // per, 0)))
        cast_shapes.append(jax.ShapeDtypeStruct(w.shape, BF16))
    return pl.pallas_call(
        functools.partial(_inproj_kernel, dilations=dilations, tail_first_steps=tuple(tail_first_steps),
                          n_cast=len(to_bf16)),
        grid=(nt,),
        in_specs=[pl.BlockSpec((tm, D_MODEL), lambda i: (0, 0)),
                  pl.BlockSpec((tm, D_MODEL), lambda i: (jnp.minimum(i + 1, nt - 1), 0))]
        + [_const_spec(c.shape) for c in consts] + cast_specs,
        out_specs=(tuple(qkv_specs) + rnn_specs + (row(D_MODEL), row(D_MODEL)) + tuple(kvt_specs)
                   + tuple(cast_specs)),
        out_shape=out_shape + tuple(cast_shapes),
        scratch_shapes=scratch,
        compiler_params=_params(("arbitrary",)),
        name="inproj_lru",
    )(x, x, *consts, *to_bf16)


def _sample_inproj_kernel(x_ref, g1_ref, qg_ref, kg_ref, w_ref, proj_ref, wb_ref):
    wb = w_ref[...].astype(BF16)
    wb_ref[...] = wb
    t = _mm(_rms_norm_rows(x_ref[...], g1_ref[...]).astype(BF16), wb)
    for part in range(w_ref.shape[1] // GROUP_W):
        block = pl.program_id(0) * (w_ref.shape[1] // GROUP_W) + part
        cols = slice(part * GROUP_W, (part + 1) * GROUP_W)
        normed = _head_rms_norm(t[:, cols])
        proj_ref[:, cols] = jnp.where(block < N_GROUPS, normed * qg_ref[...],
                                      jnp.where(block < 2 * N_GROUPS, normed * kg_ref[...], t[:, cols]))


def _sample_inproj(x, g1, w_in, qg, kg):
    nb = x.shape[0]
    width = 2 * GROUP_W
    consts = (x, g1, qg, kg)
    col = lambda rows: pl.BlockSpec((rows, width), lambda j: (0, j))
    return pl.pallas_call(
        _sample_inproj_kernel,
        grid=(w_in.shape[1] // width,),
        in_specs=[_const_spec(c.shape) for c in consts] + [col(D_MODEL)],
        out_specs=(col(nb), col(D_MODEL)),
        out_shape=(jax.ShapeDtypeStruct((nb, w_in.shape[1]), F32),
                   jax.ShapeDtypeStruct(w_in.shape, BF16)),
        compiler_params=_params(("arbitrary",)),
        name="inproj_sample",
    )(*consts, w_in)


ATTN_BLOCKS_PER_STEP = 8


def _attn_prompt_kernel(q_ref, kc_ref, kp_ref, vc_ref, vp_ref, *refs, sample_first_row):
    sample_stages = {}
    if sample_first_row is None:
        o_ref, lse_ref = refs
    else:
        qs_ref, ks_ref, vs_ref, c0_ref, c1_ref, c2_ref, o_ref, lse_ref, os_ref = refs
        flat_step = pl.program_id(0) * pl.num_programs(1) + pl.program_id(1)
        stages = _attn_sample_stages(flat_step, sample_first_row + flat_step, qs_ref, ks_ref, vs_ref,
                                     (c0_ref, c1_ref, c2_ref), os_ref)
        spread = q_ref.shape[0] // BLOCK // 8
        sample_stages = dict(zip((0, 2 * spread, 3 * spread, 4 * spread, 6 * spread), stages))
    step = pl.program_id(1)
    qi = lax.broadcasted_iota(jnp.int32, (BLOCK, 2 * BLOCK), 0)
    kj = lax.broadcasted_iota(jnp.int32, (BLOCK, 2 * BLOCK), 1)
    dist = BLOCK + qi - kj
    band = (dist >= 0) & (dist <= BLOCK)
    first_band = band & ((step > 0) | (kj >= BLOCK))
    band2 = jnp.concatenate([band, band], axis=0)
    first_band2 = jnp.concatenate([first_band, first_band], axis=0)
    first_head = lax.broadcasted_iota(jnp.int32, (1, LANES_V7X), 1) < HEAD_DIM
    zero = jnp.zeros((), BF16)

    for blk in range(q_ref.shape[0] // BLOCK):
        if blk in sample_stages:
            sample_stages[blk]()
        rows = slice(blk * BLOCK, (blk + 1) * BLOCK)
        prev_rows = slice((blk - 1) * BLOCK, blk * BLOCK)
        valid = band2 if blk else first_band2
        lse_ref[rows, :] = jnp.zeros((BLOCK, LANES_V7X), F32)
        for pair in range(HEADS_PER_GROUP // 2):
            cols = slice(pair * LANES_V7X, (pair + 1) * LANES_V7X)
            qp = q_ref[rows, cols]
            k_prev = kc_ref[prev_rows, cols] if blk else kp_ref[:, cols]
            v_prev = vc_ref[prev_rows, cols] if blk else vp_ref[:, cols]
            kk = jnp.concatenate([k_prev, kc_ref[rows, cols]], axis=0)
            vv = jnp.concatenate([v_prev, vc_ref[rows, cols]], axis=0)
            q2 = jnp.concatenate([jnp.where(first_head, qp, zero), jnp.where(first_head, zero, qp)], axis=0)
            s = jnp.where(valid, _mm_nt(q2, kk), NEG_INF)
            mx = jnp.max(s, axis=-1, keepdims=True)
            p = jnp.exp2(s - mx)
            den = jnp.sum(p, axis=-1, keepdims=True)
            pv = _mm(p.astype(BF16), vv)
            o_ref[rows, cols] = jnp.where(first_head, pv[:BLOCK], pv[BLOCK:]).astype(o_ref.dtype)
            for e, (lo, hi) in enumerate(((0, BLOCK), (BLOCK, 2 * BLOCK))):
                head = 2 * pair + e
                lse_ref[rows, head:head + 1] = mx[lo:hi]
                lse_ref[rows, HEADS_PER_GROUP + head:HEADS_PER_GROUP + head + 1] = den[lo:hi]


def _attn_prompt(q, k, v, g, dilation, sample=None, nbs=2 * ATTN_BLOCKS_PER_STEP):
    m_len = q.shape[0]
    rows = nbs * BLOCK
    n_inner = m_len // rows
    cur = pl.BlockSpec((rows, GROUP_W), lambda r, b: (b, r))
    prev = pl.BlockSpec((BLOCK, GROUP_W), lambda r, b: (jnp.maximum(b * nbs - 1, 0), r))
    in_specs, args = [cur, cur, prev, cur, prev], [q, k, k, v, v]
    out_specs = [pl.BlockSpec((rows, GROUP_W), lambda r, b: (b, r)),
                 pl.BlockSpec((rows, LANES_V7X), lambda r, b: (b, r))]
    out_shape = [jax.ShapeDtypeStruct((m_len, dilation * GROUP_W), BF16),
                 jax.ShapeDtypeStruct((m_len, dilation * LANES_V7X), F32)]
    first_row = None
    if sample is not None:
        qs, ks, vs, caches, first_row = sample
        steps = dilation * n_inner
        full = pl.BlockSpec(qs.shape, lambda r, b: (0, 0, 0))
        in_specs += [full, full, full] + [
            pl.BlockSpec((None,) + c.shape[1:], lambda r, b: (first_row + r * n_inner + b, 0, 0))
            for c in caches]
        args += [qs, ks, vs, *caches]
        out_specs.append(pl.BlockSpec((steps, GROUP_W), lambda r, b: (0, 0)))
        out_shape.append(jax.ShapeDtypeStruct((steps, GROUP_W), F32))
    return pl.pallas_call(
        functools.partial(_attn_prompt_kernel, sample_first_row=first_row),
        grid=(dilation, n_inner),
        in_specs=in_specs,
        out_specs=tuple(out_specs),
        out_shape=tuple(out_shape),
        compiler_params=_params(("arbitrary", "arbitrary")),
        name=f"attn_prompt_g{g}",
    )(*args)


def _attn_sample_stages(out_row, b, q_ref, k_ref, v_ref, cache_refs, o_ref):
    head_row = lax.broadcasted_iota(jnp.int32, (HEADS_PER_GROUP, GROUP_W), 0)
    head_lane = lax.broadcasted_iota(jnp.int32, (HEADS_PER_GROUP, GROUP_W), 1) // HEAD_DIM
    own = head_row == head_lane
    bf = lambda t: t.astype(BF16).astype(F32)
    st = [dict(c_ref=c_ref) for c_ref in cache_refs]

    def scores():
        for g, d in enumerate(st):
            window, dilation = GROUPS[g]
            qmat = jnp.where(own, jnp.broadcast_to(q_ref[g, pl.ds(b, 1), :], (HEADS_PER_GROUP, GROUP_W)), 0.0)
            qmat = qmat.astype(BF16)
            knew = bf(k_ref[g, pl.ds(b, 1), :])
            d["vnew"] = bf(v_ref[g, pl.ds(b, 1), :])
            pos = lax.broadcasted_iota(jnp.int32, (HEADS_PER_GROUP, window), 1)
            s = _mm(qmat, d["c_ref"][0:GROUP_W, :].astype(BF16))
            d["s"] = jnp.where((pos & (dilation - 1)) == 0, s, NEG_INF)
            d["s_new"] = jnp.sum(qmat.astype(F32) * knew, axis=-1, keepdims=True)

    def maxima():
        for d in st:
            d["mx"] = jnp.maximum(jnp.max(d["s"], axis=-1, keepdims=True), d["s_new"])

    def probabilities():
        for d in st:
            d["p"] = jnp.exp2(d["s"] - d["mx"])
            d["p_new"] = jnp.exp2(d["s_new"] - d["mx"])
            d["den"] = jnp.sum(d["p"], axis=-1, keepdims=True) + d["p_new"]

    def values():
        for d in st:
            numer = (_mm_nt(d["p"].astype(BF16), d["c_ref"][GROUP_W:2 * GROUP_W, :].astype(BF16))
                     + bf(d["p_new"]) * d["vnew"])
            d["numer"] = jnp.where(own, numer, 0.0)

    def mixture():
        m_all = functools.reduce(jnp.maximum, [d["mx"] for d in st])
        ws = [jnp.exp2(d["mx"] - m_all) for d in st]
        num = sum(d["numer"] * w for d, w in zip(st, ws))
        den = sum(d["den"] * w for d, w in zip(st, ws))
        o_ref[pl.ds(out_row, 1), :] = jnp.sum(num / den, axis=0, keepdims=True)

    return [scores, maxima, probabilities, values, mixture]


def _gelu_tanh(x):
    cdf = 0.5 * (1.0 + jnp.tanh(0.7978845608028654 * (x + 0.044715 * (x * x * x))))
    return x * cdf


def _softplus(x):
    return jnp.maximum(x, 0.0) + jnp.log1p(jnp.exp(-jnp.abs(x)))


LRU_CHUNK = 2 * RNN_BLOCK_W
N_LRU_CHUNKS = D_RNN // LRU_CHUNK


def _lru_lanes(n):
    return slice(n * LRU_CHUNK, (n + 1) * LRU_CHUNK)


def _lru_coeffs(xc, n, wgate_ref, brg_ref, big_ref, lam_ref):
    lanes = _lru_lanes(n)
    sigmoid = lambda v: 0.5 + 0.5 * jnp.tanh(0.5 * v)
    logits = _mm(xc.astype(BF16), wgate_ref[n])
    r = sigmoid(logits[:, :LRU_CHUNK] + brg_ref[:, lanes])
    i = sigmoid(logits[:, LRU_CHUNK:] + big_ref[:, lanes])
    neg_rate = LRU_C * _softplus(-lam_ref[:, lanes])
    a = jnp.exp2(r * (-LOG2_E * neg_rate))
    t = jnp.tanh(r * neg_rate)
    y = 2.0 * t / (1.0 + t)
    root = jnp.where(y > 0.0, y * lax.rsqrt(y), 0.0)
    return a, root * i * xc


def _gate_weights(w_rg, w_ig):
    def pairs(w):
        z = jnp.zeros_like(w[0::2])
        top = jnp.concatenate([w[0::2], z], axis=2)
        bot = jnp.concatenate([z, w[1::2]], axis=2)
        return jnp.concatenate([top, bot], axis=1)
    return jnp.concatenate([pairs(w_rg), pairs(w_ig)], axis=2)


def _lru_tile(xb, gb, n, lru_refs, ctail_ref, hcar_ref):
    cw_ref, cb_ref, wgate_ref, brg_ref, big_ref, lam_ref = lru_refs
    lanes = _lru_lanes(n)
    sl = SUBLANES_V7X
    tm = xb.shape[0]
    nj = tm // sl
    sub = lax.broadcasted_iota(jnp.int32, (sl, LRU_CHUNK), 0)
    vrow = lambda t, j: t[j * sl:(j + 1) * sl]

    taps = CONV_W - 1
    wrap = [pltpu.roll(jnp.where(sub == sl - 1, vrow(ctail_ref[:, lanes], i), vrow(xb, nj - taps + i)), 1, 0)
            for i in range(taps)]
    ctail_ref[:, lanes] = xb[tm - taps * sl:]
    xc = cb_ref[:, lanes] + xb * cw_ref[taps:taps + 1, lanes]
    for k in range(1, CONV_W):
        shifted = jnp.concatenate(wrap[taps - k:] + [xb[:tm - k * sl]], axis=0)
        xc = xc + shifted * cw_ref[taps - k:taps - k + 1, lanes]

    a, b = _lru_coeffs(xc, n, wgate_ref, brg_ref, big_ref, lam_ref)
    gate = _gelu_tanh(gb)

    hl, acc = vrow(b, 0), vrow(a, 0)
    hls, accs = [hl], [acc]
    for j in range(1, nj):
        hl = vrow(a, j) * hl + vrow(b, j)
        acc = vrow(a, j) * acc
        hls.append(hl)
        accs.append(acc)

    for s in (1, 2, 4):
        keep = sub >= s
        acc_prev = jnp.where(keep, pltpu.roll(acc, s, 0), 1.0)
        hl_prev = jnp.where(keep, pltpu.roll(hl, s, 0), 0.0)
        hl = acc * hl_prev + hl
        acc = acc * acc_prev
    carry = hcar_ref[:, lanes]
    h_end = hl + acc * carry
    h_in = jnp.where(sub == 0, carry, pltpu.roll(h_end, 1, 0))
    hcar_ref[:, lanes] = h_end[sl - 1:sl]

    return [(hls[j] + accs[j] * h_in) * vrow(gate, j) for j in range(nj)]


def _rglru_sample_kernel(xb_ref, gb_ref, sc_ref, h0_ref, cw_ref, cb_ref, wgate_ref, brg_ref, big_ref,
                         lam_ref, hb_ref, conv_ref, h_ref):
    xb = xb_ref[...]
    taps = [sc_ref[:, j * D_RNN:(j + 1) * D_RNN] for j in range(CONV_W - 1)] + [xb]
    xc = cb_ref[...] + sum(t * cw_ref[j:j + 1, :] for j, t in enumerate(taps))
    for n in range(N_LRU_CHUNKS):
        lanes = _lru_lanes(n)
        a, b = _lru_coeffs(xc[:, lanes], n, wgate_ref, brg_ref, big_ref, lam_ref)
        h = a * h0_ref[:, lanes] + b
        h_ref[:, lanes] = h
        hb_ref[:, lanes] = (h * _gelu_tanh(gb_ref[:, lanes])).astype(hb_ref.dtype)
    for j in range(CONV_W - 1):
        conv_ref[:, j * D_RNN:(j + 1) * D_RNN] = taps[j + 1]


def _rglru_sample(xb, gb, state_conv, h0, cw, cb, wgate, brg, big, lam):
    nb = xb.shape[0]
    args = (xb, gb, state_conv, h0, cw, cb, wgate, brg, big, lam)
    return pl.pallas_call(
        _rglru_sample_kernel,
        grid=(1,),
        in_specs=[_const_spec(a.shape) for a in args],
        out_specs=(_const_spec((nb, D_RNN)), _const_spec((nb, (CONV_W - 1) * D_RNN)),
                   _const_spec((nb, D_RNN))),
        out_shape=(jax.ShapeDtypeStruct((nb, D_RNN), BF16),
                   jax.ShapeDtypeStruct((nb, (CONV_W - 1) * D_RNN), F32),
                   jax.ShapeDtypeStruct((nb, D_RNN), F32)),
        compiler_params=_params(("arbitrary",)),
        name="rglru_sample",
    )(*args)


def _post_kernel(*refs, dilations):
    n_parts = len(dilations)
    attn_refs = refs[:4 * n_parts] if n_parts > 1 else refs[:1]
    rest = refs[len(attn_refs):]
    (hb_ref, ga_ref, gb2_ref, x_ref, expand_ref, bg_ref, woa_ref, wor_ref, wout_ref, g2_ref,
     wfi_ref, wfo_ref, y_ref) = rest[:13]
    tm = x_ref.shape[0]

    if n_parts > 1:
        o_nat_ref, lse_nat_ref, attn_ref = rest[13:]

        def natural(src_ref, dst_ref, width, d):
            if d == 1:
                return src_ref[...].astype(F32)
            n_slabs = width // LANES_V7X
            for r in range(d):
                for c in range(n_slabs):
                    lanes = slice(r * width + c * LANES_V7X, r * width + (c + 1) * LANES_V7X)
                    dst_ref[c, pl.ds(r, tm // d, stride=d), :] = src_ref[:, lanes].astype(F32)
            return jnp.concatenate([dst_ref[c] for c in range(n_slabs)], axis=1)

        def stage_mixture(o_refs, stat_refs):
            stats = [natural(r, lse_nat_ref, LANES_V7X, d) for r, d in zip(stat_refs, dilations)]
            m_all = functools.reduce(jnp.maximum, stats)
            ws = [jnp.exp2(st - m_all) for st in stats]
            dens = [pltpu.roll(st, LANES_V7X - HEADS_PER_GROUP, 1) for st in stats]
            inv = 1.0 / sum(w * dn for w, dn in zip(ws, dens))
            is_head = lax.broadcasted_iota(jnp.int32, (1, LANES_V7X), 1) < HEADS_PER_GROUP
            attn = 0.0
            for o_ref, w, d in zip(o_refs, ws, dilations):
                cw = jnp.where(is_head, w * inv, 0.0)
                hi = cw.astype(BF16)
                lo = (cw - hi.astype(F32)).astype(BF16)
                wide = _mm(jnp.concatenate([hi, lo], axis=1), expand_ref[...])
                attn = attn + wide * natural(o_ref, o_nat_ref, GROUP_W, d)
            attn_ref[...] = attn.astype(BF16)

        first, nxt = attn_refs[:2 * n_parts], attn_refs[2 * n_parts:]

        @pl.when(pl.program_id(0) == 0)
        def _():
            stage_mixture(first[:n_parts], first[n_parts:])

        attn = attn_ref[...]
    else:
        attn = attn_refs[0][...].astype(BF16)

    ya = _mm(attn, woa_ref[...])
    yb = _mm(hb_ref[...], wor_ref[...])
    merged = (jax.nn.sigmoid(ga_ref[...] + bg_ref[0:1, :]) * ya
              + jax.nn.sigmoid(gb2_ref[...] + bg_ref[1:2, :]) * yb)
    x1 = x_ref[...] + _mm(merged.astype(BF16), wout_ref[...])
    hn2 = _rms_norm_rows(x1, g2_ref[...]).astype(BF16)
    gu = _mm(hn2, wfi_ref[...])
    act = jax.nn.silu(gu[:, :D_FF]) * gu[:, D_FF:]
    y_ref[...] = x1 + _mm(act.astype(BF16), wfo_ref[...])
    if n_parts > 1:
        stage_mixture(nxt[:n_parts], nxt[n_parts:])


def _post(attn_parts, dilations, hb, ga, gb2, x, expand, bg, woa, wor, wout, g2, wfi, wfo, *, tm):
    m = x.shape[0]
    row = lambda w: pl.BlockSpec((tm, w), lambda i: (i, 0))
    n_parts = len(attn_parts)
    scratch = []
    if n_parts > 1:
        nt = m // tm
        first = lambda w, d: pl.BlockSpec((tm // d, d * w), lambda i: (0, 0))
        nxt = lambda w, d: pl.BlockSpec((tm // d, d * w), lambda i: (jnp.minimum(i + 1, nt - 1), 0))
        attn_args = 2 * ([o for o, _ in attn_parts] + [l for _, l in attn_parts])
        attn_specs = [spec(w, d) for spec in (first, nxt) for w in (GROUP_W, LANES_V7X) for d in dilations]
        scratch = [pltpu.VMEM((GROUP_W // LANES_V7X, tm, LANES_V7X), F32),
                   pltpu.VMEM((1, tm, LANES_V7X), F32),
                   pltpu.VMEM((tm, GROUP_W), BF16)]
    else:
        attn_args, attn_specs = list(attn_parts), [row(GROUP_W)]
    consts = (expand, bg, woa, wor, wout, g2, wfi, wfo)
    return pl.pallas_call(
        functools.partial(_post_kernel, dilations=dilations),
        grid=(m // tm,),
        in_specs=attn_specs + [row(D_RNN), row(D_MODEL), row(D_MODEL), row(D_MODEL)]
        + [_const_spec(c.shape) for c in consts],
        out_specs=row(D_MODEL),
        out_shape=jax.ShapeDtypeStruct((m, D_MODEL), F32),
        scratch_shapes=scratch,
        compiler_params=_params(("arbitrary",)),
        name="post_prompt" if n_parts > 1 else "post_sample",
    )(*attn_args, hb, ga, gb2, x, *consts)


def kernel(x_prompt, x_sample, cache_kv_w128, cache_kv_w512, cache_kv_w2048, state_conv, state_h,
           norm1_g, w_in, b_gate, q_norm_g, k_norm_g, conv_w, conv_b, w_rg, b_rg, w_ig, b_ig,
           lru_lambda, w_o_attn, w_o_rnn, w_out, norm2_g, w_ffn_in, w_ffn_out):
    assert x_prompt.shape[0] == 1 and norm1_g.shape[0] == 1 and x_sample.shape[1] == 1
    seq = x_prompt.shape[1]
    nb = x_sample.shape[0]
    layer = 0

    expand = (jnp.arange(LANES_V7X)[:, None] == (jnp.arange(GROUP_W) // HEAD_DIM)[None, :]).astype(BF16)
    expand = jnp.concatenate([expand, expand], axis=0)

    row2 = lambda t: t[layer].reshape(1, -1)
    g1, g2 = row2(norm1_g), row2(norm2_g)
    qg = jnp.tile(q_norm_g[layer], HEADS_PER_GROUP).reshape(1, GROUP_W) * (
        HEAD_NORM_GAIN * ATTN_SCALE * LOG2_E)
    kg = jnp.tile(k_norm_g[layer], HEADS_PER_GROUP).reshape(1, GROUP_W) * HEAD_NORM_GAIN
    wgate = _gate_weights(w_rg[layer].astype(BF16), w_ig[layer].astype(BF16))
    lru = (conv_w[layer], row2(conv_b), wgate, row2(b_rg), row2(b_ig), row2(lru_lambda))
    post_f32 = tuple(w[layer] for w in (w_o_attn, w_o_rnn, w_out, w_ffn_in, w_ffn_out))

    xs = x_sample[:, 0]
    proj, w_in_b = _sample_inproj(xs, g1, w_in[layer], qg, kg)
    qkvs = proj[:, :OFF_XB].reshape(nb, 3 * N_GROUPS, GROUP_W).transpose(1, 0, 2)
    qs, ks, vs = qkvs[:N_GROUPS], qkvs[N_GROUPS:2 * N_GROUPS], qkvs[2 * N_GROUPS:]
    xbs, gbs, gas, gb2s = (proj[:, a:b] for a, b in
                           ((OFF_XB, OFF_GB), (OFF_GB, OFF_GA), (OFF_GA, OFF_GB2), (OFF_GB2, proj.shape[1])))
    caches = [jnp.transpose(c[layer], (0, 2, 3, 4, 1)).reshape(nb, 2 * GROUP_W, c.shape[2])
              for c in (cache_kv_w128, cache_kv_w512, cache_kv_w2048)]

    xp = x_prompt[0]
    *qkv, hb, conv_rows, h_last, ga, gb2, kvp0, kvp1, kvp2, woa, wor, wout, wfi, wfo = _inproj(
        xp, g1, w_in_b, qg, kg, lru, post_f32, tm=PROMPT_ROW_TILE,
        tails=tuple(min(w, seq) for w, _ in GROUPS), dilations=DILATIONS)
    post_w = (expand, b_gate[layer], woa, wor, wout, g2, wfi, wfo)
    parts, attn_s, first_row = [], [], 0
    for g, d in enumerate(DILATIONS):
        nbs = min(2 * ATTN_BLOCKS_PER_STEP, qkv[3 * g].shape[0] // BLOCK)
        *part, = _attn_prompt(*qkv[3 * g:3 * g + 3], g, d, (qs, ks, vs, caches, first_row), nbs)
        parts.append(part[:2])
        attn_s += part[2:]
        first_row += part[2].shape[0]
    attn_s = jnp.concatenate(attn_s, axis=0)
    assert attn_s.shape == (nb, GROUP_W)
    y_p = _post(parts, DILATIONS, hb, ga, gb2, xp, *post_w, tm=PROMPT_ROW_TILE)
    conv_p = conv_rows[SUBLANES_V7X - 1::SUBLANES_V7X]

    hbs, conv_s, h_s = _rglru_sample(xbs, gbs, state_conv[layer].reshape(nb, -1), state_h[layer], *lru)
    y_s = _post([attn_s], (1,), hbs, gas, gb2s, xs, *post_w, tm=nb)

    kv_prompt = [t.reshape(1, 1, t.shape[0], 2, HEADS_PER_GROUP, HEAD_DIM) for t in (kvp0, kvp1, kvp2)]
    kv_sample = [jnp.stack([ks[g], vs[g]], axis=1).reshape(1, nb, 1, 2, HEADS_PER_GROUP, HEAD_DIM)
                 for g in range(N_GROUPS)]
    return (y_p[None], y_s[:, None],
            kv_prompt[0], kv_prompt[1], kv_prompt[2],
            conv_p[None, None], h_last[None],
            kv_sample[0], kv_sample[1], kv_sample[2],
            conv_s.reshape(1, nb, CONV_W - 1, D_RNN), h_s[None])
```
